```python
import math
import jax
import jax.numpy as jnp
from jax import lax
import numpy as np

D_MODEL = 1024
BATCH = 2
SEQ = 8192
DEPTH = 4
DEC_BATCH = 128
DEC_SEQ = 8
PAST_LEN = 8192
PAGE_SIZE = 128

HEAD_DIM = 64
ATTN_HEADS = D_MODEL // 128
KV_HEADS = max(1, ATTN_HEADS // 4)
Q_PER_KV = ATTN_HEADS // KV_HEADS
ATTN_DIM = ATTN_HEADS * HEAD_DIM
KV_DIM = KV_HEADS * HEAD_DIM
WINDOW = 128
ATTN_BLOCK = WINDOW
MEM_HEADS = 4
MEM_DIM = MEM_HEADS * HEAD_DIM
N_MEM = 256
CONV_DIM = D_MODEL - ATTN_DIM - MEM_DIM
CONV_W = 3
CONV_BUF = CONV_W - 1
MIX_DIM = CONV_DIM + ATTN_DIM + MEM_DIM
IN_DIM = 4 * CONV_DIM + 2 * ATTN_DIM + 2 * KV_DIM + 2 * MEM_DIM
RMS_EPS = 1e-6

kernel_name = "hymba_conv_swa_sink_memxattn_step"


def _rmsnorm(x, g):
    xf = x.astype(jnp.float32)
    r = lax.rsqrt(jnp.mean(xf * xf, axis=-1, keepdims=True) + RMS_EPS)
    return (xf * r).astype(x.dtype) * g


def _split_cols(z):
    sizes = (CONV_DIM, CONV_DIM, CONV_DIM, CONV_DIM,
             ATTN_DIM, KV_DIM, KV_DIM, ATTN_DIM, MEM_DIM, MEM_DIM)
    idx, acc = [], 0
    for s in sizes[:-1]:
        acc += s
        idx.append(acc)
    return jnp.split(z, idx, axis=-1)


def _sink_attend(q, k, v, sink, mask):
    s = jnp.einsum('...qkgd,...skd->...kgqs', q, k).astype(jnp.float32) * (HEAD_DIM ** -0.5)
    s = jnp.where(mask, s, -jnp.inf)
    sk = sink.astype(jnp.float32)[:, :, None]
    m = jnp.maximum(jnp.max(s, axis=-1), sk)
    p = jnp.exp(s - m[..., None])
    p = p / (jnp.sum(p, axis=-1) + jnp.exp(sk - m))[..., None]
    return jnp.einsum('...kgqs,...skd->...qkgd', p.astype(v.dtype), v)


def _window_prompt(q, k, v, sink):
    b, s = q.shape[0], q.shape[1]
    nb = s // ATTN_BLOCK
    qb = q.reshape(b, nb, ATTN_BLOCK, KV_HEADS, Q_PER_KV, HEAD_DIM)
    kb = k.reshape(b, nb, ATTN_BLOCK, KV_HEADS, HEAD_DIM)
    vb = v.reshape(b, nb, ATTN_BLOCK, KV_HEADS, HEAD_DIM)
    kk = jnp.concatenate([jnp.concatenate([jnp.zeros_like(kb[:, :1]), kb[:, :-1]], axis=1), kb], axis=2)
    vv = jnp.concatenate([jnp.concatenate([jnp.zeros_like(vb[:, :1]), vb[:, :-1]], axis=1), vb], axis=2)
    a = jnp.arange(ATTN_BLOCK)[:, None]
    j = jnp.arange(2 * ATTN_BLOCK)[None, :]
    diff = a + ATTN_BLOCK - j
    blk = jnp.arange(nb)[:, None, None]
    valid_key = (blk * ATTN_BLOCK + j[None] - ATTN_BLOCK) >= 0
    mask = (diff >= 0)[None] & (diff < WINDOW)[None] & valid_key
    o = _sink_attend(qb, kk, vv, sink, mask[None, :, None, None])
    return o.reshape(b, s, ATTN_DIM)


def _window_sample(q, k, v, buf_k, buf_v, sink):
    n, t = q.shape[0], q.shape[1]
    kk = jnp.concatenate([buf_k, k], axis=1)
    vv = jnp.concatenate([buf_v, v], axis=1)
    i = jnp.arange(t)[:, None]
    j = jnp.arange(WINDOW + t)[None, :]
    diff = i + WINDOW - j
    mask = (diff >= 0) & (diff < WINDOW)
    o = _sink_attend(q, kk, vv, sink, mask)
    return o.reshape(n, t, ATTN_DIM), kk[:, -WINDOW:], vv[:, -WINDOW:]


def _mem_attend(q, mk, mv):
    s = jnp.einsum('nthd,nmhd->nhtm', q, mk).astype(jnp.float32) * (HEAD_DIM ** -0.5)
    p = jax.nn.softmax(s, axis=-1)
    o = jnp.einsum('nhtm,nmhd->nthd', p.astype(mv.dtype), mv)
    return o.reshape(q.shape[0], q.shape[1], MEM_DIM)


def _layer(x, conv_buf, buf_k, buf_v, mem_k, mem_v, g_pre, g_post, w_in, conv_w, sink, w_out, prompt):
    n, t = x.shape[0], x.shape[1]
    h = _rmsnorm(x, g_pre)
    z = h @ w_in
    cb, cc, ch, cg, q, k, v, ag, mq, mg = _split_cols(z)
    u = cc * ch
    up = jnp.concatenate([conv_buf, u], axis=1)
    conv = conv_w[0] * up[:, 0:t] + conv_w[1] * up[:, 1:t + 1] + conv_w[2] * up[:, 2:t + 2]
    out_a = jax.nn.silu(cg) * cb * conv
    new_conv = up[:, -CONV_BUF:]
    q = q.reshape(n, t, KV_HEADS, Q_PER_KV, HEAD_DIM)
    k = k.reshape(n, t, KV_HEADS, HEAD_DIM)
    v = v.reshape(n, t, KV_HEADS, HEAD_DIM)
    sink_g = sink.reshape(KV_HEADS, Q_PER_KV)
    if prompt:
        o_b = _window_prompt(q, k, v, sink_g)
        new_k, new_v = k[:, -WINDOW:], v[:, -WINDOW:]
    else:
        o_b, new_k, new_v = _window_sample(q, k, v, buf_k, buf_v, sink_g)
    out_b = jax.nn.silu(ag) * o_b
    o_c = _mem_attend(mq.reshape(n, t, MEM_HEADS, HEAD_DIM), mem_k, mem_v)
    out_c = jax.nn.silu(mg) * o_c
    y = jnp.concatenate([out_a, out_b, out_c], axis=-1) @ w_out
    return x + _rmsnorm(y, g_post), new_conv, new_k, new_v


def setup_inputs(seed: int = 0) -> dict:
    key = jax.random.key(seed)
    ks = jax.random.split(key, 20)
    f32 = jnp.float32
    nrm = lambda k, shp, sc: jax.random.normal(k, shp, f32) * sc
    return {
        "x_prompt": nrm(ks[0], (BATCH, SEQ, D_MODEL), 1.0),
        "x_sample": nrm(ks[1], (DEC_BATCH, DEC_SEQ, D_MODEL), 1.0),
        "mem_prompt": nrm(ks[2], (BATCH, N_MEM, D_MODEL), 1.0),
        "cache_win_k": nrm(ks[3], (DEPTH, DEC_BATCH, WINDOW, KV_HEADS, HEAD_DIM), 1.0),
        "cache_win_v": nrm(ks[4], (DEPTH, DEC_BATCH, WINDOW, KV_HEADS, HEAD_DIM), 1.0),
        "state_conv": nrm(ks[5], (DEPTH, DEC_BATCH, CONV_BUF, CONV_DIM), 1.0),
        "cache_mem_k": nrm(ks[6], (DEPTH, DEC_BATCH, N_MEM, MEM_HEADS, HEAD_DIM), 1.0),
        "cache_mem_v": nrm(ks[7], (DEPTH, DEC_BATCH, N_MEM, MEM_HEADS, HEAD_DIM), 1.0),
        "norm_pre": 1.0 + nrm(ks[8], (DEPTH, D_MODEL), 0.05),
        "norm_post": 1.0 + nrm(ks[9], (DEPTH, D_MODEL), 0.05),
        "norm_mem": 1.0 + nrm(ks[10], (DEPTH, D_MODEL), 0.05),
        "w_in": nrm(ks[11], (DEPTH, D_MODEL, IN_DIM), D_MODEL ** -0.5),
        "conv_w": nrm(ks[12], (DEPTH, CONV_W, CONV_DIM), CONV_W ** -0.5),
        "attn_sinks": nrm(ks[13], (DEPTH, ATTN_HEADS), 0.5),
        "w_mem_kv": nrm(ks[14], (DEPTH, D_MODEL, 2 * MEM_DIM), D_MODEL ** -0.5),
        "w_out": nrm(ks[15], (DEPTH, MIX_DIM, D_MODEL), MIX_DIM ** -0.5),
    }


def reference(x_prompt, x_sample, mem_prompt, cache_win_k, cache_win_v, state_conv,
              cache_mem_k, cache_mem_v, norm_pre, norm_post, norm_mem, w_in, conv_w,
              attn_sinks, w_mem_kv, w_out):
    xp, xs = x_prompt, x_sample
    bp, m = mem_prompt.shape[0], mem_prompt.shape[1]
    wkp, wvp, cvp, mkp, mvp, wks, wvs, cvs = [], [], [], [], [], [], [], []
    for l in range(DEPTH):
        mkv = _rmsnorm(mem_prompt, norm_mem[l]) @ w_mem_kv[l]
        mk = mkv[..., :MEM_DIM].reshape(bp, m, MEM_HEADS, HEAD_DIM)
        mv = mkv[..., MEM_DIM:].reshape(bp, m, MEM_HEADS, HEAD_DIM)
        zero_buf = jnp.zeros((xp.shape[0], CONV_BUF, CONV_DIM), xp.dtype)
        xp, cb_p, k_p, v_p = _layer(xp, zero_buf, None, None, mk, mv, norm_pre[l], norm_post[l],
                                    w_in[l], conv_w[l], attn_sinks[l], w_out[l], True)
        xs, cb_s, k_s, v_s = _layer(xs, state_conv[l], cache_win_k[l], cache_win_v[l],
                                    cache_mem_k[l], cache_mem_v[l], norm_pre[l], norm_post[l],
                                    w_in[l], conv_w[l], attn_sinks[l], w_out[l], False)
        wkp.append(k_p); wvp.append(v_p); cvp.append(cb_p); mkp.append(mk); mvp.append(mv)
        wks.append(k_s); wvs.append(v_s); cvs.append(cb_s)
    return (xp, xs, jnp.stack(wkp), jnp.stack(wvp), jnp.stack(cvp), jnp.stack(mkp), jnp.stack(mvp),
            jnp.stack(wks), jnp.stack(wvs), jnp.stack(cvs))
```

```python
import functools

import jax
import jax.numpy as jnp
import numpy as np
from jax import lax
from jax.experimental import pallas as pl
from jax.experimental.pallas import tpu as pltpu

D_MODEL = 1024
DEPTH = 4
HEAD_DIM = 64
ATTN_HEADS = 8
KV_HEADS = 2
Q_PER_KV = ATTN_HEADS // KV_HEADS
ATTN_DIM = ATTN_HEADS * HEAD_DIM
KV_DIM = KV_HEADS * HEAD_DIM
WINDOW = 128
MEM_HEADS = 4
MEM_DIM = MEM_HEADS * HEAD_DIM
CONV_DIM = 256
CONV_W = 3
CONV_BUF = CONV_W - 1
MIX_DIM = CONV_DIM + ATTN_DIM + MEM_DIM
IN_DIM = 4 * CONV_DIM + 2 * ATTN_DIM + 2 * KV_DIM + 2 * MEM_DIM
RMS_EPS = 1e-6
SCALE = HEAD_DIM ** -0.5

LANES = 128
SUBLANES = 8
VMEM_LIMIT = 56 * 1024 * 1024

OFF_CB, OFF_CC, OFF_CH, OFF_CG = 0, 256, 512, 768
OFF_Q, OFF_K, OFF_V, OFF_AG = 1024, 1536, 1664, 1792
OFF_MQ, OFF_MG = 2304, 2560

HEAD_ORDER = (0, 4, 1, 5, 2, 6, 3, 7)
N_CHUNK = ATTN_DIM // LANES
M_CHUNK = MEM_DIM // LANES

PROMPT_TILE = 256
SAMPLE_GROUP = 16
PROJ_ROWS = 256

F32 = jnp.float32
BF16 = jnp.bfloat16


def _rmsnorm(x, g):
    r = lax.rsqrt(jnp.mean(x * x, axis=-1, keepdims=True) + RMS_EPS)
    return (x * r) * g


def _silu(x):
    return x * jax.nn.sigmoid(x)


def _dot(a, b):
    return jnp.dot(a, b, preferred_element_type=F32)


def _dot_nt(a, b):
    return lax.dot_general(a, b, (((1,), (1,)), ((), ())), preferred_element_type=F32)


def _low_lanes(shape):
    return lax.broadcasted_iota(jnp.int32, shape, len(shape) - 1) < HEAD_DIM


def _memkv_kernel(mem_ref, g_ref, w_ref, mk_ref, mv_ref, kcat_t_ref, vcat_ref, *, batch, n_mem):
    h = _rmsnorm(mem_ref[...], g_ref[...]).astype(BF16)
    kv = _dot(h, w_ref[...])
    mk = kv[:, :MEM_DIM]
    mv = kv[:, MEM_DIM:]
    mk_ref[...] = mk
    mv_ref[...] = mv
    low = _low_lanes((n_mem, LANES))
    for b in range(batch):
        for c in range(M_CHUNK):
            kc = mk[b * n_mem:(b + 1) * n_mem, c * LANES:(c + 1) * LANES]
            vc = mv[b * n_mem:(b + 1) * n_mem, c * LANES:(c + 1) * LANES]
            kcat = jnp.concatenate([jnp.where(low, kc, 0.0), jnp.where(low, 0.0, kc)], axis=0)
            vcat = jnp.concatenate([jnp.where(low, vc, 0.0), jnp.where(low, 0.0, vc)], axis=0)
            kcat_t_ref[b, c] = kcat.T.astype(BF16)
            vcat_ref[b, c] = vcat.astype(BF16)


def _memkv(mem2d, norm_mem3, w_mem_bf, batch, n_mem):
    rows = batch * n_mem
    return pl.pallas_call(
        functools.partial(_memkv_kernel, batch=batch, n_mem=n_mem),
        grid=(DEPTH,),
        in_specs=[
            pl.BlockSpec((rows, D_MODEL), lambda l: (0, 0)),
            pl.BlockSpec((None, 1, D_MODEL), lambda l: (l, 0, 0)),
            pl.BlockSpec((None, D_MODEL, 2 * MEM_DIM), lambda l: (l, 0, 0)),
        ],
        out_specs=[
            pl.BlockSpec((None, rows, MEM_DIM), lambda l: (l, 0, 0)),
            pl.BlockSpec((None, rows, MEM_DIM), lambda l: (l, 0, 0)),
            pl.BlockSpec((None, batch, M_CHUNK, LANES, 2 * n_mem), lambda l: (l, 0, 0, 0, 0)),
            pl.BlockSpec((None, batch, M_CHUNK, 2 * n_mem, LANES), lambda l: (l, 0, 0, 0, 0)),
        ],
        out_shape=[
            jax.ShapeDtypeStruct((DEPTH, rows, MEM_DIM), F32),
            jax.ShapeDtypeStruct((DEPTH, rows, MEM_DIM), F32),
            jax.ShapeDtypeStruct((DEPTH, batch, M_CHUNK, LANES, 2 * n_mem), BF16),
            jax.ShapeDtypeStruct((DEPTH, batch, M_CHUNK, 2 * n_mem, LANES), BF16),
        ],
        compiler_params=pltpu.CompilerParams(dimension_semantics=("arbitrary",)),
        name="memkv",
    )(mem2d, norm_mem3, w_mem_bf)


def _prompt_layer_kernel(sink_ref, x_ref, gpre_ref, gpost_ref, win_ref, convw_ref, kcat_t_ref,
                         vcat_ref, wout_ref,
                         xo_ref, klast_ref, vlast_ref, convlast_ref,
                         kprev_ref, vprev_ref, ubuf_ref, *, tile, n_mem):
    i = pl.program_id(1)

    @pl.when(i == 0)
    def _():
        kprev_ref[...] = jnp.zeros_like(kprev_ref)
        vprev_ref[...] = jnp.zeros_like(vprev_ref)
        ubuf_ref[0:SUBLANES, :] = jnp.zeros((SUBLANES, CONV_DIM), F32)

    x = x_ref[...]
    h = _rmsnorm(x, gpre_ref[...]).astype(BF16)
    z = _dot(h, win_ref[...])

    u = z[:, OFF_CC:OFF_CC + CONV_DIM] * z[:, OFF_CH:OFF_CH + CONV_DIM]
    ubuf_ref[SUBLANES:SUBLANES + tile, :] = u
    u1 = ubuf_ref[SUBLANES - 1:SUBLANES - 1 + tile, :]
    u2 = ubuf_ref[SUBLANES - 2:SUBLANES - 2 + tile, :]
    cw = convw_ref[...]
    conv = cw[0:1, :] * u2 + cw[1:2, :] * u1 + cw[2:3, :] * u
    out_a = _silu(z[:, OFF_CG:OFF_CG + CONV_DIM]) * z[:, OFF_CB:OFF_CB + CONV_DIM] * conv
    u_tail = u[tile - SUBLANES:tile, :]
    ubuf_ref[0:SUBLANES, :] = u_tail
    convlast_ref[...] = u_tail

    k = z[:, OFF_K:OFF_K + KV_DIM]
    v = z[:, OFF_V:OFF_V + KV_DIM]
    kfull = jnp.concatenate([kprev_ref[...], k], axis=0)
    vfull = jnp.concatenate([vprev_ref[...], v], axis=0)
    low = _low_lanes((WINDOW + tile, KV_DIM))
    klo = jnp.where(low, kfull, 0.0).astype(BF16)
    khi = jnp.where(low, 0.0, kfull).astype(BF16)
    vlo = jnp.where(low, vfull, 0.0).astype(BF16)
    vhi = jnp.where(low, 0.0, vfull).astype(BF16)
    k_tail = k[tile - WINDOW:tile, :]
    v_tail = v[tile - WINDOW:tile, :]
    kprev_ref[...] = k_tail
    vprev_ref[...] = v_tail
    klast_ref[...] = k_tail
    vlast_ref[...] = v_tail

    q = (z[:, OFF_Q:OFF_Q + ATTN_DIM] * SCALE).astype(BF16)
    rows = N_CHUNK * WINDOW
    qpos = lax.broadcasted_iota(jnp.int32, (rows, 2 * WINDOW), 0) % WINDOW
    kpos = lax.broadcasted_iota(jnp.int32, (rows, 2 * WINDOW), 1)
    band = (kpos > qpos) & (kpos <= qpos + WINDOW)
    band_first = band & ((kpos >= WINDOW) | (i > 0))
    chunk_of_row = lax.broadcasted_iota(jnp.int32, (rows, 1), 0) // WINDOW
    sink_lo = jnp.zeros((rows, 1), F32)
    sink_hi = jnp.zeros((rows, 1), F32)
    for c in range(N_CHUNK):
        sink_lo = jnp.where(chunk_of_row == c, sink_ref[c], sink_lo)
        sink_hi = jnp.where(chunk_of_row == c, sink_ref[N_CHUNK + c], sink_hi)
    low_o = _low_lanes((rows, LANES))
    o_blocks = []
    for j in range(tile // WINDOW):
        r0 = j * WINDOW
        q_all = jnp.concatenate([q[r0:r0 + WINDOW, c * LANES:(c + 1) * LANES] for c in range(N_CHUNK)], axis=0)
        k_cat = jnp.concatenate([klo[r0:r0 + 2 * WINDOW], khi[r0:r0 + 2 * WINDOW]], axis=0)
        v_cat = jnp.concatenate([vlo[r0:r0 + 2 * WINDOW], vhi[r0:r0 + 2 * WINDOW]], axis=0)
        s = _dot_nt(q_all, k_cat)
        mask = band_first if j == 0 else band
        s_lo = jnp.where(mask, s[:, :2 * WINDOW], -jnp.inf)
        s_hi = jnp.where(mask, s[:, 2 * WINDOW:], -jnp.inf)
        m_lo = jnp.maximum(jnp.max(s_lo, axis=1, keepdims=True), sink_lo)
        m_hi = jnp.maximum(jnp.max(s_hi, axis=1, keepdims=True), sink_hi)
        p_lo = jnp.exp(s_lo - m_lo)
        p_hi = jnp.exp(s_hi - m_hi)
        d_lo = jnp.sum(p_lo, axis=1, keepdims=True) + jnp.exp(sink_lo - m_lo)
        d_hi = jnp.sum(p_hi, axis=1, keepdims=True) + jnp.exp(sink_hi - m_hi)
        p = jnp.concatenate([p_lo, p_hi], axis=1).astype(BF16)
        o = _dot(p, v_cat)
        o = o * jnp.where(low_o, 1.0 / d_lo, 1.0 / d_hi)
        o_blocks.append(jnp.concatenate([o[c * WINDOW:(c + 1) * WINDOW] for c in range(N_CHUNK)], axis=1))
    o_b = jnp.concatenate(o_blocks, axis=0) if len(o_blocks) > 1 else o_blocks[0]
    out_b = _silu(z[:, OFF_AG:OFF_AG + ATTN_DIM]) * o_b

    mq = (z[:, OFF_MQ:OFF_MQ + MEM_DIM] * SCALE).astype(BF16)
    low_t = _low_lanes((tile, LANES))
    oc_chunks = []
    for c in range(M_CHUNK):
        s = _dot(mq[:, c * LANES:(c + 1) * LANES], kcat_t_ref[c])
        s0 = s[:, :n_mem]
        s1 = s[:, n_mem:]
        p0 = jnp.exp(s0 - jnp.max(s0, axis=1, keepdims=True))
        p1 = jnp.exp(s1 - jnp.max(s1, axis=1, keepdims=True))
        d0 = jnp.sum(p0, axis=1, keepdims=True)
        d1 = jnp.sum(p1, axis=1, keepdims=True)
        p = jnp.concatenate([p0, p1], axis=1).astype(BF16)
        o = _dot(p, vcat_ref[c])
        oc_chunks.append(o * jnp.where(low_t, 1.0 / d0, 1.0 / d1))
    o_c = jnp.concatenate(oc_chunks, axis=1)
    out_c = _silu(z[:, OFF_MG:OFF_MG + MEM_DIM]) * o_c

    mix = jnp.concatenate([out_a, out_b, out_c], axis=1).astype(BF16)
    y = _dot(mix, wout_ref[...])
    xo_ref[...] = x + _rmsnorm(y, gpost_ref[...])


def _prompt_layer(l, x, sinks_perm, gpre3, gpost3, w_in_bf, conv_w, kcat_t, vcat, w_out_bf, n_mem):
    batch, seq, _ = x.shape
    tile = PROMPT_TILE
    nt = seq // tile
    const2 = lambda b, i: (0, 0)
    return pl.pallas_call(
        functools.partial(_prompt_layer_kernel, tile=tile, n_mem=n_mem),
        grid=(batch, nt),
        in_specs=[
            pl.BlockSpec(memory_space=pltpu.SMEM),
            pl.BlockSpec((None, tile, D_MODEL), lambda b, i: (b, i, 0)),
            pl.BlockSpec((None, 1, D_MODEL), lambda b, i: (l, 0, 0)),
            pl.BlockSpec((None, 1, D_MODEL), lambda b, i: (l, 0, 0)),
            pl.BlockSpec((None, D_MODEL, IN_DIM), lambda b, i: (l, 0, 0)),
            pl.BlockSpec((None, CONV_W, CONV_DIM), lambda b, i: (l, 0, 0)),
            pl.BlockSpec((None, None, M_CHUNK, LANES, 2 * n_mem), lambda b, i: (l, b, 0, 0, 0)),
            pl.BlockSpec((None, None, M_CHUNK, 2 * n_mem, LANES), lambda b, i: (l, b, 0, 0, 0)),
            pl.BlockSpec((None, MIX_DIM, D_MODEL), lambda b, i: (l, 0, 0)),
        ],
        out_specs=[
            pl.BlockSpec((None, tile, D_MODEL), lambda b, i: (b, i, 0)),
            pl.BlockSpec((None, WINDOW, KV_DIM), lambda b, i: (b, 0, 0)),
            pl.BlockSpec((None, WINDOW, KV_DIM), lambda b, i: (b, 0, 0)),
            pl.BlockSpec((None, SUBLANES, CONV_DIM), lambda b, i: (b, 0, 0)),
        ],
        out_shape=[
            jax.ShapeDtypeStruct((batch, seq, D_MODEL), F32),
            jax.ShapeDtypeStruct((batch, WINDOW, KV_DIM), F32),
            jax.ShapeDtypeStruct((batch, WINDOW, KV_DIM), F32),
            jax.ShapeDtypeStruct((batch, SUBLANES, CONV_DIM), F32),
        ],
        scratch_shapes=[
            pltpu.VMEM((WINDOW, KV_DIM), F32),
            pltpu.VMEM((WINDOW, KV_DIM), F32),
            pltpu.VMEM((SUBLANES + tile, CONV_DIM), F32),
        ],
        compiler_params=pltpu.CompilerParams(
            dimension_semantics=("arbitrary", "arbitrary"), vmem_limit_bytes=VMEM_LIMIT),
        name="prompt_layer",
    )(sinks_perm[l], x, gpre3, gpost3, w_in_bf, conv_w, kcat_t, vcat, w_out_bf)


def _proj_in_kernel(x_ref, g_ref, w_ref, z_ref):
    h = _rmsnorm(x_ref[...], g_ref[...]).astype(BF16)
    z_ref[...] = _dot(h, w_ref[...])


def _proj_in(l, x2d, gpre3, w_in_bf):
    rows = x2d.shape[0]
    return pl.pallas_call(
        _proj_in_kernel,
        grid=(rows // PROJ_ROWS,),
        in_specs=[
            pl.BlockSpec((PROJ_ROWS, D_MODEL), lambda r: (r, 0)),
            pl.BlockSpec((None, 1, D_MODEL), lambda r: (l, 0, 0)),
            pl.BlockSpec((None, D_MODEL, IN_DIM), lambda r: (l, 0, 0)),
        ],
        out_specs=pl.BlockSpec((PROJ_ROWS, IN_DIM), lambda r: (r, 0)),
        out_shape=jax.ShapeDtypeStruct((rows, IN_DIM), F32),
        compiler_params=pltpu.CompilerParams(
            dimension_semantics=("arbitrary",), vmem_limit_bytes=VMEM_LIMIT),
        name="sample_proj_in",
    )(x2d, gpre3, w_in_bf)


def _proj_out_kernel(x_ref, mix_ref, g_ref, w_ref, xo_ref):
    y = _dot(mix_ref[...].astype(BF16), w_ref[...])
    xo_ref[...] = x_ref[...] + _rmsnorm(y, g_ref[...])


def _proj_out(l, x2d, mix, gpost3, w_out_bf):
    rows = x2d.shape[0]
    return pl.pallas_call(
        _proj_out_kernel,
        grid=(rows // PROJ_ROWS,),
        in_specs=[
            pl.BlockSpec((PROJ_ROWS, D_MODEL), lambda r: (r, 0)),
            pl.BlockSpec((PROJ_ROWS, MIX_DIM), lambda r: (r, 0)),
            pl.BlockSpec((None, 1, D_MODEL), lambda r: (l, 0, 0)),
            pl.BlockSpec((None, MIX_DIM, D_MODEL), lambda r: (l, 0, 0)),
        ],
        out_specs=pl.BlockSpec((PROJ_ROWS, D_MODEL), lambda r: (r, 0)),
        out_shape=jax.ShapeDtypeStruct((rows, D_MODEL), F32),
        compiler_params=pltpu.CompilerParams(
            dimension_semantics=("arbitrary",), vmem_limit_bytes=VMEM_LIMIT),
        name="sample_proj_out",
    )(x2d, mix, gpost3, w_out_bf)


def _sample_mix_kernel(z_ref, convp_ref, wk_ref, wv_ref, mk_ref, mv_ref, convw_ref, sink_ref,
                       mix_ref, wko_ref, wvo_ref, uo_ref, *, group, t, n_mem):
    z = z_ref[...]

    def per_seq(a):
        return a.reshape(group, t, a.shape[-1])

    u = per_seq(z[:, OFF_CC:OFF_CC + CONV_DIM] * z[:, OFF_CH:OFF_CH + CONV_DIM])
    prev = convp_ref[...]
    tpos = lax.broadcasted_iota(jnp.int32, (group, t, CONV_DIM), 1)
    u1 = jnp.where(tpos >= 1, pltpu.roll(u, 1, 1), pltpu.roll(prev, 1, 1))
    u2 = jnp.where(tpos >= 2, pltpu.roll(u, 2, 1), pltpu.roll(prev, 2, 1))
    cw = convw_ref[...]
    conv = cw[0:1, :] * u2 + cw[1:2, :] * u1 + cw[2:3, :] * u
    out_a = (per_seq(_silu(z[:, OFF_CG:OFF_CG + CONV_DIM]) * z[:, OFF_CB:OFF_CB + CONV_DIM]) * conv)
    uo_ref[...] = u

    k_new = per_seq(z[:, OFF_K:OFF_K + KV_DIM])
    v_new = per_seq(z[:, OFF_V:OFF_V + KV_DIM])
    k_old = wk_ref[...]
    v_old = wv_ref[...]
    wko_ref[...] = jnp.concatenate([k_old[:, t:, :], k_new], axis=1)
    wvo_ref[...] = jnp.concatenate([v_old[:, t:, :], v_new], axis=1)
    pad = jnp.zeros((group, WINDOW - t, KV_DIM), F32)
    kk = jnp.concatenate([k_old, k_new, pad], axis=1).astype(BF16)
    vv = jnp.concatenate([v_old, v_new, pad], axis=1).astype(BF16)
    q = per_seq(z[:, OFF_Q:OFF_Q + ATTN_DIM] * SCALE)
    low_q = _low_lanes((group, t, LANES))
    pieces = []
    for c in range(N_CHUNK):
        qc = q[:, :, c * LANES:(c + 1) * LANES]
        pieces.append(jnp.where(low_q, qc, 0.0))
        pieces.append(jnp.where(low_q, 0.0, qc))
    qbd = jnp.concatenate(pieces, axis=1).astype(BF16)
    nrow = 2 * N_CHUNK * t
    s = jnp.einsum('gqd,gkd->gqk', qbd, kk, preferred_element_type=F32)
    qpos = lax.broadcasted_iota(jnp.int32, (group, nrow, 2 * WINDOW), 1) % t
    kpos = lax.broadcasted_iota(jnp.int32, (group, nrow, 2 * WINDOW), 2)
    s = jnp.where((kpos > qpos) & (kpos <= qpos + WINDOW), s, -jnp.inf)
    sink = sink_ref[...]
    m = jnp.maximum(jnp.max(s, axis=2, keepdims=True), sink)
    p = jnp.exp(s - m)
    d = jnp.sum(p, axis=2, keepdims=True) + jnp.exp(sink - m)
    o = jnp.einsum('gqk,gkd->gqd', p.astype(BF16), vv, preferred_element_type=F32)
    o = o * (1.0 / d)
    o_b = jnp.concatenate(
        [jnp.where(low_q, o[:, 2 * c * t:(2 * c + 1) * t], o[:, (2 * c + 1) * t:(2 * c + 2) * t])
         for c in range(N_CHUNK)], axis=2)
    out_b = per_seq(_silu(z[:, OFF_AG:OFF_AG + ATTN_DIM])) * o_b

    mq = per_seq(z[:, OFF_MQ:OFF_MQ + MEM_DIM] * SCALE)
    head_of_lane = lax.broadcasted_iota(jnp.int32, (group, t, MEM_DIM), 2) // HEAD_DIM
    qm = jnp.concatenate([jnp.where(head_of_lane == hh, mq, 0.0) for hh in range(MEM_HEADS)],
                         axis=1).astype(BF16)
    s = jnp.einsum('gqd,gkd->gqk', qm, mk_ref[...].astype(BF16), preferred_element_type=F32)
    p = jnp.exp(s - jnp.max(s, axis=2, keepdims=True))
    d = jnp.sum(p, axis=2, keepdims=True)
    o = jnp.einsum('gqk,gkd->gqd', p.astype(BF16), mv_ref[...].astype(BF16), preferred_element_type=F32)
    o = o * (1.0 / d)
    o_c = jnp.zeros((group, t, MEM_DIM), F32)
    for hh in range(MEM_HEADS):
        o_c = jnp.where(head_of_lane == hh, o[:, hh * t:(hh + 1) * t], o_c)
    out_c = per_seq(_silu(z[:, OFF_MG:OFF_MG + MEM_DIM])) * o_c

    mix = jnp.concatenate([out_a, out_b, out_c], axis=2)
    mix_ref[...] = mix.reshape(group * t, MIX_DIM)


def _sample_mix(l, z, convp, cache_k, cache_v, cache_mk, cache_mv, conv_w, sink_rows, n_seq, t, n_mem):
    group = SAMPLE_GROUP
    nrow = 2 * N_CHUNK * t
    return pl.pallas_call(
        functools.partial(_sample_mix_kernel, group=group, t=t, n_mem=n_mem),
        grid=(n_seq // group,),
        in_specs=[
            pl.BlockSpec((group * t, IN_DIM), lambda g: (g, 0)),
            pl.BlockSpec((None, group, t, CONV_DIM), lambda g: (l, g, 0, 0)),
            pl.BlockSpec((None, group, WINDOW, KV_DIM), lambda g: (l, g, 0, 0)),
            pl.BlockSpec((None, group, WINDOW, KV_DIM), lambda g: (l, g, 0, 0)),
            pl.BlockSpec((None, group, n_mem, MEM_DIM), lambda g: (l, g, 0, 0)),
            pl.BlockSpec((None, group, n_mem, MEM_DIM), lambda g: (l, g, 0, 0)),
            pl.BlockSpec((None, CONV_W, CONV_DIM), lambda g: (l, 0, 0)),
            pl.BlockSpec((None, nrow, 1), lambda g: (l, 0, 0)),
        ],
        out_specs=[
            pl.BlockSpec((group * t, MIX_DIM), lambda g: (g, 0)),
            pl.BlockSpec((group, WINDOW, KV_DIM), lambda g: (g, 0, 0)),
            pl.BlockSpec((group, WINDOW, KV_DIM), lambda g: (g, 0, 0)),
            pl.BlockSpec((group, t, CONV_DIM), lambda g: (g, 0, 0)),
        ],
        out_shape=[
            jax.ShapeDtypeStruct((n_seq * t, MIX_DIM), F32),
            jax.ShapeDtypeStruct((n_seq, WINDOW, KV_DIM), F32),
            jax.ShapeDtypeStruct((n_seq, WINDOW, KV_DIM), F32),
            jax.ShapeDtypeStruct((n_seq, t, CONV_DIM), F32),
        ],
        compiler_params=pltpu.CompilerParams(
            dimension_semantics=("arbitrary",), vmem_limit_bytes=VMEM_LIMIT),
        name="sample_mix",
    )(z, convp, cache_k, cache_v, cache_mk, cache_mv, conv_w, sink_rows)


def _head_perm_cols(offset):
    return np.concatenate([offset + h * HEAD_DIM + np.arange(HEAD_DIM) for h in HEAD_ORDER])


def kernel(x_prompt, x_sample, mem_prompt, cache_win_k, cache_win_v, state_conv, cache_mem_k, cache_mem_v,
           norm_pre, norm_post, norm_mem, w_in, conv_w, attn_sinks, w_mem_kv, w_out):
    batch, seq, _ = x_prompt.shape
    n_seq, t, _ = x_sample.shape
    n_mem = mem_prompt.shape[1]
    assert seq % PROMPT_TILE == 0 and PROMPT_TILE % WINDOW == 0
    assert n_seq % SAMPLE_GROUP == 0 and t == SUBLANES and (n_seq * t) % PROJ_ROWS == 0

    in_cols = np.arange(IN_DIM)
    in_cols[OFF_Q:OFF_Q + ATTN_DIM] = _head_perm_cols(OFF_Q)
    in_cols[OFF_AG:OFF_AG + ATTN_DIM] = _head_perm_cols(OFF_AG)
    out_rows = np.arange(MIX_DIM)
    out_rows[CONV_DIM:CONV_DIM + ATTN_DIM] = _head_perm_cols(CONV_DIM)
    w_in_bf = w_in[:, :, in_cols].astype(BF16)
    w_out_bf = w_out[:, out_rows, :].astype(BF16)
    w_mem_bf = w_mem_kv.astype(BF16)
    sinks_perm = attn_sinks[:, np.array(HEAD_ORDER)]
    sinks_chunk = attn_sinks
    sink_rows = jnp.repeat(sinks_perm, t, axis=1)[:, :, None]
    gpre3 = norm_pre[:, None, :]
    gpost3 = norm_post[:, None, :]
    gmem3 = norm_mem[:, None, :]

    mk_all, mv_all, kcat_t, vcat = _memkv(mem_prompt.reshape(batch * n_mem, D_MODEL), gmem3, w_mem_bf,
                                          batch, n_mem)

    cache_k = cache_win_k.reshape(DEPTH, n_seq, WINDOW, KV_DIM)
    cache_v = cache_win_v.reshape(DEPTH, n_seq, WINDOW, KV_DIM)
    cache_mk = cache_mem_k.reshape(DEPTH, n_seq, n_mem, MEM_DIM)
    cache_mv = cache_mem_v.reshape(DEPTH, n_seq, n_mem, MEM_DIM)
    convp = jnp.pad(state_conv, ((0, 0), (0, 0), (t - CONV_BUF, 0), (0, 0)))

    xp = x_prompt
    xs = x_sample.reshape(n_seq * t, D_MODEL)
    wkp, wvp, cvp, wks, wvs, cvs = [], [], [], [], [], []
    for l in range(DEPTH):
        xp, k_p, v_p, c_p = _prompt_layer(l, xp, sinks_chunk, gpre3, gpost3, w_in_bf, conv_w, kcat_t, vcat,
                                          w_out_bf, n_mem)
        z = _proj_in(l, xs, gpre3, w_in_bf)
        mix, k_s, v_s, u_s = _sample_mix(l, z, convp, cache_k, cache_v, cache_mk, cache_mv, conv_w, sink_rows,
                                         n_seq, t, n_mem)
        xs = _proj_out(l, xs, mix, gpost3, w_out_bf)
        wkp.append(k_p)
        wvp.append(v_p)
        cvp.append(c_p[:, SUBLANES - CONV_BUF:, :])
        wks.append(k_s)
        wvs.append(v_s)
        cvs.append(u_s[:, t - CONV_BUF:, :])

    kv5 = lambda a, n: a.reshape(DEPTH, n, WINDOW, KV_HEADS, HEAD_DIM)
    return (xp,
            xs.reshape(n_seq, t, D_MODEL),
            kv5(jnp.stack(wkp), batch),
            kv5(jnp.stack(wvp), batch),
            jnp.stack(cvp),
            mk_all.reshape(DEPTH, batch, n_mem, MEM_HEADS, HEAD_DIM),
            mv_all.reshape(DEPTH, batch, n_mem, MEM_HEADS, HEAD_DIM),
            kv5(jnp.stack(wks), n_seq),
            kv5(jnp.stack(wvs), n_seq),
            jnp.stack(cvs))
```

```python
import functools

import jax
import jax.numpy as jnp
from jax import lax
from jax.experimental import pallas as pl
from jax.experimental.pallas import tpu as pltpu

D_MODEL = 1024
DEPTH = 4
HEAD_DIM = 64
ATTN_HEADS = 8
KV_HEADS = 2
ATTN_DIM = ATTN_HEADS * HEAD_DIM
KV_DIM = KV_HEADS * HEAD_DIM
WINDOW = 128
MEM_HEADS = 4
MEM_DIM = MEM_HEADS * HEAD_DIM
CONV_DIM = 256
CONV_W = 3
CONV_BUF = CONV_W - 1
MIX_DIM = CONV_DIM + ATTN_DIM + MEM_DIM
IN_DIM = 4 * CONV_DIM + 2 * ATTN_DIM + 2 * KV_DIM + 2 * MEM_DIM
RMS_EPS = 1e-6
SCALE = HEAD_DIM ** -0.5

LANES = 128
SUBLANES = 8
VMEM_LIMIT = 56 * 1024 * 1024

OFF_CB, OFF_CC, OFF_CH, OFF_CG = 0, 256, 512, 768
OFF_Q, OFF_K, OFF_V, OFF_AG = 1024, 1536, 1664, 1792
OFF_MQ, OFF_MG = 2304, 2560

N_CHUNK = ATTN_DIM // LANES
M_CHUNK = MEM_DIM // LANES
CHUNKS_PER_KV = N_CHUNK // KV_HEADS

PROMPT_TILE = 256
SAMPLE_GROUP = 16

F32 = jnp.float32
BF16 = jnp.bfloat16


def _rmsnorm(x, g):
    r = lax.rsqrt(jnp.mean(x * x, axis=-1, keepdims=True) + RMS_EPS)
    return (x * r) * g


def _silu(x):
    return x * jax.nn.sigmoid(x)


def _dot(a, b):
    return jnp.dot(a, b, preferred_element_type=F32)


def _dot_nt(a, b):
    return lax.dot_general(a, b, (((1,), (1,)), ((), ())), preferred_element_type=F32)


def _low_lanes(shape):
    return lax.broadcasted_iota(jnp.int32, shape, len(shape) - 1) < HEAD_DIM


def _swap_halves(a):
    return pltpu.roll(a, HEAD_DIM, a.ndim - 1)


def _memkv_kernel(mem_ref, g_ref, w_ref, mkt_ref, mvt_ref, kcat_t_ref, vcat_ref, *, batch, n_mem):
    h = _rmsnorm(mem_ref[...], g_ref[...]).astype(BF16)
    kv = _dot(h, w_ref[...])
    mk = kv[:, :MEM_DIM]
    mv = kv[:, MEM_DIM:]
    low = _low_lanes((n_mem, LANES))
    for b in range(batch):
        mkt_ref[b] = mk[b * n_mem:(b + 1) * n_mem, :].T
        mvt_ref[b] = mv[b * n_mem:(b + 1) * n_mem, :].T
        for c in range(M_CHUNK):
            kc = mk[b * n_mem:(b + 1) * n_mem, c * LANES:(c + 1) * LANES]
            vc = mv[b * n_mem:(b + 1) * n_mem, c * LANES:(c + 1) * LANES]
            kcat = jnp.concatenate([jnp.where(low, kc, 0.0), jnp.where(low, 0.0, kc)], axis=0)
            vcat = jnp.concatenate([jnp.where(low, vc, 0.0), jnp.where(low, 0.0, vc)], axis=0)
            kcat_t_ref[b, c] = kcat.T.astype(BF16)
            vcat_ref[b, c] = vcat.astype(BF16)


def _memkv(mem2d, norm_mem3, w_mem_bf, batch, n_mem):
    rows = batch * n_mem
    return pl.pallas_call(
        functools.partial(_memkv_kernel, batch=batch, n_mem=n_mem),
        grid=(DEPTH,),
        in_specs=[
            pl.BlockSpec((rows, D_MODEL), lambda l: (0, 0)),
            pl.BlockSpec((None, 1, D_MODEL), lambda l: (l, 0, 0)),
            pl.BlockSpec((None, D_MODEL, 2 * MEM_DIM), lambda l: (l, 0, 0)),
        ],
        out_specs=[
            pl.BlockSpec((None, batch, MEM_DIM, n_mem), lambda l: (l, 0, 0, 0)),
            pl.BlockSpec((None, batch, MEM_DIM, n_mem), lambda l: (l, 0, 0, 0)),
            pl.BlockSpec((None, batch, M_CHUNK, LANES, 2 * n_mem), lambda l: (l, 0, 0, 0, 0)),
            pl.BlockSpec((None, batch, M_CHUNK, 2 * n_mem, LANES), lambda l: (l, 0, 0, 0, 0)),
        ],
        out_shape=[
            jax.ShapeDtypeStruct((DEPTH, batch, MEM_DIM, n_mem), F32),
            jax.ShapeDtypeStruct((DEPTH, batch, MEM_DIM, n_mem), F32),
            jax.ShapeDtypeStruct((DEPTH, batch, M_CHUNK, LANES, 2 * n_mem), BF16),
            jax.ShapeDtypeStruct((DEPTH, batch, M_CHUNK, 2 * n_mem, LANES), BF16),
        ],
        compiler_params=pltpu.CompilerParams(dimension_semantics=("arbitrary",)),
        name="memkv",
    )(mem2d, norm_mem3, w_mem_bf)


def _prompt_layer_kernel(sink_ref, x_ref, gpre_ref, gpost_ref, win_ref, convw_ref, kcat_t_ref,
                         vcat_ref, wout_ref,
                         xo_ref, klast_ref, vlast_ref, convlast_ref,
                         kprev_ref, vprev_ref, ubuf_ref, *, tile, n_mem, n_tiles):
    i = pl.program_id(1)

    @pl.when(i == 0)
    def _():
        kprev_ref[...] = jnp.zeros_like(kprev_ref)
        vprev_ref[...] = jnp.zeros_like(vprev_ref)
        ubuf_ref[0:SUBLANES, :] = jnp.zeros((SUBLANES, CONV_DIM), F32)

    x = x_ref[...]
    h = _rmsnorm(x, gpre_ref[...]).astype(BF16)
    z = _dot(h, win_ref[...])

    u = z[:, OFF_CC:OFF_CC + CONV_DIM] * z[:, OFF_CH:OFF_CH + CONV_DIM]
    ubuf_ref[SUBLANES:SUBLANES + tile, :] = u
    u1 = ubuf_ref[SUBLANES - 1:SUBLANES - 1 + tile, :]
    u2 = ubuf_ref[SUBLANES - 2:SUBLANES - 2 + tile, :]
    cw = convw_ref[...]
    conv = cw[0:1, :] * u2 + cw[1:2, :] * u1 + cw[2:3, :] * u
    out_a = _silu(z[:, OFF_CG:OFF_CG + CONV_DIM]) * z[:, OFF_CB:OFF_CB + CONV_DIM] * conv
    u_tail = u[tile - SUBLANES:tile, :]
    ubuf_ref[0:SUBLANES, :] = u_tail
    convlast_ref[...] = u_tail

    k = z[:, OFF_K:OFF_K + KV_DIM]
    v = z[:, OFF_V:OFF_V + KV_DIM]
    kfull = jnp.concatenate([kprev_ref[...], k], axis=0)
    vfull = jnp.concatenate([vprev_ref[...], v], axis=0)
    k_tail = k[tile - WINDOW:tile, :]
    v_tail = v[tile - WINDOW:tile, :]
    kprev_ref[...] = k_tail
    vprev_ref[...] = v_tail

    @pl.when(i == n_tiles - 1)
    def _():
        klast_ref[...] = k_tail.T
        vlast_ref[...] = v_tail.T

    low = _low_lanes((WINDOW + tile, KV_DIM))
    kswap = _swap_halves(kfull)
    vswap = _swap_halves(vfull)
    k_even = (jnp.where(low, kfull, 0.0).astype(BF16), jnp.where(low, kswap, 0.0).astype(BF16))
    k_odd = (jnp.where(low, 0.0, kswap).astype(BF16), jnp.where(low, 0.0, kfull).astype(BF16))
    v_even = (jnp.where(low, vfull, 0.0).astype(BF16), jnp.where(low, vswap, 0.0).astype(BF16))
    v_odd = (jnp.where(low, 0.0, vswap).astype(BF16), jnp.where(low, 0.0, vfull).astype(BF16))

    q = (z[:, OFF_Q:OFF_Q + ATTN_DIM] * SCALE).astype(BF16)
    rows = CHUNKS_PER_KV * WINDOW
    qpos = lax.broadcasted_iota(jnp.int32, (rows, 2 * WINDOW), 0) % WINDOW
    kpos = lax.broadcasted_iota(jnp.int32, (rows, 2 * WINDOW), 1)
    band = (kpos > qpos) & (kpos <= qpos + WINDOW)
    band_first = band & ((kpos >= WINDOW) | (i > 0))
    chunk_of_row = lax.broadcasted_iota(jnp.int32, (rows, 1), 0) // WINDOW
    low_o = _low_lanes((rows, LANES))
    o_blocks = []
    for j in range(tile // WINDOW):
        r0 = j * WINDOW
        mask = band_first if j == 0 else band
        o_chunks = []
        for kvh in range(KV_HEADS):
            c0 = kvh * CHUNKS_PER_KV
            sink_even = jnp.zeros((rows, 1), F32)
            sink_odd = jnp.zeros((rows, 1), F32)
            for cc in range(CHUNKS_PER_KV):
                sink_even = jnp.where(chunk_of_row == cc, sink_ref[2 * (c0 + cc)], sink_even)
                sink_odd = jnp.where(chunk_of_row == cc, sink_ref[2 * (c0 + cc) + 1], sink_odd)
            q_all = jnp.concatenate(
                [q[r0:r0 + WINDOW, (c0 + cc) * LANES:(c0 + cc + 1) * LANES] for cc in range(CHUNKS_PER_KV)],
                axis=0)
            k_cat = jnp.concatenate([k_even[kvh][r0:r0 + 2 * WINDOW], k_odd[kvh][r0:r0 + 2 * WINDOW]], axis=0)
            v_cat = jnp.concatenate([v_even[kvh][r0:r0 + 2 * WINDOW], v_odd[kvh][r0:r0 + 2 * WINDOW]], axis=0)
            s = _dot_nt(q_all, k_cat)
            s_e = jnp.where(mask, s[:, :2 * WINDOW], -jnp.inf)
            s_o = jnp.where(mask, s[:, 2 * WINDOW:], -jnp.inf)
            m_e = jnp.maximum(jnp.max(s_e, axis=1, keepdims=True), sink_even)
            m_o = jnp.maximum(jnp.max(s_o, axis=1, keepdims=True), sink_odd)
            p_e = jnp.exp(s_e - m_e)
            p_o = jnp.exp(s_o - m_o)
            d_e = jnp.sum(p_e, axis=1, keepdims=True) + jnp.exp(sink_even - m_e)
            d_o = jnp.sum(p_o, axis=1, keepdims=True) + jnp.exp(sink_odd - m_o)
            p = jnp.concatenate([p_e, p_o], axis=1).astype(BF16)
            o = _dot(p, v_cat)
            o = o * jnp.where(low_o, 1.0 / d_e, 1.0 / d_o)
            o_chunks += [o[cc * WINDOW:(cc + 1) * WINDOW] for cc in range(CHUNKS_PER_KV)]
        o_blocks.append(jnp.concatenate(o_chunks, axis=1))
    o_b = jnp.concatenate(o_blocks, axis=0) if len(o_blocks) > 1 else o_blocks[0]
    out_b = _silu(z[:, OFF_AG:OFF_AG + ATTN_DIM]) * o_b

    mq = (z[:, OFF_MQ:OFF_MQ + MEM_DIM] * SCALE).astype(BF16)
    low_t = _low_lanes((tile, LANES))
    oc_chunks = []
    for c in range(M_CHUNK):
        s = _dot(mq[:, c * LANES:(c + 1) * LANES], kcat_t_ref[c])
        s0 = s[:, :n_mem]
        s1 = s[:, n_mem:]
        p0 = jnp.exp(s0 - jnp.max(s0, axis=1, keepdims=True))
        p1 = jnp.exp(s1 - jnp.max(s1, axis=1, keepdims=True))
        d0 = jnp.sum(p0, axis=1, keepdims=True)
        d1 = jnp.sum(p1, axis=1, keepdims=True)
        p = jnp.concatenate([p0, p1], axis=1).astype(BF16)
        o = _dot(p, vcat_ref[c])
        oc_chunks.append(o * jnp.where(low_t, 1.0 / d0, 1.0 / d1))
    o_c = jnp.concatenate(oc_chunks, axis=1)
    out_c = _silu(z[:, OFF_MG:OFF_MG + MEM_DIM]) * o_c

    mix = jnp.concatenate([out_a, out_b, out_c], axis=1).astype(BF16)
    y = _dot(mix, wout_ref[...])
    xo_ref[...] = x + _rmsnorm(y, gpost_ref[...])


def _prompt_layer(l, x, sinks, gpre3, gpost3, w_in_bf, conv_w, kcat_t, vcat, w_out_bf, n_mem):
    batch, seq, _ = x.shape
    tile = PROMPT_TILE
    nt = seq // tile
    return pl.pallas_call(
        functools.partial(_prompt_layer_kernel, tile=tile, n_mem=n_mem, n_tiles=nt),
        grid=(batch, nt),
        in_specs=[
            pl.BlockSpec(memory_space=pltpu.SMEM),
            pl.BlockSpec((None, tile, D_MODEL), lambda b, i: (b, i, 0)),
            pl.BlockSpec((None, 1, D_MODEL), lambda b, i: (l, 0, 0)),
            pl.BlockSpec((None, 1, D_MODEL), lambda b, i: (l, 0, 0)),
            pl.BlockSpec((None, D_MODEL, IN_DIM), lambda b, i: (l, 0, 0)),
            pl.BlockSpec((None, CONV_W, CONV_DIM), lambda b, i: (l, 0, 0)),
            pl.BlockSpec((None, None, M_CHUNK, LANES, 2 * n_mem), lambda b, i: (l, b, 0, 0, 0)),
            pl.BlockSpec((None, None, M_CHUNK, 2 * n_mem, LANES), lambda b, i: (l, b, 0, 0, 0)),
            pl.BlockSpec((None, MIX_DIM, D_MODEL), lambda b, i: (l, 0, 0)),
        ],
        out_specs=[
            pl.BlockSpec((None, tile, D_MODEL), lambda b, i: (b, i, 0)),
            pl.BlockSpec((None, KV_DIM, WINDOW), lambda b, i: (b, 0, 0)),
            pl.BlockSpec((None, KV_DIM, WINDOW), lambda b, i: (b, 0, 0)),
            pl.BlockSpec((None, SUBLANES, CONV_DIM), lambda b, i: (b, 0, 0)),
        ],
        out_shape=[
            jax.ShapeDtypeStruct((batch, seq, D_MODEL), F32),
            jax.ShapeDtypeStruct((batch, KV_DIM, WINDOW), F32),
            jax.ShapeDtypeStruct((batch, KV_DIM, WINDOW), F32),
            jax.ShapeDtypeStruct((batch, SUBLANES, CONV_DIM), F32),
        ],
        scratch_shapes=[
            pltpu.VMEM((WINDOW, KV_DIM), F32),
            pltpu.VMEM((WINDOW, KV_DIM), F32),
            pltpu.VMEM((SUBLANES + tile, CONV_DIM), F32),
        ],
        compiler_params=pltpu.CompilerParams(
            dimension_semantics=("arbitrary", "arbitrary"), vmem_limit_bytes=VMEM_LIMIT),
        name="prompt_layer",
    )(sinks[l], x, gpre3, gpost3, w_in_bf, conv_w, kcat_t, vcat, w_out_bf)


def _sample_kernel(x_ref, gpre_ref, gpost_ref, win_ref, wout_ref, convw_ref, sink_ref, convp_ref,
                   kt_ref, vt_ref, mkt_ref, mvt_ref,
                   y_ref, kto_ref, vto_ref, uo_ref,
                   xs_ref, *, group, t, n_mem):
    l = pl.program_id(0)
    g = pl.program_id(1)
    nrows = group * t
    rows = pl.ds(pl.multiple_of(g * nrows, nrows), nrows)

    @pl.when(l == 0)
    def _():
        xs_ref[rows, :] = x_ref[...]

    x = xs_ref[rows, :]
    h = _rmsnorm(x, gpre_ref[...]).astype(BF16)
    z = _dot(h, win_ref[...])

    def per_seq(a):
        return a.reshape(group, t, a.shape[-1])

    u2d = z[:, OFF_CC:OFF_CC + CONV_DIM] * z[:, OFF_CH:OFF_CH + CONV_DIM]
    uo_ref[...] = u2d
    u = per_seq(u2d)
    prev = convp_ref[...]
    tpos = lax.broadcasted_iota(jnp.int32, (group, t, CONV_DIM), 1)
    u1 = jnp.where(tpos >= 1, pltpu.roll(u, 1, 1), pltpu.roll(prev, 1, 1))
    u2 = jnp.where(tpos >= 2, pltpu.roll(u, 2, 1), pltpu.roll(prev, 2, 1))
    cw = convw_ref[...]
    conv = cw[0:1, :] * u2 + cw[1:2, :] * u1 + cw[2:3, :] * u
    out_a = per_seq(_silu(z[:, OFF_CG:OFF_CG + CONV_DIM]) * z[:, OFF_CB:OFF_CB + CONV_DIM]) * conv

    kt_old = kt_ref[...]
    vt_old = vt_ref[...]
    knt = z[:, OFF_K:OFF_K + KV_DIM].T
    vnt = z[:, OFF_V:OFF_V + KV_DIM].T
    lane3 = lax.broadcasted_iota(jnp.int32, (group, KV_DIM, WINDOW), 2)
    k_ins = jnp.stack([pltpu.roll(knt, (WINDOW - t - s * t) % WINDOW, 1) for s in range(group)])
    v_ins = jnp.stack([pltpu.roll(vnt, (WINDOW - t - s * t) % WINDOW, 1) for s in range(group)])
    kt_new = jnp.where(lane3 >= WINDOW - t, k_ins, pltpu.roll(kt_old, WINDOW - t, 2))
    vt_new = jnp.where(lane3 >= WINDOW - t, v_ins, pltpu.roll(vt_old, WINDOW - t, 2))
    kto_ref[...] = kt_new
    vto_ref[...] = vt_new
    kcat = jnp.concatenate([kt_new, kt_old], axis=2).astype(BF16)
    vcat = jnp.concatenate([vt_new, vt_old], axis=2).astype(BF16)

    q = per_seq(z[:, OFF_Q:OFF_Q + ATTN_DIM] * SCALE)
    low_q = _low_lanes((group, t, LANES))
    pieces = []
    for c in range(N_CHUNK):
        qc = q[:, :, c * LANES:(c + 1) * LANES]
        qs = _swap_halves(qc)
        if c // CHUNKS_PER_KV == 0:
            pieces += [jnp.where(low_q, qc, 0.0), jnp.where(low_q, qs, 0.0)]
        else:
            pieces += [jnp.where(low_q, 0.0, qs), jnp.where(low_q, 0.0, qc)]
    qbd = jnp.concatenate(pieces, axis=1).astype(BF16)
    nrow = ATTN_HEADS * t
    s = jnp.einsum('gqd,gdk->gqk', qbd, kcat, preferred_element_type=F32)
    tq = lax.broadcasted_iota(jnp.int32, (group, nrow, 2 * WINDOW), 1) % t
    col = lax.broadcasted_iota(jnp.int32, (group, nrow, 2 * WINDOW), 2)
    visible = (col <= tq + (WINDOW - t)) | ((col > tq + WINDOW) & (col < WINDOW + t))
    s = jnp.where(visible, s, -jnp.inf)
    sink = sink_ref[...]
    m = jnp.maximum(jnp.max(s, axis=2, keepdims=True), sink)
    p = jnp.exp(s - m)
    d = jnp.sum(p, axis=2, keepdims=True) + jnp.exp(sink - m)
    o = jnp.einsum('gqk,gdk->gqd', p.astype(BF16), vcat, preferred_element_type=F32)
    o = o * (1.0 / d)
    ob_chunks = []
    for c in range(N_CHUNK):
        o_even = o[:, 2 * c * t:(2 * c + 1) * t]
        o_odd = o[:, (2 * c + 1) * t:(2 * c + 2) * t]
        if c // CHUNKS_PER_KV == 0:
            ob_chunks.append(jnp.where(low_q, o_even, _swap_halves(o_odd)))
        else:
            ob_chunks.append(jnp.where(low_q, _swap_halves(o_even), o_odd))
    o_b = jnp.concatenate(ob_chunks, axis=2)
    out_b = per_seq(_silu(z[:, OFF_AG:OFF_AG + ATTN_DIM])) * o_b

    mq = per_seq(z[:, OFF_MQ:OFF_MQ + MEM_DIM] * SCALE)
    head_of_lane = lax.broadcasted_iota(jnp.int32, (group, t, MEM_DIM), 2) // HEAD_DIM
    qm = jnp.concatenate([jnp.where(head_of_lane == hh, mq, 0.0) for hh in range(MEM_HEADS)],
                         axis=1).astype(BF16)
    s = jnp.einsum('gqd,gdk->gqk', qm, mkt_ref[...].astype(BF16), preferred_element_type=F32)
    p = jnp.exp(s - jnp.max(s, axis=2, keepdims=True))
    d = jnp.sum(p, axis=2, keepdims=True)
    o = jnp.einsum('gqk,gdk->gqd', p.astype(BF16), mvt_ref[...].astype(BF16), preferred_element_type=F32)
    o = o * (1.0 / d)
    o_c = jnp.zeros((group, t, MEM_DIM), F32)
    for hh in range(MEM_HEADS):
        o_c = jnp.where(head_of_lane == hh, o[:, hh * t:(hh + 1) * t], o_c)
    out_c = per_seq(_silu(z[:, OFF_MG:OFF_MG + MEM_DIM])) * o_c

    mix = jnp.concatenate([out_a, out_b, out_c], axis=2).reshape(nrows, MIX_DIM).astype(BF16)
    y = _dot(mix, wout_ref[...])
    x_new = x + _rmsnorm(y, gpost_ref[...])
    xs_ref[rows, :] = x_new

    @pl.when(l == DEPTH - 1)
    def _():
        y_ref[...] = x_new


def _sample_stream(x2d, gpre3, gpost3, w_in_bf, w_out_bf, conv_w, sink_rows, convp, cache_kt, cache_vt,
                   cache_mkt, cache_mvt, n_seq, t, n_mem):
    group = SAMPLE_GROUP
    ng = n_seq // group
    nrows = group * t
    nrow_attn = ATTN_HEADS * t
    lg = lambda l, g: (l, g, 0, 0)
    return pl.pallas_call(
        functools.partial(_sample_kernel, group=group, t=t, n_mem=n_mem),
        grid=(DEPTH, ng),
        in_specs=[
            pl.BlockSpec((nrows, D_MODEL), lambda l, g: (jnp.where(l == 0, g, ng - 1), 0)),
            pl.BlockSpec((None, 1, D_MODEL), lambda l, g: (l, 0, 0)),
            pl.BlockSpec((None, 1, D_MODEL), lambda l, g: (l, 0, 0)),
            pl.BlockSpec((None, D_MODEL, IN_DIM), lambda l, g: (l, 0, 0)),
            pl.BlockSpec((None, MIX_DIM, D_MODEL), lambda l, g: (l, 0, 0)),
            pl.BlockSpec((None, CONV_W, CONV_DIM), lambda l, g: (l, 0, 0)),
            pl.BlockSpec((None, nrow_attn, 1), lambda l, g: (l, 0, 0)),
            pl.BlockSpec((None, group, t, CONV_DIM), lg),
            pl.BlockSpec((None, group, KV_DIM, WINDOW), lg),
            pl.BlockSpec((None, group, KV_DIM, WINDOW), lg),
            pl.BlockSpec((None, group, MEM_DIM, n_mem), lg),
            pl.BlockSpec((None, group, MEM_DIM, n_mem), lg),
        ],
        out_specs=[
            pl.BlockSpec((nrows, D_MODEL), lambda l, g: (jnp.where(l == DEPTH - 1, g, 0), 0)),
            pl.BlockSpec((None, group, KV_DIM, WINDOW), lg),
            pl.BlockSpec((None, group, KV_DIM, WINDOW), lg),
            pl.BlockSpec((None, nrows, CONV_DIM), lambda l, g: (l, g, 0)),
        ],
        out_shape=[
            jax.ShapeDtypeStruct((n_seq * t, D_MODEL), F32),
            jax.ShapeDtypeStruct((DEPTH, n_seq, KV_DIM, WINDOW), F32),
            jax.ShapeDtypeStruct((DEPTH, n_seq, KV_DIM, WINDOW), F32),
            jax.ShapeDtypeStruct((DEPTH, n_seq * t, CONV_DIM), F32),
        ],
        scratch_shapes=[pltpu.VMEM((n_seq * t, D_MODEL), F32)],
        compiler_params=pltpu.CompilerParams(
            dimension_semantics=("arbitrary", "arbitrary"), vmem_limit_bytes=VMEM_LIMIT),
        name="sample_stream",
    )(x2d, gpre3, gpost3, w_in_bf, w_out_bf, conv_w, sink_rows, convp, cache_kt, cache_vt, cache_mkt, cache_mvt)


def _keys_minor(a):
    lead = a.shape[:-3]
    n, heads, hd = a.shape[-3:]
    nd = a.ndim
    perm = tuple(range(nd - 3)) + (nd - 2, nd - 1, nd - 3)
    return jnp.transpose(a, perm).reshape(*lead, heads * hd, n)


def _keys_major(a, heads):
    lead = a.shape[:-2]
    n = a.shape[-1]
    a = a.reshape(*lead, heads, HEAD_DIM, n)
    nd = a.ndim
    perm = tuple(range(nd - 3)) + (nd - 1, nd - 3, nd - 2)
    return jnp.transpose(a, perm)


def kernel(x_prompt, x_sample, mem_prompt, cache_win_k, cache_win_v, state_conv, cache_mem_k, cache_mem_v,
           norm_pre, norm_post, norm_mem, w_in, conv_w, attn_sinks, w_mem_kv, w_out):
    batch, seq, _ = x_prompt.shape
    n_seq, t, _ = x_sample.shape
    n_mem = mem_prompt.shape[1]
    assert seq % PROMPT_TILE == 0 and PROMPT_TILE % WINDOW == 0
    assert n_seq % SAMPLE_GROUP == 0 and t == SUBLANES and SAMPLE_GROUP * t == WINDOW

    w_in_bf = w_in.astype(BF16)
    w_out_bf = w_out.astype(BF16)
    w_mem_bf = w_mem_kv.astype(BF16)
    sink_rows = jnp.repeat(attn_sinks, t, axis=1)[:, :, None]
    gpre3 = norm_pre[:, None, :]
    gpost3 = norm_post[:, None, :]
    gmem3 = norm_mem[:, None, :]

    mkt, mvt, kcat_t, vcat = _memkv(mem_prompt.reshape(batch * n_mem, D_MODEL), gmem3, w_mem_bf, batch, n_mem)

    convp = jnp.pad(state_conv, ((0, 0), (0, 0), (t - CONV_BUF, 0), (0, 0)))
    ys, kts, vts, us = _sample_stream(
        x_sample.reshape(n_seq * t, D_MODEL), gpre3, gpost3, w_in_bf, w_out_bf, conv_w, sink_rows, convp,
        _keys_minor(cache_win_k), _keys_minor(cache_win_v), _keys_minor(cache_mem_k), _keys_minor(cache_mem_v),
        n_seq, t, n_mem)

    xp = x_prompt
    ktp, vtp, cvp = [], [], []
    for l in range(DEPTH):
        xp, k_p, v_p, c_p = _prompt_layer(l, xp, attn_sinks, gpre3, gpost3, w_in_bf, conv_w, kcat_t, vcat,
                                          w_out_bf, n_mem)
        ktp.append(k_p)
        vtp.append(v_p)
        cvp.append(c_p[:, SUBLANES - CONV_BUF:, :])

    return (xp,
            ys.reshape(n_seq, t, D_MODEL),
            _keys_major(jnp.stack(ktp), KV_HEADS),
            _keys_major(jnp.stack(vtp), KV_HEADS),
            jnp.stack(cvp),
            _keys_major(mkt, MEM_HEADS),
            _keys_major(mvt, MEM_HEADS),
            _keys_major(kts, KV_HEADS),
            _keys_major(vts, KV_HEADS),
            us.reshape(DEPTH, n_seq, t, CONV_DIM)[:, :, t - CONV_BUF:, :])
```

```python
import functools

import jax
import jax.numpy as jnp
from jax import lax
from jax.experimental import pallas as pl
from jax.experimental.pallas import tpu as pltpu

D_MODEL = 1024
DEPTH = 4
HEAD_DIM = 64
ATTN_HEADS = 8
KV_HEADS = 2
ATTN_DIM = ATTN_HEADS * HEAD_DIM
KV_DIM = KV_HEADS * HEAD_DIM
WINDOW = 128
MEM_HEADS = 4
MEM_DIM = MEM_HEADS * HEAD_DIM
CONV_DIM = 256
CONV_W = 3
CONV_BUF = CONV_W - 1
MIX_DIM = CONV_DIM + ATTN_DIM + MEM_DIM
IN_DIM = 4 * CONV_DIM + 2 * ATTN_DIM + 2 * KV_DIM + 2 * MEM_DIM
RMS_EPS = 1e-6
SCALE = HEAD_DIM ** -0.5
LOG2E = 1.4426950408889634

LANES = 128
SUBLANES = 8
VMEM_LIMIT = 56 * 1024 * 1024

OFF_CB, OFF_CC, OFF_CH, OFF_CG = 0, 256, 512, 768
OFF_Q, OFF_K, OFF_V, OFF_AG = 1024, 1536, 1664, 1792
OFF_MQ, OFF_MG = 2304, 2560

N_CHUNK = ATTN_DIM // LANES
M_CHUNK = MEM_DIM // LANES
CHUNKS_PER_KV = N_CHUNK // KV_HEADS

PROMPT_TILE = 512
SAMPLE_GROUP = 16

F32 = jnp.float32
BF16 = jnp.bfloat16


def _rmsnorm(x, g):
    r = lax.rsqrt(jnp.mean(x * x, axis=-1, keepdims=True) + RMS_EPS)
    return (x * r) * g


def _silu(x):
    return x * jax.nn.sigmoid(x)


def _dot(a, b):
    return jnp.dot(a, b, preferred_element_type=F32)


def _dot_nt(a, b):
    return lax.dot_general(a, b, (((1,), (1,)), ((), ())), preferred_element_type=F32)


def _low_lanes(shape):
    return lax.broadcasted_iota(jnp.int32, shape, len(shape) - 1) < HEAD_DIM


def _swap_halves(a):
    return pltpu.roll(a, HEAD_DIM, a.ndim - 1)


def _memkv_kernel(mem_ref, g_ref, w_ref, mkt_ref, mvt_ref, kcat_t_ref, vcat_ref, *, batch, n_mem):
    h = _rmsnorm(mem_ref[...], g_ref[...]).astype(BF16)
    kv = _dot(h, w_ref[...])
    mk = kv[:, :MEM_DIM]
    mv = kv[:, MEM_DIM:]
    low = _low_lanes((n_mem, LANES))
    for b in range(batch):
        mkt_ref[b] = mk[b * n_mem:(b + 1) * n_mem, :].T
        mvt_ref[b] = mv[b * n_mem:(b + 1) * n_mem, :].T
        for c in range(M_CHUNK):
            kc = mk[b * n_mem:(b + 1) * n_mem, c * LANES:(c + 1) * LANES]
            vc = mv[b * n_mem:(b + 1) * n_mem, c * LANES:(c + 1) * LANES]
            kcat = jnp.concatenate([jnp.where(low, kc, 0.0), jnp.where(low, 0.0, kc)], axis=0)
            vcat = jnp.concatenate([jnp.where(low, vc, 0.0), jnp.where(low, 0.0, vc)], axis=0)
            kcat_t_ref[b, c] = kcat.T.astype(BF16)
            vcat_ref[b, c] = vcat.astype(BF16)


def _memkv(mem2d, norm_mem3, w_mem_bf, batch, n_mem):
    rows = batch * n_mem
    return pl.pallas_call(
        functools.partial(_memkv_kernel, batch=batch, n_mem=n_mem),
        grid=(DEPTH,),
        in_specs=[
            pl.BlockSpec((rows, D_MODEL), lambda l: (0, 0)),
            pl.BlockSpec((None, 1, D_MODEL), lambda l: (l, 0, 0)),
            pl.BlockSpec((None, D_MODEL, 2 * MEM_DIM), lambda l: (l, 0, 0)),
        ],
        out_specs=[
            pl.BlockSpec((None, batch, MEM_DIM, n_mem), lambda l: (l, 0, 0, 0)),
            pl.BlockSpec((None, batch, MEM_DIM, n_mem), lambda l: (l, 0, 0, 0)),
            pl.BlockSpec((None, batch, M_CHUNK, LANES, 2 * n_mem), lambda l: (l, 0, 0, 0, 0)),
            pl.BlockSpec((None, batch, M_CHUNK, 2 * n_mem, LANES), lambda l: (l, 0, 0, 0, 0)),
        ],
        out_shape=[
            jax.ShapeDtypeStruct((DEPTH, batch, MEM_DIM, n_mem), F32),
            jax.ShapeDtypeStruct((DEPTH, batch, MEM_DIM, n_mem), F32),
            jax.ShapeDtypeStruct((DEPTH, batch, M_CHUNK, LANES, 2 * n_mem), BF16),
            jax.ShapeDtypeStruct((DEPTH, batch, M_CHUNK, 2 * n_mem, LANES), BF16),
        ],
        compiler_params=pltpu.CompilerParams(dimension_semantics=("arbitrary",)),
        name="memkv",
    )(mem2d, norm_mem3, w_mem_bf)


def _prompt_layer_kernel(sink_ref, x_ref, gpre_ref, gpost_ref, win_ref, convw_ref, kcat_t_ref,
                         vcat_ref, wout_ref,
                         xo_ref, klast_ref, vlast_ref, convlast_ref,
                         kprev_ref, vprev_ref, ubuf_ref, *, tile, n_mem, n_tiles):
    i = pl.program_id(1)

    @pl.when(i == 0)
    def _():
        kprev_ref[...] = jnp.zeros_like(kprev_ref)
        vprev_ref[...] = jnp.zeros_like(vprev_ref)
        ubuf_ref[0:SUBLANES, :] = jnp.zeros((SUBLANES, CONV_DIM), F32)

    x = x_ref[...]
    h = _rmsnorm(x, gpre_ref[...]).astype(BF16)

    def proj(off, width):
        return _dot(h, win_ref[:, off:off + width])

    q_raw = proj(OFF_Q, ATTN_DIM)
    kv = proj(OFF_K, 2 * KV_DIM)
    mq_raw = proj(OFF_MQ, MEM_DIM)
    k = kv[:, :KV_DIM]
    v = kv[:, KV_DIM:]

    kfull = jnp.concatenate([kprev_ref[...], k], axis=0)
    vfull = jnp.concatenate([vprev_ref[...], v], axis=0)
    k_tail = k[tile - WINDOW:tile, :]
    v_tail = v[tile - WINDOW:tile, :]
    kprev_ref[...] = k_tail
    vprev_ref[...] = v_tail

    low = _low_lanes((WINDOW + tile, KV_DIM))
    klo = jnp.where(low, kfull, 0.0).astype(BF16)
    khi = jnp.where(low, 0.0, kfull).astype(BF16)
    vlo = jnp.where(low, vfull, 0.0).astype(BF16)
    vhi = jnp.where(low, 0.0, vfull).astype(BF16)

    q = q_raw * (SCALE * LOG2E)
    low_t = _low_lanes((tile, LANES))
    nat = [q[:, c * LANES:(c + 1) * LANES] for c in range(N_CHUNK)]
    qp = []
    for p in range(N_CHUNK):
        a, b = nat[p // 2], nat[CHUNKS_PER_KV + p // 2]
        pair = jnp.where(low_t, a, _swap_halves(b)) if p % 2 == 0 else jnp.where(low_t, _swap_halves(a), b)
        qp.append(pair.astype(BF16))

    rows = N_CHUNK * WINDOW
    qpos = lax.broadcasted_iota(jnp.int32, (rows, 2 * WINDOW), 0) % WINDOW
    kpos = lax.broadcasted_iota(jnp.int32, (rows, 2 * WINDOW), 1)
    band = (kpos > qpos) & (kpos <= qpos + WINDOW)
    band_first = band & ((kpos >= WINDOW) | (i > 0))
    chunk_of_row = lax.broadcasted_iota(jnp.int32, (rows, 1), 0) // WINDOW
    sink_lo = jnp.zeros((rows, 1), F32)
    sink_hi = jnp.zeros((rows, 1), F32)
    for p in range(N_CHUNK):
        sink_lo = jnp.where(chunk_of_row == p, sink_ref[p] * LOG2E, sink_lo)
        sink_hi = jnp.where(chunk_of_row == p, sink_ref[N_CHUNK + p] * LOG2E, sink_hi)
    low_o = _low_lanes((rows, LANES))
    low_w = _low_lanes((WINDOW, LANES))
    o_blocks = []
    for j in range(tile // WINDOW):
        r0 = j * WINDOW
        q_all = jnp.concatenate([qp[p][r0:r0 + WINDOW] for p in range(N_CHUNK)], axis=0)
        k_cat = jnp.concatenate([klo[r0:r0 + 2 * WINDOW], khi[r0:r0 + 2 * WINDOW]], axis=0)
        v_cat = jnp.concatenate([vlo[r0:r0 + 2 * WINDOW], vhi[r0:r0 + 2 * WINDOW]], axis=0)
        s = _dot_nt(q_all, k_cat)
        mask = band_first if j == 0 else band
        s_lo = jnp.where(mask, s[:, :2 * WINDOW], -jnp.inf)
        s_hi = jnp.where(mask, s[:, 2 * WINDOW:], -jnp.inf)
        m_lo = jnp.maximum(jnp.max(s_lo, axis=1, keepdims=True), sink_lo)
        m_hi = jnp.maximum(jnp.max(s_hi, axis=1, keepdims=True), sink_hi)
        p_lo = jnp.exp2(s_lo - m_lo)
        p_hi = jnp.exp2(s_hi - m_hi)
        d_lo = jnp.sum(p_lo, axis=1, keepdims=True) + jnp.exp2(sink_lo - m_lo)
        d_hi = jnp.sum(p_hi, axis=1, keepdims=True) + jnp.exp2(sink_hi - m_hi)
        pr = jnp.concatenate([p_lo, p_hi], axis=1).astype(BF16)
        o = _dot(pr, v_cat)
        o = o * jnp.where(low_o, 1.0 / d_lo, 1.0 / d_hi)
        op = [o[p * WINDOW:(p + 1) * WINDOW] for p in range(N_CHUNK)]
        o_blocks.append(jnp.concatenate(
            [jnp.where(low_w, op[0], _swap_halves(op[1])), jnp.where(low_w, op[2], _swap_halves(op[3])),
             jnp.where(low_w, _swap_halves(op[0]), op[1]), jnp.where(low_w, _swap_halves(op[2]), op[3])],
            axis=1))
    o_b = jnp.concatenate(o_blocks, axis=0) if len(o_blocks) > 1 else o_blocks[0]
    out_b = _silu(proj(OFF_AG, ATTN_DIM)) * o_b

    mq = (mq_raw * (SCALE * LOG2E)).astype(BF16)
    oc_chunks = []
    for c in range(M_CHUNK):
        s = _dot(mq[:, c * LANES:(c + 1) * LANES], kcat_t_ref[c])
        s0 = s[:, :n_mem]
        s1 = s[:, n_mem:]
        p0 = jnp.exp2(s0 - jnp.max(s0, axis=1, keepdims=True))
        p1 = jnp.exp2(s1 - jnp.max(s1, axis=1, keepdims=True))
        d0 = jnp.sum(p0, axis=1, keepdims=True)
        d1 = jnp.sum(p1, axis=1, keepdims=True)
        p = jnp.concatenate([p0, p1], axis=1).astype(BF16)
        o = _dot(p, vcat_ref[c])
        oc_chunks.append(o * jnp.where(low_t, 1.0 / d0, 1.0 / d1))
    o_c = jnp.concatenate(oc_chunks, axis=1)
    out_c = _silu(proj(OFF_MG, MEM_DIM)) * o_c

    cch = proj(OFF_CC, 2 * CONV_DIM)
    u = cch[:, :CONV_DIM] * cch[:, CONV_DIM:]
    ubuf_ref[SUBLANES:SUBLANES + tile, :] = u
    u1 = ubuf_ref[SUBLANES - 1:SUBLANES - 1 + tile, :]
    u2 = ubuf_ref[SUBLANES - 2:SUBLANES - 2 + tile, :]
    cw = convw_ref[...]
    conv = cw[0:1, :] * u2 + cw[1:2, :] * u1 + cw[2:3, :] * u
    out_a = _silu(proj(OFF_CG, CONV_DIM)) * proj(OFF_CB, CONV_DIM) * conv
    u_tail = u[tile - SUBLANES:tile, :]
    ubuf_ref[0:SUBLANES, :] = u_tail
    convlast_ref[...] = u_tail

    mix = jnp.concatenate([out_a, out_b, out_c], axis=1).astype(BF16)
    y = _dot(mix, wout_ref[...])
    xo_ref[...] = x + _rmsnorm(y, gpost_ref[...])

    @pl.when(i == n_tiles - 1)
    def _():
        klast_ref[...] = kprev_ref[...].T
        vlast_ref[...] = vprev_ref[...].T


def _prompt_layer(l, x, sinks, gpre3, gpost3, w_in_bf, conv_w, kcat_t, vcat, w_out_bf, n_mem):
    batch, seq, _ = x.shape
    tile = PROMPT_TILE
    nt = seq // tile
    return pl.pallas_call(
        functools.partial(_prompt_layer_kernel, tile=tile, n_mem=n_mem, n_tiles=nt),
        grid=(batch, nt),
        in_specs=[
            pl.BlockSpec(memory_space=pltpu.SMEM),
            pl.BlockSpec((None, tile, D_MODEL), lambda b, i: (b, i, 0)),
            pl.BlockSpec((None, 1, D_MODEL), lambda b, i: (l, 0, 0)),
            pl.BlockSpec((None, 1, D_MODEL), lambda b, i: (l, 0, 0)),
            pl.BlockSpec((None, D_MODEL, IN_DIM), lambda b, i: (l, 0, 0)),
            pl.BlockSpec((None, CONV_W, CONV_DIM), lambda b, i: (l, 0, 0)),
            pl.BlockSpec((None, None, M_CHUNK, LANES, 2 * n_mem), lambda b, i: (l, b, 0, 0, 0)),
            pl.BlockSpec((None, None, M_CHUNK, 2 * n_mem, LANES), lambda b, i: (l, b, 0, 0, 0)),
            pl.BlockSpec((None, MIX_DIM, D_MODEL), lambda b, i: (l, 0, 0)),
        ],
        out_specs=[
            pl.BlockSpec((None, tile, D_MODEL), lambda b, i: (b, i, 0)),
            pl.BlockSpec((None, KV_DIM, WINDOW), lambda b, i: (b, 0, 0)),
            pl.BlockSpec((None, KV_DIM, WINDOW), lambda b, i: (b, 0, 0)),
            pl.BlockSpec((None, SUBLANES, CONV_DIM), lambda b, i: (b, 0, 0)),
        ],
        out_shape=[
            jax.ShapeDtypeStruct((batch, seq, D_MODEL), F32),
            jax.ShapeDtypeStruct((batch, KV_DIM, WINDOW), F32),
            jax.ShapeDtypeStruct((batch, KV_DIM, WINDOW), F32),
            jax.ShapeDtypeStruct((batch, SUBLANES, CONV_DIM), F32),
        ],
        scratch_shapes=[
            pltpu.VMEM((WINDOW, KV_DIM), F32),
            pltpu.VMEM((WINDOW, KV_DIM), F32),
            pltpu.VMEM((SUBLANES + tile, CONV_DIM), F32),
        ],
        compiler_params=pltpu.CompilerParams(
            dimension_semantics=("arbitrary", "arbitrary"), vmem_limit_bytes=VMEM_LIMIT),
        name="prompt_layer",
    )(sinks[l], x, gpre3, gpost3, w_in_bf, conv_w, kcat_t, vcat, w_out_bf)


def _sample_kernel(x_ref, gpre_ref, gpost_ref, win_ref, wout_ref, convw_ref, sink_ref, convp_ref,
                   kt_ref, vt_ref, mkt_ref, mvt_ref,
                   y_ref, kto_ref, vto_ref, uo_ref,
                   xs_ref, *, group, t, n_mem):
    l = pl.program_id(0)
    g = pl.program_id(1)
    nrows = group * t
    rows = pl.ds(pl.multiple_of(g * nrows, nrows), nrows)

    @pl.when(l == 0)
    def _():
        xs_ref[rows, :] = x_ref[...]

    x = xs_ref[rows, :]
    h = _rmsnorm(x, gpre_ref[...]).astype(BF16)
    z = _dot(h, win_ref[...])

    def per_seq(a):
        return a.reshape(group, t, a.shape[-1])

    u2d = z[:, OFF_CC:OFF_CC + CONV_DIM] * z[:, OFF_CH:OFF_CH + CONV_DIM]
    uo_ref[...] = u2d
    u = per_seq(u2d)
    prev = convp_ref[...]
    tpos = lax.broadcasted_iota(jnp.int32, (group, t, CONV_DIM), 1)
    u1 = jnp.where(tpos >= 1, pltpu.roll(u, 1, 1), pltpu.roll(prev, 1, 1))
    u2 = jnp.where(tpos >= 2, pltpu.roll(u, 2, 1), pltpu.roll(prev, 2, 1))
    cw = convw_ref[...]
    conv = cw[0:1, :] * u2 + cw[1:2, :] * u1 + cw[2:3, :] * u
    out_a = per_seq(_silu(z[:, OFF_CG:OFF_CG + CONV_DIM]) * z[:, OFF_CB:OFF_CB + CONV_DIM]) * conv

    kt_old = kt_ref[...]
    vt_old = vt_ref[...]
    knt = z[:, OFF_K:OFF_K + KV_DIM].T
    vnt = z[:, OFF_V:OFF_V + KV_DIM].T
    lane3 = lax.broadcasted_iota(jnp.int32, (group, KV_DIM, WINDOW), 2)
    k_ins = jnp.stack([pltpu.roll(knt, (WINDOW - t - s * t) % WINDOW, 1) for s in range(group)])
    v_ins = jnp.stack([pltpu.roll(vnt, (WINDOW - t - s * t) % WINDOW, 1) for s in range(group)])
    kt_new = jnp.where(lane3 >= WINDOW - t, k_ins, pltpu.roll(kt_old, WINDOW - t, 2))
    vt_new = jnp.where(lane3 >= WINDOW - t, v_ins, pltpu.roll(vt_old, WINDOW - t, 2))
    kto_ref[...] = kt_new
    vto_ref[...] = vt_new
    kcat = jnp.concatenate([kt_new, kt_old], axis=2).astype(BF16)
    vcat = jnp.concatenate([vt_new, vt_old], axis=2).astype(BF16)

    q = per_seq(z[:, OFF_Q:OFF_Q + ATTN_DIM] * SCALE)
    low_q = _low_lanes((group, t, LANES))
    pieces = []
    for c in range(N_CHUNK):
        qc = q[:, :, c * LANES:(c + 1) * LANES]
        qs = _swap_halves(qc)
        if c // CHUNKS_PER_KV == 0:
            pieces += [jnp.where(low_q, qc, 0.0), jnp.where(low_q, qs, 0.0)]
        else:
            pieces += [jnp.where(low_q, 0.0, qs), jnp.where(low_q, 0.0, qc)]
    qbd = jnp.concatenate(pieces, axis=1).astype(BF16)
    nrow = ATTN_HEADS * t
    s = jnp.einsum('gqd,gdk->gqk', qbd, kcat, preferred_element_type=F32)
    tq = lax.broadcasted_iota(jnp.int32, (group, nrow, 2 * WINDOW), 1) % t
    col = lax.broadcasted_iota(jnp.int32, (group, nrow, 2 * WINDOW), 2)
    visible = (col <= tq + (WINDOW - t)) | ((col > tq + WINDOW) & (col < WINDOW + t))
    s = jnp.where(visible, s, -jnp.inf)
    sink = sink_ref[...]
    m = jnp.maximum(jnp.max(s, axis=2, keepdims=True), sink)
    p = jnp.exp(s - m)
    d = jnp.sum(p, axis=2, keepdims=True) + jnp.exp(sink - m)
    o = jnp.einsum('gqk,gdk->gqd', p.astype(BF16), vcat, preferred_element_type=F32)
    o = o * (1.0 / d)
    ob_chunks = []
    for c in range(N_CHUNK):
        o_even = o[:, 2 * c * t:(2 * c + 1) * t]
        o_odd = o[:, (2 * c + 1) * t:(2 * c + 2) * t]
        if c // CHUNKS_PER_KV == 0:
            ob_chunks.append(jnp.where(low_q, o_even, _swap_halves(o_odd)))
        else:
            ob_chunks.append(jnp.where(low_q, _swap_halves(o_even), o_odd))
    o_b = jnp.concatenate(ob_chunks, axis=2)
    out_b = per_seq(_silu(z[:, OFF_AG:OFF_AG + ATTN_DIM])) * o_b

    mq = per_seq(z[:, OFF_MQ:OFF_MQ + MEM_DIM] * SCALE)
    head_of_lane = lax.broadcasted_iota(jnp.int32, (group, t, MEM_DIM), 2) // HEAD_DIM
    qm = jnp.concatenate([jnp.where(head_of_lane == hh, mq, 0.0) for hh in range(MEM_HEADS)],
                         axis=1).astype(BF16)
    s = jnp.einsum('gqd,gdk->gqk', qm, mkt_ref[...].astype(BF16), preferred_element_type=F32)
    p = jnp.exp(s - jnp.max(s, axis=2, keepdims=True))
    d = jnp.sum(p, axis=2, keepdims=True)
    o = jnp.einsum('gqk,gdk->gqd', p.astype(BF16), mvt_ref[...].astype(BF16), preferred_element_type=F32)
    o = o * (1.0 / d)
    o_c = jnp.zeros((group, t, MEM_DIM), F32)
    for hh in range(MEM_HEADS):
        o_c = jnp.where(head_of_lane == hh, o[:, hh * t:(hh + 1) * t], o_c)
    out_c = per_seq(_silu(z[:, OFF_MG:OFF_MG + MEM_DIM])) * o_c

    mix = jnp.concatenate([out_a, out_b, out_c], axis=2).reshape(nrows, MIX_DIM).astype(BF16)
    y = _dot(mix, wout_ref[...])
    x_new = x + _rmsnorm(y, gpost_ref[...])
    xs_ref[rows, :] = x_new

    @pl.when(l == DEPTH - 1)
    def _():
        y_ref[...] = x_new


def _sample_stream(x2d, gpre3, gpost3, w_in_bf, w_out_bf, conv_w, sink_rows, convp, cache_kt, cache_vt,
                   cache_mkt, cache_mvt, n_seq, t, n_mem):
    group = SAMPLE_GROUP
    ng = n_seq // group
    nrows = group * t
    nrow_attn = ATTN_HEADS * t
    lg = lambda l, g: (l, g, 0, 0)
    return pl.pallas_call(
        functools.partial(_sample_kernel, group=group, t=t, n_mem=n_mem),
        grid=(DEPTH, ng),
        in_specs=[
            pl.BlockSpec((nrows, D_MODEL), lambda l, g: (jnp.where(l == 0, g, ng - 1), 0)),
            pl.BlockSpec((None, 1, D_MODEL), lambda l, g: (l, 0, 0)),
            pl.BlockSpec((None, 1, D_MODEL), lambda l, g: (l, 0, 0)),
            pl.BlockSpec((None, D_MODEL, IN_DIM), lambda l, g: (l, 0, 0)),
            pl.BlockSpec((None, MIX_DIM, D_MODEL), lambda l, g: (l, 0, 0)),
            pl.BlockSpec((None, CONV_W, CONV_DIM), lambda l, g: (l, 0, 0)),
            pl.BlockSpec((None, nrow_attn, 1), lambda l, g: (l, 0, 0)),
            pl.BlockSpec((None, group, t, CONV_DIM), lg),
            pl.BlockSpec((None, group, KV_DIM, WINDOW), lg),
            pl.BlockSpec((None, group, KV_DIM, WINDOW), lg),
            pl.BlockSpec((None, group, MEM_DIM, n_mem), lg),
            pl.BlockSpec((None, group, MEM_DIM, n_mem), lg),
        ],
        out_specs=[
            pl.BlockSpec((nrows, D_MODEL), lambda l, g: (jnp.where(l == DEPTH - 1, g, 0), 0)),
            pl.BlockSpec((None, group, KV_DIM, WINDOW), lg),
            pl.BlockSpec((None, group, KV_DIM, WINDOW), lg),
            pl.BlockSpec((None, nrows, CONV_DIM), lambda l, g: (l, g, 0)),
        ],
        out_shape=[
            jax.ShapeDtypeStruct((n_seq * t, D_MODEL), F32),
            jax.ShapeDtypeStruct((DEPTH, n_seq, KV_DIM, WINDOW), F32),
            jax.ShapeDtypeStruct((DEPTH, n_seq, KV_DIM, WINDOW), F32),
            jax.ShapeDtypeStruct((DEPTH, n_seq * t, CONV_DIM), F32),
        ],
        scratch_shapes=[pltpu.VMEM((n_seq * t, D_MODEL), F32)],
        compiler_params=pltpu.CompilerParams(
            dimension_semantics=("arbitrary", "arbitrary"), vmem_limit_bytes=VMEM_LIMIT),
        name="sample_stream",
    )(x2d, gpre3, gpost3, w_in_bf, w_out_bf, conv_w, sink_rows, convp, cache_kt, cache_vt, cache_mkt, cache_mvt)


def _keys_minor(a):
    lead = a.shape[:-3]
    n, heads, hd = a.shape[-3:]
    nd = a.ndim
    perm = tuple(range(nd - 3)) + (nd - 2, nd - 1, nd - 3)
    return jnp.transpose(a, perm).reshape(*lead, heads * hd, n)


def _keys_major(a, heads):
    lead = a.shape[:-2]
    n = a.shape[-1]
    a = a.reshape(*lead, heads, HEAD_DIM, n)
    nd = a.ndim
    perm = tuple(range(nd - 3)) + (nd - 1, nd - 3, nd - 2)
    return jnp.transpose(a, perm)


def kernel(x_prompt, x_sample, mem_prompt, cache_win_k, cache_win_v, state_conv, cache_mem_k, cache_mem_v,
           norm_pre, norm_post, norm_mem, w_in, conv_w, attn_sinks, w_mem_kv, w_out):
    batch, seq, _ = x_prompt.shape
    n_seq, t, _ = x_sample.shape
    n_mem = mem_prompt.shape[1]
    assert seq % PROMPT_TILE == 0 and PROMPT_TILE % WINDOW == 0
    assert n_seq % SAMPLE_GROUP == 0 and t == SUBLANES and SAMPLE_GROUP * t == WINDOW

    w_in_bf = w_in.astype(BF16)
    w_out_bf = w_out.astype(BF16)
    w_mem_bf = w_mem_kv.astype(BF16)
    sink_rows = jnp.repeat(attn_sinks, t, axis=1)[:, :, None]
    gpre3 = norm_pre[:, None, :]
    gpost3 = norm_post[:, None, :]
    gmem3 = norm_mem[:, None, :]

    mkt, mvt, kcat_t, vcat = _memkv(mem_prompt.reshape(batch * n_mem, D_MODEL), gmem3, w_mem_bf, batch, n_mem)

    convp = jnp.pad(state_conv, ((0, 0), (0, 0), (t - CONV_BUF, 0), (0, 0)))
    ys, kts, vts, us = _sample_stream(
        x_sample.reshape(n_seq * t, D_MODEL), gpre3, gpost3, w_in_bf, w_out_bf, conv_w, sink_rows, convp,
        _keys_minor(cache_win_k), _keys_minor(cache_win_v), _keys_minor(cache_mem_k), _keys_minor(cache_mem_v),
        n_seq, t, n_mem)

    xp = x_prompt
    ktp, vtp, cvp = [], [], []
    for l in range(DEPTH):
        xp, k_p, v_p, c_p = _prompt_layer(l, xp, attn_sinks, gpre3, gpost3, w_in_bf, conv_w, kcat_t, vcat,
                                          w_out_bf, n_mem)
        ktp.append(k_p)
        vtp.append(v_p)
        cvp.append(c_p[:, SUBLANES - CONV_BUF:, :])

    return (xp,
            ys.reshape(n_seq, t, D_MODEL),
            _keys_major(jnp.stack(ktp), KV_HEADS),
            _keys_major(jnp.stack(vtp), KV_HEADS),
            jnp.stack(cvp),
            _keys_major(mkt, MEM_HEADS),
            _keys_major(mvt, MEM_HEADS),
            _keys_major(kts, KV_HEADS),
            _keys_major(vts, KV_HEADS),
            us.reshape(DEPTH, n_seq, t, CONV_DIM)[:, :, t - CONV_BUF:, :])
```

```python
import functools

import jax
import jax.numpy as jnp
from jax import lax
from jax.experimental import pallas as pl
from jax.experimental.pallas import tpu as pltpu

D_MODEL = 1024
DEPTH = 4
HEAD_DIM = 64
ATTN_HEADS = 8
KV_HEADS = 2
ATTN_DIM = ATTN_HEADS * HEAD_DIM
KV_DIM = KV_HEADS * HEAD_DIM
WINDOW = 128
MEM_HEADS = 4
MEM_DIM = MEM_HEADS * HEAD_DIM
CONV_DIM = 256
CONV_W = 3
CONV_BUF = CONV_W - 1
MIX_DIM = CONV_DIM + ATTN_DIM + MEM_DIM
IN_DIM = 4 * CONV_DIM + 2 * ATTN_DIM + 2 * KV_DIM + 2 * MEM_DIM
RMS_EPS = 1e-6
SCALE = HEAD_DIM ** -0.5
LOG2E = 1.4426950408889634

LANES = 128
SUBLANES = 8
VMEM_LIMIT = 56 * 1024 * 1024

OFF_CB, OFF_CC, OFF_CH, OFF_CG = 0, 256, 512, 768
OFF_Q, OFF_K, OFF_V, OFF_AG = 1024, 1536, 1664, 1792
OFF_MQ, OFF_MG = 2304, 2560

N_CHUNK = ATTN_DIM // LANES
M_CHUNK = MEM_DIM // LANES
CHUNKS_PER_KV = N_CHUNK // KV_HEADS

PROMPT_TILE = 512
SAMPLE_GROUP = 16
SAMPLE_SUBSTEPS = 2

F32 = jnp.float32
BF16 = jnp.bfloat16


def _rmsnorm(x, g):
    r = lax.rsqrt(jnp.mean(x * x, axis=-1, keepdims=True) + RMS_EPS)
    return (x * r) * g


def _silu(x):
    return x * jax.nn.sigmoid(x)


def _dot(a, b):
    return jnp.dot(a, b, preferred_element_type=F32)


def _dot_nt(a, b):
    return lax.dot_general(a, b, (((1,), (1,)), ((), ())), preferred_element_type=F32)


def _low_lanes(shape):
    return lax.broadcasted_iota(jnp.int32, shape, len(shape) - 1) < HEAD_DIM


def _swap_halves(a):
    return pltpu.roll(a, HEAD_DIM, a.ndim - 1)


def _memkv_kernel(mem_ref, g_ref, w_ref, mkt_ref, mvt_ref, kcat_t_ref, vcat_ref, *, batch, n_mem):
    h = _rmsnorm(mem_ref[...], g_ref[...]).astype(BF16)
    kv = _dot(h, w_ref[...])
    mk = kv[:, :MEM_DIM]
    mv = kv[:, MEM_DIM:]
    low = _low_lanes((n_mem, LANES))
    for b in range(batch):
        mkt_ref[b] = mk[b * n_mem:(b + 1) * n_mem, :].T
        mvt_ref[b] = mv[b * n_mem:(b + 1) * n_mem, :].T
        for c in range(M_CHUNK):
            kc = mk[b * n_mem:(b + 1) * n_mem, c * LANES:(c + 1) * LANES]
            vc = mv[b * n_mem:(b + 1) * n_mem, c * LANES:(c + 1) * LANES]
            kcat = jnp.concatenate([jnp.where(low, kc, 0.0), jnp.where(low, 0.0, kc)], axis=0)
            vcat = jnp.concatenate([jnp.where(low, vc, 0.0), jnp.where(low, 0.0, vc)], axis=0)
            kcat_t_ref[b, c] = kcat.T.astype(BF16)
            vcat_ref[b, c] = vcat.astype(BF16)


def _memkv(mem2d, norm_mem3, w_mem_bf, batch, n_mem):
    rows = batch * n_mem
    return pl.pallas_call(
        functools.partial(_memkv_kernel, batch=batch, n_mem=n_mem),
        grid=(DEPTH,),
        in_specs=[
            pl.BlockSpec((rows, D_MODEL), lambda l: (0, 0)),
            pl.BlockSpec((None, 1, D_MODEL), lambda l: (l, 0, 0)),
            pl.BlockSpec((None, D_MODEL, 2 * MEM_DIM), lambda l: (l, 0, 0)),
        ],
        out_specs=[
            pl.BlockSpec((None, batch, MEM_DIM, n_mem), lambda l: (l, 0, 0, 0)),
            pl.BlockSpec((None, batch, MEM_DIM, n_mem), lambda l: (l, 0, 0, 0)),
            pl.BlockSpec((None, batch, M_CHUNK, LANES, 2 * n_mem), lambda l: (l, 0, 0, 0, 0)),
            pl.BlockSpec((None, batch, M_CHUNK, 2 * n_mem, LANES), lambda l: (l, 0, 0, 0, 0)),
        ],
        out_shape=[
            jax.ShapeDtypeStruct((DEPTH, batch, MEM_DIM, n_mem), F32),
            jax.ShapeDtypeStruct((DEPTH, batch, MEM_DIM, n_mem), F32),
            jax.ShapeDtypeStruct((DEPTH, batch, M_CHUNK, LANES, 2 * n_mem), BF16),
            jax.ShapeDtypeStruct((DEPTH, batch, M_CHUNK, 2 * n_mem, LANES), BF16),
        ],
        compiler_params=pltpu.CompilerParams(dimension_semantics=("arbitrary",)),
        name="memkv",
    )(mem2d, norm_mem3, w_mem_bf)


def _prompt_layer_kernel(sink_ref, x_ref, gpre_ref, gpost_ref, win_ref, convw_ref, kcat_t_ref,
                         vcat_ref, wout_ref,
                         xo_ref, klast_ref, vlast_ref, convlast_ref,
                         kprev_ref, vprev_ref, ubuf_ref, *, tile, n_mem, n_tiles):
    i = pl.program_id(1)

    @pl.when(i == 0)
    def _():
        kprev_ref[...] = jnp.zeros_like(kprev_ref)
        vprev_ref[...] = jnp.zeros_like(vprev_ref)
        ubuf_ref[0:SUBLANES, :] = jnp.zeros((SUBLANES, CONV_DIM), F32)

    x = x_ref[...]
    h = _rmsnorm(x, gpre_ref[...]).astype(BF16)

    def proj(off, width):
        return _dot(h, win_ref[:, off:off + width])

    q_raw = proj(OFF_Q, ATTN_DIM)
    kv = proj(OFF_K, 2 * KV_DIM)
    mq_raw = proj(OFF_MQ, MEM_DIM)
    k = kv[:, :KV_DIM]
    v = kv[:, KV_DIM:]

    kfull = jnp.concatenate([kprev_ref[...], k], axis=0)
    vfull = jnp.concatenate([vprev_ref[...], v], axis=0)
    k_tail = k[tile - WINDOW:tile, :]
    v_tail = v[tile - WINDOW:tile, :]
    kprev_ref[...] = k_tail
    vprev_ref[...] = v_tail

    low = _low_lanes((WINDOW + tile, KV_DIM))
    klo = jnp.where(low, kfull, 0.0).astype(BF16)
    khi = jnp.where(low, 0.0, kfull).astype(BF16)
    vlo = jnp.where(low, vfull, 0.0).astype(BF16)
    vhi = jnp.where(low, 0.0, vfull).astype(BF16)

    q = q_raw * (SCALE * LOG2E)
    low_t = _low_lanes((tile, LANES))
    nat = [q[:, c * LANES:(c + 1) * LANES] for c in range(N_CHUNK)]
    qp = []
    for p in range(N_CHUNK):
        a, b = nat[p // 2], nat[CHUNKS_PER_KV + p // 2]
        pair = jnp.where(low_t, a, _swap_halves(b)) if p % 2 == 0 else jnp.where(low_t, _swap_halves(a), b)
        qp.append(pair.astype(BF16))

    rows = N_CHUNK * WINDOW
    qpos = lax.broadcasted_iota(jnp.int32, (rows, 2 * WINDOW), 0) % WINDOW
    kpos = lax.broadcasted_iota(jnp.int32, (rows, 2 * WINDOW), 1)
    band = (kpos > qpos) & (kpos <= qpos + WINDOW)
    band_first = band & ((kpos >= WINDOW) | (i > 0))
    chunk_of_row = lax.broadcasted_iota(jnp.int32, (rows, 1), 0) // WINDOW
    sink_lo = jnp.zeros((rows, 1), F32)
    sink_hi = jnp.zeros((rows, 1), F32)
    for p in range(N_CHUNK):
        sink_lo = jnp.where(chunk_of_row == p, sink_ref[p] * LOG2E, sink_lo)
        sink_hi = jnp.where(chunk_of_row == p, sink_ref[N_CHUNK + p] * LOG2E, sink_hi)
    low_o = _low_lanes((rows, LANES))
    low_w = _low_lanes((WINDOW, LANES))
    o_blocks = []
    for j in range(tile // WINDOW):
        r0 = j * WINDOW
        q_all = jnp.concatenate([qp[p][r0:r0 + WINDOW] for p in range(N_CHUNK)], axis=0)
        k_cat = jnp.concatenate([klo[r0:r0 + 2 * WINDOW], khi[r0:r0 + 2 * WINDOW]], axis=0)
        v_cat = jnp.concatenate([vlo[r0:r0 + 2 * WINDOW], vhi[r0:r0 + 2 * WINDOW]], axis=0)
        s = _dot_nt(q_all, k_cat)
        mask = band_first if j == 0 else band
        s_lo = jnp.where(mask, s[:, :2 * WINDOW], -jnp.inf)
        s_hi = jnp.where(mask, s[:, 2 * WINDOW:], -jnp.inf)
        m_lo = jnp.maximum(jnp.max(s_lo, axis=1, keepdims=True), sink_lo)
        m_hi = jnp.maximum(jnp.max(s_hi, axis=1, keepdims=True), sink_hi)
        p_lo = jnp.exp2(s_lo - m_lo)
        p_hi = jnp.exp2(s_hi - m_hi)
        d_lo = jnp.sum(p_lo, axis=1, keepdims=True) + jnp.exp2(sink_lo - m_lo)
        d_hi = jnp.sum(p_hi, axis=1, keepdims=True) + jnp.exp2(sink_hi - m_hi)
        pr = jnp.concatenate([p_lo, p_hi], axis=1).astype(BF16)
        o = _dot(pr, v_cat)
        o = o * jnp.where(low_o, 1.0 / d_lo, 1.0 / d_hi)
        op = [o[p * WINDOW:(p + 1) * WINDOW] for p in range(N_CHUNK)]
        o_blocks.append(jnp.concatenate(
            [jnp.where(low_w, op[0], _swap_halves(op[1])), jnp.where(low_w, op[2], _swap_halves(op[3])),
             jnp.where(low_w, _swap_halves(op[0]), op[1]), jnp.where(low_w, _swap_halves(op[2]), op[3])],
            axis=1))
    o_b = jnp.concatenate(o_blocks, axis=0) if len(o_blocks) > 1 else o_blocks[0]
    out_b = _silu(proj(OFF_AG, ATTN_DIM)) * o_b

    mq = (mq_raw * (SCALE * LOG2E)).astype(BF16)
    oc_chunks = []
    for c in range(M_CHUNK):
        s = _dot(mq[:, c * LANES:(c + 1) * LANES], kcat_t_ref[c])
        s0 = s[:, :n_mem]
        s1 = s[:, n_mem:]
        p0 = jnp.exp2(s0 - jnp.max(s0, axis=1, keepdims=True))
        p1 = jnp.exp2(s1 - jnp.max(s1, axis=1, keepdims=True))
        d0 = jnp.sum(p0, axis=1, keepdims=True)
        d1 = jnp.sum(p1, axis=1, keepdims=True)
        p = jnp.concatenate([p0, p1], axis=1).astype(BF16)
        o = _dot(p, vcat_ref[c])
        oc_chunks.append(o * jnp.where(low_t, 1.0 / d0, 1.0 / d1))
    o_c = jnp.concatenate(oc_chunks, axis=1)
    out_c = _silu(proj(OFF_MG, MEM_DIM)) * o_c

    cch = proj(OFF_CC, 2 * CONV_DIM)
    u = cch[:, :CONV_DIM] * cch[:, CONV_DIM:]
    ubuf_ref[SUBLANES:SUBLANES + tile, :] = u
    u1 = ubuf_ref[SUBLANES - 1:SUBLANES - 1 + tile, :]
    u2 = ubuf_ref[SUBLANES - 2:SUBLANES - 2 + tile, :]
    cw = convw_ref[...]
    conv = cw[0:1, :] * u2 + cw[1:2, :] * u1 + cw[2:3, :] * u
    out_a = _silu(proj(OFF_CG, CONV_DIM)) * proj(OFF_CB, CONV_DIM) * conv
    u_tail = u[tile - SUBLANES:tile, :]
    ubuf_ref[0:SUBLANES, :] = u_tail
    convlast_ref[...] = u_tail

    mix = jnp.concatenate([out_a, out_b, out_c], axis=1).astype(BF16)
    y = _dot(mix, wout_ref[...])
    xo_ref[...] = x + _rmsnorm(y, gpost_ref[...])

    @pl.when(i == n_tiles - 1)
    def _():
        klast_ref[...] = kprev_ref[...].T
        vlast_ref[...] = vprev_ref[...].T


def _prompt_layer(l, x, sinks, gpre3, gpost3, w_in_bf, conv_w, kcat_t, vcat, w_out_bf, n_mem):
    batch, seq, _ = x.shape
    tile = PROMPT_TILE
    nt = seq // tile
    return pl.pallas_call(
        functools.partial(_prompt_layer_kernel, tile=tile, n_mem=n_mem, n_tiles=nt),
        grid=(batch, nt),
        in_specs=[
            pl.BlockSpec(memory_space=pltpu.SMEM),
            pl.BlockSpec((None, tile, D_MODEL), lambda b, i: (b, i, 0)),
            pl.BlockSpec((None, 1, D_MODEL), lambda b, i: (l, 0, 0)),
            pl.BlockSpec((None, 1, D_MODEL), lambda b, i: (l, 0, 0)),
            pl.BlockSpec((None, D_MODEL, IN_DIM), lambda b, i: (l, 0, 0)),
            pl.BlockSpec((None, CONV_W, CONV_DIM), lambda b, i: (l, 0, 0)),
            pl.BlockSpec((None, None, M_CHUNK, LANES, 2 * n_mem), lambda b, i: (l, b, 0, 0, 0)),
            pl.BlockSpec((None, None, M_CHUNK, 2 * n_mem, LANES), lambda b, i: (l, b, 0, 0, 0)),
            pl.BlockSpec((None, MIX_DIM, D_MODEL), lambda b, i: (l, 0, 0)),
        ],
        out_specs=[
            pl.BlockSpec((None, tile, D_MODEL), lambda b, i: (b, i, 0)),
            pl.BlockSpec((None, KV_DIM, WINDOW), lambda b, i: (b, 0, 0)),
            pl.BlockSpec((None, KV_DIM, WINDOW), lambda b, i: (b, 0, 0)),
            pl.BlockSpec((None, SUBLANES, CONV_DIM), lambda b, i: (b, 0, 0)),
        ],
        out_shape=[
            jax.ShapeDtypeStruct((batch, seq, D_MODEL), F32),
            jax.ShapeDtypeStruct((batch, KV_DIM, WINDOW), F32),
            jax.ShapeDtypeStruct((batch, KV_DIM, WINDOW), F32),
            jax.ShapeDtypeStruct((batch, SUBLANES, CONV_DIM), F32),
        ],
        scratch_shapes=[
            pltpu.VMEM((WINDOW, KV_DIM), F32),
            pltpu.VMEM((WINDOW, KV_DIM), F32),
            pltpu.VMEM((SUBLANES + tile, CONV_DIM), F32),
        ],
        compiler_params=pltpu.CompilerParams(
            dimension_semantics=("arbitrary", "arbitrary"), vmem_limit_bytes=VMEM_LIMIT),
        name="prompt_layer",
    )(sinks[l], x, gpre3, gpost3, w_in_bf, conv_w, kcat_t, vcat, w_out_bf)


def _sample_kernel(x_ref, gpre_ref, gpost_ref, win_ref, wout_ref, convw_ref, sink_ref, convp_ref,
                   kt_ref, vt_ref, mkt_ref, mvt_ref,
                   y_ref, kto_ref, vto_ref, uo_ref,
                   xs_ref, z_ref, mix_ref, *, group, t, n_mem, n_sub):
    l = pl.program_id(0)
    p = pl.program_id(1)
    ss = pl.program_id(2)
    nrows = group * t
    prows = n_sub * nrows
    rows_p = pl.ds(pl.multiple_of(p * prows, prows), prows)
    rows_s = pl.ds(pl.multiple_of(ss * nrows, nrows), nrows)

    @pl.when(ss == 0)
    def _():
        @pl.when(l == 0)
        def _():
            xs_ref[rows_p, :] = x_ref[...]

        h = _rmsnorm(xs_ref[rows_p, :], gpre_ref[...]).astype(BF16)
        z_ref[...] = _dot(h, win_ref[...])

    def zc(off, width):
        return z_ref[rows_s, off:off + width]

    def per_seq(a):
        return a.reshape(group, t, a.shape[-1])

    u2d = zc(OFF_CC, CONV_DIM) * zc(OFF_CH, CONV_DIM)
    uo_ref[...] = u2d
    u = per_seq(u2d)
    prev = convp_ref[...]
    tpos = lax.broadcasted_iota(jnp.int32, (group, t, CONV_DIM), 1)
    u1 = jnp.where(tpos >= 1, pltpu.roll(u, 1, 1), pltpu.roll(prev, 1, 1))
    u2 = jnp.where(tpos >= 2, pltpu.roll(u, 2, 1), pltpu.roll(prev, 2, 1))
    cw = convw_ref[...]
    conv = cw[0:1, :] * u2 + cw[1:2, :] * u1 + cw[2:3, :] * u
    out_a = per_seq(_silu(zc(OFF_CG, CONV_DIM)) * zc(OFF_CB, CONV_DIM)) * conv

    kt_old = kt_ref[...]
    vt_old = vt_ref[...]
    k_new = zc(OFF_K, KV_DIM)
    v_new = zc(OFF_V, KV_DIM)
    k_new_t = jnp.swapaxes(per_seq(k_new), 1, 2)
    v_new_t = jnp.swapaxes(per_seq(v_new), 1, 2)
    kto_ref[...] = pltpu.roll(jnp.concatenate([k_new_t, kt_old[:, :, t:]], axis=2), WINDOW - t, 2)
    vto_ref[...] = pltpu.roll(jnp.concatenate([v_new_t, vt_old[:, :, t:]], axis=2), WINDOW - t, 2)

    q = per_seq(zc(OFF_Q, ATTN_DIM) * SCALE)
    low_q = _low_lanes((group, t, LANES))
    pieces = []
    for c in range(N_CHUNK):
        qc = q[:, :, c * LANES:(c + 1) * LANES]
        qs = _swap_halves(qc)
        if c // CHUNKS_PER_KV == 0:
            pieces += [jnp.where(low_q, qc, 0.0), jnp.where(low_q, qs, 0.0)]
        else:
            pieces += [jnp.where(low_q, 0.0, qs), jnp.where(low_q, 0.0, qc)]
    qbd = jnp.concatenate(pieces, axis=1).astype(BF16)
    nrow = ATTN_HEADS * t
    s_old = jnp.einsum('gqd,gdk->gqk', qbd, kt_old.astype(BF16), preferred_element_type=F32)
    s_new = _dot_nt(qbd.reshape(group * nrow, KV_DIM), k_new.astype(BF16)).reshape(group, nrow, group * t)
    tq = lax.broadcasted_iota(jnp.int32, (group, nrow, WINDOW), 1) % t
    col = lax.broadcasted_iota(jnp.int32, (group, nrow, WINDOW), 2)
    first_new = lax.broadcasted_iota(jnp.int32, (group, nrow, WINDOW), 0) * t
    s_old = jnp.where(col > tq, s_old, -jnp.inf)
    s_new = jnp.where((col >= first_new) & (col <= first_new + tq), s_new, -jnp.inf)
    sink = sink_ref[...]
    m = jnp.maximum(jnp.max(jnp.maximum(s_old, s_new), axis=2, keepdims=True), sink)
    p_old = jnp.exp(s_old - m)
    p_new = jnp.exp(s_new - m)
    d = jnp.sum(p_old + p_new, axis=2, keepdims=True) + jnp.exp(sink - m)
    o = jnp.einsum('gqk,gdk->gqd', p_old.astype(BF16), vt_old.astype(BF16), preferred_element_type=F32)
    o = o + _dot(p_new.astype(BF16).reshape(group * nrow, group * t), v_new.astype(BF16)).reshape(group, nrow, KV_DIM)
    o = o * (1.0 / d)
    ob_chunks = []
    for c in range(N_CHUNK):
        o_even = o[:, 2 * c * t:(2 * c + 1) * t]
        o_odd = o[:, (2 * c + 1) * t:(2 * c + 2) * t]
        if c // CHUNKS_PER_KV == 0:
            ob_chunks.append(jnp.where(low_q, o_even, _swap_halves(o_odd)))
        else:
            ob_chunks.append(jnp.where(low_q, _swap_halves(o_even), o_odd))
    o_b = jnp.concatenate(ob_chunks, axis=2)
    out_b = per_seq(_silu(zc(OFF_AG, ATTN_DIM))) * o_b

    mq = per_seq(zc(OFF_MQ, MEM_DIM) * SCALE)
    head_of_lane = lax.broadcasted_iota(jnp.int32, (group, t, MEM_DIM), 2) // HEAD_DIM
    qm = jnp.concatenate([jnp.where(head_of_lane == hh, mq, 0.0) for hh in range(MEM_HEADS)],
                         axis=1).astype(BF16)
    s = jnp.einsum('gqd,gdk->gqk', qm, mkt_ref[...].astype(BF16), preferred_element_type=F32)
    p = jnp.exp(s - jnp.max(s, axis=2, keepdims=True))
    d = jnp.sum(p, axis=2, keepdims=True)
    o = jnp.einsum('gqk,gdk->gqd', p.astype(BF16), mvt_ref[...].astype(BF16), preferred_element_type=F32)
    o = o * (1.0 / d)
    o_c = jnp.zeros((group, t, MEM_DIM), F32)
    for hh in range(MEM_HEADS):
        o_c = jnp.where(head_of_lane == hh, o[:, hh * t:(hh + 1) * t], o_c)
    out_c = per_seq(_silu(zc(OFF_MG, MEM_DIM))) * o_c

    mix_ref[rows_s, :] = jnp.concatenate([out_a, out_b, out_c], axis=2).reshape(nrows, MIX_DIM).astype(BF16)

    @pl.when(ss == n_sub - 1)
    def _():
        y = _dot(mix_ref[...], wout_ref[...])
        x_new = xs_ref[rows_p, :] + _rmsnorm(y, gpost_ref[...])
        xs_ref[rows_p, :] = x_new

        @pl.when(l == DEPTH - 1)
        def _():
            y_ref[...] = x_new


def _sample_stream(x2d, gpre3, gpost3, w_in_bf, w_out_bf, conv_w, sink_rows, convp, cache_kt, cache_vt,
                   cache_mkt, cache_mvt, n_seq, t, n_mem):
    group = SAMPLE_GROUP
    n_sub = SAMPLE_SUBSTEPS
    n_proj = n_seq // (group * n_sub)
    nrows = group * t
    prows = n_sub * nrows
    nrow_attn = ATTN_HEADS * t
    per_layer = lambda l, p, ss: (l, 0, 0)
    per_group = lambda l, p, ss: (l, p * n_sub + ss, 0, 0)
    return pl.pallas_call(
        functools.partial(_sample_kernel, group=group, t=t, n_mem=n_mem, n_sub=n_sub),
        grid=(DEPTH, n_proj, n_sub),
        in_specs=[
            pl.BlockSpec((prows, D_MODEL), lambda l, p, ss: (jnp.where(l == 0, p, n_proj - 1), 0)),
            pl.BlockSpec((None, 1, D_MODEL), per_layer),
            pl.BlockSpec((None, 1, D_MODEL), per_layer),
            pl.BlockSpec((None, D_MODEL, IN_DIM), per_layer),
            pl.BlockSpec((None, MIX_DIM, D_MODEL), per_layer),
            pl.BlockSpec((None, CONV_W, CONV_DIM), per_layer),
            pl.BlockSpec((None, nrow_attn, 1), per_layer),
            pl.BlockSpec((None, group, t, CONV_DIM), per_group),
            pl.BlockSpec((None, group, KV_DIM, WINDOW), per_group),
            pl.BlockSpec((None, group, KV_DIM, WINDOW), per_group),
            pl.BlockSpec((None, group, MEM_DIM, n_mem), per_group),
            pl.BlockSpec((None, group, MEM_DIM, n_mem), per_group),
        ],
        out_specs=[
            pl.BlockSpec((prows, D_MODEL), lambda l, p, ss: (jnp.where(l == DEPTH - 1, p, 0), 0)),
            pl.BlockSpec((None, group, KV_DIM, WINDOW), per_group),
            pl.BlockSpec((None, group, KV_DIM, WINDOW), per_group),
            pl.BlockSpec((None, nrows, CONV_DIM), lambda l, p, ss: (l, p * n_sub + ss, 0)),
        ],
        out_shape=[
            jax.ShapeDtypeStruct((n_seq * t, D_MODEL), F32),
            jax.ShapeDtypeStruct((DEPTH, n_seq, KV_DIM, WINDOW), F32),
            jax.ShapeDtypeStruct((DEPTH, n_seq, KV_DIM, WINDOW), F32),
            jax.ShapeDtypeStruct((DEPTH, n_seq * t, CONV_DIM), F32),
        ],
        scratch_shapes=[
            pltpu.VMEM((n_seq * t, D_MODEL), F32),
            pltpu.VMEM((prows, IN_DIM), F32),
            pltpu.VMEM((prows, MIX_DIM), BF16),
        ],
        compiler_params=pltpu.CompilerParams(
            dimension_semantics=("arbitrary", "arbitrary", "arbitrary"), vmem_limit_bytes=VMEM_LIMIT),
        name="sample_stream",
    )(x2d, gpre3, gpost3, w_in_bf, w_out_bf, conv_w, sink_rows, convp, cache_kt, cache_vt, cache_mkt, cache_mvt)


def _keys_minor(a):
    lead = a.shape[:-3]
    n, heads, hd = a.shape[-3:]
    nd = a.ndim
    perm = tuple(range(nd - 3)) + (nd - 2, nd - 1, nd - 3)
    return jnp.transpose(a, perm).reshape(*lead, heads * hd, n)


def _keys_major(a, heads):
    lead = a.shape[:-2]
    n = a.shape[-1]
    a = a.reshape(*lead, heads, HEAD_DIM, n)
    nd = a.ndim
    perm = tuple(range(nd - 3)) + (nd - 1, nd - 3, nd - 2)
    return jnp.transpose(a, perm)


def kernel(x_prompt, x_sample, mem_prompt, cache_win_k, cache_win_v, state_conv, cache_mem_k, cache_mem_v,
           norm_pre, norm_post, norm_mem, w_in, conv_w, attn_sinks, w_mem_kv, w_out):
    batch, seq, _ = x_prompt.shape
    n_seq, t, _ = x_sample.shape
    n_mem = mem_prompt.shape[1]
    assert seq % PROMPT_TILE == 0 and PROMPT_TILE % WINDOW == 0
    assert n_seq % (SAMPLE_GROUP * SAMPLE_SUBSTEPS) == 0 and t == SUBLANES

    w_in_bf = w_in.astype(BF16)
    w_out_bf = w_out.astype(BF16)
    w_mem_bf = w_mem_kv.astype(BF16)
    sink_rows = jnp.repeat(attn_sinks, t, axis=1)[:, :, None]
    gpre3 = norm_pre[:, None, :]
    gpost3 = norm_post[:, None, :]
    gmem3 = norm_mem[:, None, :]

    mkt, mvt, kcat_t, vcat = _memkv(mem_prompt.reshape(batch * n_mem, D_MODEL), gmem3, w_mem_bf, batch, n_mem)

    convp = jnp.pad(state_conv, ((0, 0), (0, 0), (t - CONV_BUF, 0), (0, 0)))
    ys, kts, vts, us = _sample_stream(
        x_sample.reshape(n_seq * t, D_MODEL), gpre3, gpost3, w_in_bf, w_out_bf, conv_w, sink_rows, convp,
        _keys_minor(cache_win_k), _keys_minor(cache_win_v), _keys_minor(cache_mem_k), _keys_minor(cache_mem_v),
        n_seq, t, n_mem)

    xp = x_prompt
    ktp, vtp, cvp = [], [], []
    for l in range(DEPTH):
        xp, k_p, v_p, c_p = _prompt_layer(l, xp, attn_sinks, gpre3, gpost3, w_in_bf, conv_w, kcat_t, vcat,
                                          w_out_bf, n_mem)
        ktp.append(k_p)
        vtp.append(v_p)
        cvp.append(c_p[:, SUBLANES - CONV_BUF:, :])

    return (xp,
            ys.reshape(n_seq, t, D_MODEL),
            _keys_major(jnp.stack(ktp), KV_HEADS),
            _keys_major(jnp.stack(vtp), KV_HEADS),
            jnp.stack(cvp),
            _keys_major(mkt, MEM_HEADS),
            _keys_major(mvt, MEM_HEADS),
            _keys_major(kts, KV_HEADS),
            _keys_major(vts, KV_HEADS),
            us.reshape(DEPTH, n_seq, t, CONV_DIM)[:, :, t - CONV_BUF:, :])
```

```python
import functools

import jax
import jax.numpy as jnp
from jax import lax
from jax.experimental import pallas as pl
from jax.experimental.pallas import tpu as pltpu

D_MODEL = 1024
DEPTH = 4
HEAD_DIM = 64
ATTN_HEADS = 8
KV_HEADS = 2
ATTN_DIM = ATTN_HEADS * HEAD_DIM
KV_DIM = KV_HEADS * HEAD_DIM
WINDOW = 128
MEM_HEADS = 4
MEM_DIM = MEM_HEADS * HEAD_DIM
CONV_DIM = 256
CONV_W = 3
CONV_BUF = CONV_W - 1
MIX_DIM = CONV_DIM + ATTN_DIM + MEM_DIM
IN_DIM = 4 * CONV_DIM + 2 * ATTN_DIM + 2 * KV_DIM + 2 * MEM_DIM
RMS_EPS = 1e-6
SCALE = HEAD_DIM ** -0.5
LOG2E = 1.4426950408889634

LANES = 128
SUBLANES = 8
VMEM_LIMIT = 56 * 1024 * 1024

OFF_CB, OFF_CC, OFF_CH, OFF_CG = 0, 256, 512, 768
OFF_Q, OFF_K, OFF_V, OFF_AG = 1024, 1536, 1664, 1792
OFF_MQ, OFF_MG = 2304, 2560

N_CHUNK = ATTN_DIM // LANES
M_CHUNK = MEM_DIM // LANES
CHUNKS_PER_KV = N_CHUNK // KV_HEADS

PROMPT_TILE = 512
SAMPLE_GROUP = 16
SAMPLE_SUBSTEPS = 2

F32 = jnp.float32
BF16 = jnp.bfloat16


def _rmsnorm(x, g):
    r = lax.rsqrt(jnp.mean(x * x, axis=-1, keepdims=True) + RMS_EPS)
    return (x * r) * g


def _silu(x):
    return x * jax.nn.sigmoid(x)


def _dot(a, b):
    return jnp.dot(a, b, preferred_element_type=F32)


def _dot_nt(a, b):
    return lax.dot_general(a, b, (((1,), (1,)), ((), ())), preferred_element_type=F32)


def _low_lanes(shape):
    return lax.broadcasted_iota(jnp.int32, shape, len(shape) - 1) < HEAD_DIM


def _swap_halves(a):
    return pltpu.roll(a, HEAD_DIM, a.ndim - 1)


def _memkv_kernel(mem_ref, g_ref, w_ref, mkt_ref, mvt_ref, kcat_t_ref, vcat_ref, *, batch, n_mem):
    h = _rmsnorm(mem_ref[...], g_ref[...]).astype(BF16)
    kv = _dot(h, w_ref[...])
    mk = kv[:, :MEM_DIM]
    mv = kv[:, MEM_DIM:]
    low = _low_lanes((n_mem, LANES))
    for b in range(batch):
        mkt_ref[b] = mk[b * n_mem:(b + 1) * n_mem, :].T
        mvt_ref[b] = mv[b * n_mem:(b + 1) * n_mem, :].T
        for c in range(M_CHUNK):
            kc = mk[b * n_mem:(b + 1) * n_mem, c * LANES:(c + 1) * LANES]
            vc = mv[b * n_mem:(b + 1) * n_mem, c * LANES:(c + 1) * LANES]
            kcat = jnp.concatenate([jnp.where(low, kc, 0.0), jnp.where(low, 0.0, kc)], axis=0)
            vcat = jnp.concatenate([jnp.where(low, vc, 0.0), jnp.where(low, 0.0, vc)], axis=0)
            kcat_t_ref[b, c] = kcat.T.astype(BF16)
            vcat_ref[b, c] = vcat.astype(BF16)


def _memkv(mem2d, norm_mem3, w_mem_bf, batch, n_mem):
    rows = batch * n_mem
    return pl.pallas_call(
        functools.partial(_memkv_kernel, batch=batch, n_mem=n_mem),
        grid=(DEPTH,),
        in_specs=[
            pl.BlockSpec((rows, D_MODEL), lambda l: (0, 0)),
            pl.BlockSpec((None, 1, D_MODEL), lambda l: (l, 0, 0)),
            pl.BlockSpec((None, D_MODEL, 2 * MEM_DIM), lambda l: (l, 0, 0)),
        ],
        out_specs=[
            pl.BlockSpec((None, batch, MEM_DIM, n_mem), lambda l: (l, 0, 0, 0)),
            pl.BlockSpec((None, batch, MEM_DIM, n_mem), lambda l: (l, 0, 0, 0)),
            pl.BlockSpec((None, batch, M_CHUNK, LANES, 2 * n_mem), lambda l: (l, 0, 0, 0, 0)),
            pl.BlockSpec((None, batch, M_CHUNK, 2 * n_mem, LANES), lambda l: (l, 0, 0, 0, 0)),
        ],
        out_shape=[
            jax.ShapeDtypeStruct((DEPTH, batch, MEM_DIM, n_mem), F32),
            jax.ShapeDtypeStruct((DEPTH, batch, MEM_DIM, n_mem), F32),
            jax.ShapeDtypeStruct((DEPTH, batch, M_CHUNK, LANES, 2 * n_mem), BF16),
            jax.ShapeDtypeStruct((DEPTH, batch, M_CHUNK, 2 * n_mem, LANES), BF16),
        ],
        compiler_params=pltpu.CompilerParams(dimension_semantics=("arbitrary",)),
        name="memkv",
    )(mem2d, norm_mem3, w_mem_bf)


def _prompt_layer_kernel(sink_ref, x_ref, gpre_ref, gpost_ref, win_ref, convw_ref, kcat_t_ref,
                         vcat_ref, wout_ref,
                         xo_ref, klast_ref, vlast_ref, convlast_ref,
                         kprev_ref, vprev_ref, ubuf_ref, *, tile, n_mem, n_tiles):
    i = pl.program_id(1)

    @pl.when(i == 0)
    def _():
        kprev_ref[...] = jnp.zeros_like(kprev_ref)
        vprev_ref[...] = jnp.zeros_like(vprev_ref)
        ubuf_ref[0:SUBLANES, :] = jnp.zeros((SUBLANES, CONV_DIM), F32)

    x = x_ref[...]
    h = _rmsnorm(x, gpre_ref[...]).astype(BF16)

    def proj(off, width):
        return _dot(h, win_ref[:, off:off + width])

    q_raw = proj(OFF_Q, ATTN_DIM)
    kv = proj(OFF_K, 2 * KV_DIM)
    mq_raw = proj(OFF_MQ, MEM_DIM)
    k = kv[:, :KV_DIM]
    v = kv[:, KV_DIM:]

    kfull = jnp.concatenate([kprev_ref[...], k], axis=0)
    vfull = jnp.concatenate([vprev_ref[...], v], axis=0)
    k_tail = k[tile - WINDOW:tile, :]
    v_tail = v[tile - WINDOW:tile, :]
    kprev_ref[...] = k_tail
    vprev_ref[...] = v_tail

    low = _low_lanes((WINDOW + tile, KV_DIM))
    klo = jnp.where(low, kfull, 0.0).astype(BF16)
    khi = jnp.where(low, 0.0, kfull).astype(BF16)
    vlo = jnp.where(low, vfull, 0.0).astype(BF16)
    vhi = jnp.where(low, 0.0, vfull).astype(BF16)

    q = q_raw * (SCALE * LOG2E)
    low_t = _low_lanes((tile, LANES))
    nat = [q[:, c * LANES:(c + 1) * LANES] for c in range(N_CHUNK)]
    qp = []
    for p in range(N_CHUNK):
        a, b = nat[p // 2], nat[CHUNKS_PER_KV + p // 2]
        pair = jnp.where(low_t, a, _swap_halves(b)) if p % 2 == 0 else jnp.where(low_t, _swap_halves(a), b)
        qp.append(pair.astype(BF16))

    rows = N_CHUNK * WINDOW
    qpos = lax.broadcasted_iota(jnp.int32, (rows, 2 * WINDOW), 0) % WINDOW
    kpos = lax.broadcasted_iota(jnp.int32, (rows, 2 * WINDOW), 1)
    band = (kpos > qpos) & (kpos <= qpos + WINDOW)
    band_first = band & ((kpos >= WINDOW) | (i > 0))
    chunk_of_row = lax.broadcasted_iota(jnp.int32, (rows, 1), 0) // WINDOW
    sink_lo = jnp.zeros((rows, 1), F32)
    sink_hi = jnp.zeros((rows, 1), F32)
    for p in range(N_CHUNK):
        sink_lo = jnp.where(chunk_of_row == p, sink_ref[p] * LOG2E, sink_lo)
        sink_hi = jnp.where(chunk_of_row == p, sink_ref[N_CHUNK + p] * LOG2E, sink_hi)
    low_o = _low_lanes((rows, LANES))
    low_w = _low_lanes((WINDOW, LANES))
    fillers = [("cg", OFF_CG), ("cb", OFF_CB), ("cc", OFF_CC), ("ch", OFF_CH),
               ("ag0", OFF_AG), ("ag1", OFF_AG + 2 * LANES), ("mg", OFF_MG)]
    filled = {}

    def fill_one():
        if fillers:
            name, off = fillers.pop(0)
            filled[name] = proj(off, 2 * LANES)

    def swa_scores(j):
        r0 = j * WINDOW
        q_all = jnp.concatenate([qp[p][r0:r0 + WINDOW] for p in range(N_CHUNK)], axis=0)
        k_cat = jnp.concatenate([klo[r0:r0 + 2 * WINDOW], khi[r0:r0 + 2 * WINDOW]], axis=0)
        return _dot_nt(q_all, k_cat)

    n_blocks = tile // WINDOW
    o_blocks = []
    fill_one()
    s = swa_scores(0)
    for j in range(n_blocks):
        r0 = j * WINDOW
        fill_one()
        s_next = swa_scores(j + 1) if j + 1 < n_blocks else None
        v_cat = jnp.concatenate([vlo[r0:r0 + 2 * WINDOW], vhi[r0:r0 + 2 * WINDOW]], axis=0)
        mask = band_first if j == 0 else band
        s_lo = jnp.where(mask, s[:, :2 * WINDOW], -jnp.inf)
        s_hi = jnp.where(mask, s[:, 2 * WINDOW:], -jnp.inf)
        m_lo = jnp.maximum(jnp.max(s_lo, axis=1, keepdims=True), sink_lo)
        m_hi = jnp.maximum(jnp.max(s_hi, axis=1, keepdims=True), sink_hi)
        p_lo = jnp.exp2(s_lo - m_lo)
        p_hi = jnp.exp2(s_hi - m_hi)
        d_lo = jnp.sum(p_lo, axis=1, keepdims=True) + jnp.exp2(sink_lo - m_lo)
        d_hi = jnp.sum(p_hi, axis=1, keepdims=True) + jnp.exp2(sink_hi - m_hi)
        pr = jnp.concatenate([p_lo, p_hi], axis=1).astype(BF16)
        o = _dot(pr, v_cat)
        o = o * jnp.where(low_o, 1.0 / d_lo, 1.0 / d_hi)
        op = [o[p * WINDOW:(p + 1) * WINDOW] for p in range(N_CHUNK)]
        o_blocks.append(jnp.concatenate(
            [jnp.where(low_w, op[0], _swap_halves(op[1])), jnp.where(low_w, op[2], _swap_halves(op[3])),
             jnp.where(low_w, _swap_halves(op[0]), op[1]), jnp.where(low_w, _swap_halves(op[2]), op[3])],
            axis=1))
        s = s_next
    o_b = jnp.concatenate(o_blocks, axis=0) if len(o_blocks) > 1 else o_blocks[0]

    mq = (mq_raw * (SCALE * LOG2E)).astype(BF16)
    oc_chunks = []
    for c in range(M_CHUNK):
        s = _dot(mq[:, c * LANES:(c + 1) * LANES], kcat_t_ref[c])
        fill_one()
        s0 = s[:, :n_mem]
        s1 = s[:, n_mem:]
        p0 = jnp.exp2(s0 - jnp.max(s0, axis=1, keepdims=True))
        p1 = jnp.exp2(s1 - jnp.max(s1, axis=1, keepdims=True))
        d0 = jnp.sum(p0, axis=1, keepdims=True)
        d1 = jnp.sum(p1, axis=1, keepdims=True)
        p = jnp.concatenate([p0, p1], axis=1).astype(BF16)
        o = _dot(p, vcat_ref[c])
        oc_chunks.append(o * jnp.where(low_t, 1.0 / d0, 1.0 / d1))
    o_c = jnp.concatenate(oc_chunks, axis=1)
    while fillers:
        fill_one()
    out_b = _silu(jnp.concatenate([filled["ag0"], filled["ag1"]], axis=1)) * o_b
    out_c = _silu(filled["mg"]) * o_c

    u = filled["cc"] * filled["ch"]
    ubuf_ref[SUBLANES:SUBLANES + tile, :] = u
    u1 = ubuf_ref[SUBLANES - 1:SUBLANES - 1 + tile, :]
    u2 = ubuf_ref[SUBLANES - 2:SUBLANES - 2 + tile, :]
    cw = convw_ref[...]
    conv = cw[0:1, :] * u2 + cw[1:2, :] * u1 + cw[2:3, :] * u
    out_a = _silu(filled["cg"]) * filled["cb"] * conv
    u_tail = u[tile - SUBLANES:tile, :]
    ubuf_ref[0:SUBLANES, :] = u_tail
    convlast_ref[...] = u_tail

    mix = jnp.concatenate([out_a, out_b, out_c], axis=1).astype(BF16)
    half = tile // 2
    y0 = _dot(mix[:half], wout_ref[...])
    y1 = _dot(mix[half:], wout_ref[...])
    xo_ref[:half, :] = x[:half] + _rmsnorm(y0, gpost_ref[...])
    xo_ref[half:, :] = x[half:] + _rmsnorm(y1, gpost_ref[...])

    @pl.when(i == n_tiles - 1)
    def _():
        klast_ref[...] = kprev_ref[...].T
        vlast_ref[...] = vprev_ref[...].T


def _prompt_layer(l, x, sinks, gpre3, gpost3, w_in_bf, conv_w, kcat_t, vcat, w_out_bf, n_mem):
    batch, seq, _ = x.shape
    tile = PROMPT_TILE
    nt = seq // tile
    return pl.pallas_call(
        functools.partial(_prompt_layer_kernel, tile=tile, n_mem=n_mem, n_tiles=nt),
        grid=(batch, nt),
        in_specs=[
            pl.BlockSpec(memory_space=pltpu.SMEM),
            pl.BlockSpec((None, tile, D_MODEL), lambda b, i: (b, i, 0)),
            pl.BlockSpec((None, 1, D_MODEL), lambda b, i: (l, 0, 0)),
            pl.BlockSpec((None, 1, D_MODEL), lambda b, i: (l, 0, 0)),
            pl.BlockSpec((None, D_MODEL, IN_DIM), lambda b, i: (l, 0, 0)),
            pl.BlockSpec((None, CONV_W, CONV_DIM), lambda b, i: (l, 0, 0)),
            pl.BlockSpec((None, None, M_CHUNK, LANES, 2 * n_mem), lambda b, i: (l, b, 0, 0, 0)),
            pl.BlockSpec((None, None, M_CHUNK, 2 * n_mem, LANES), lambda b, i: (l, b, 0, 0, 0)),
            pl.BlockSpec((None, MIX_DIM, D_MODEL), lambda b, i: (l, 0, 0)),
        ],
        out_specs=[
            pl.BlockSpec((None, tile, D_MODEL), lambda b, i: (b, i, 0)),
            pl.BlockSpec((None, KV_DIM, WINDOW), lambda b, i: (b, 0, 0)),
            pl.BlockSpec((None, KV_DIM, WINDOW), lambda b, i: (b, 0, 0)),
            pl.BlockSpec((None, SUBLANES, CONV_DIM), lambda b, i: (b, 0, 0)),
        ],
        out_shape=[
            jax.ShapeDtypeStruct((batch, seq, D_MODEL), F32),
            jax.ShapeDtypeStruct((batch, KV_DIM, WINDOW), F32),
            jax.ShapeDtypeStruct((batch, KV_DIM, WINDOW), F32),
            jax.ShapeDtypeStruct((batch, SUBLANES, CONV_DIM), F32),
        ],
        scratch_shapes=[
            pltpu.VMEM((WINDOW, KV_DIM), F32),
            pltpu.VMEM((WINDOW, KV_DIM), F32),
            pltpu.VMEM((SUBLANES + tile, CONV_DIM), F32),
        ],
        compiler_params=pltpu.CompilerParams(
            dimension_semantics=("arbitrary", "arbitrary"), vmem_limit_bytes=VMEM_LIMIT),
        name="prompt_layer",
    )(sinks[l], x, gpre3, gpost3, w_in_bf, conv_w, kcat_t, vcat, w_out_bf)


def _sample_kernel(x_ref, gpre_ref, gpost_ref, win_ref, wout_ref, convw_ref, sink_ref, convp_ref,
                   kt_ref, vt_ref, mkt_ref, mvt_ref,
                   y_ref, kto_ref, vto_ref, uo_ref,
                   xs_ref, z_ref, mix_ref, *, group, t, n_mem, n_sub):
    l = pl.program_id(0)
    p = pl.program_id(1)
    ss = pl.program_id(2)
    nrows = group * t
    prows = n_sub * nrows
    rows_p = pl.ds(pl.multiple_of(p * prows, prows), prows)
    rows_s = pl.ds(pl.multiple_of(ss * nrows, nrows), nrows)

    @pl.when(ss == 0)
    def _():
        @pl.when(l == 0)
        def _():
            xs_ref[rows_p, :] = x_ref[...]

        h = _rmsnorm(xs_ref[rows_p, :], gpre_ref[...]).astype(BF16)
        z_ref[...] = _dot(h, win_ref[...])

    def zc(off, width):
        return z_ref[rows_s, off:off + width]

    def per_seq(a):
        return a.reshape(group, t, a.shape[-1])

    u2d = zc(OFF_CC, CONV_DIM) * zc(OFF_CH, CONV_DIM)
    uo_ref[...] = u2d
    u = per_seq(u2d)
    prev = convp_ref[...]
    tpos = lax.broadcasted_iota(jnp.int32, (group, t, CONV_DIM), 1)
    u1 = jnp.where(tpos >= 1, pltpu.roll(u, 1, 1), pltpu.roll(prev, 1, 1))
    u2 = jnp.where(tpos >= 2, pltpu.roll(u, 2, 1), pltpu.roll(prev, 2, 1))
    cw = convw_ref[...]
    conv = cw[0:1, :] * u2 + cw[1:2, :] * u1 + cw[2:3, :] * u
    out_a = per_seq(_silu(zc(OFF_CG, CONV_DIM)) * zc(OFF_CB, CONV_DIM)) * conv

    kt_old = kt_ref[...]
    vt_old = vt_ref[...]
    k_new = zc(OFF_K, KV_DIM)
    v_new = zc(OFF_V, KV_DIM)
    k_new_t = jnp.swapaxes(per_seq(k_new), 1, 2)
    v_new_t = jnp.swapaxes(per_seq(v_new), 1, 2)
    kto_ref[...] = pltpu.roll(jnp.concatenate([k_new_t, kt_old[:, :, t:]], axis=2), WINDOW - t, 2)
    vto_ref[...] = pltpu.roll(jnp.concatenate([v_new_t, vt_old[:, :, t:]], axis=2), WINDOW - t, 2)

    q = per_seq(zc(OFF_Q, ATTN_DIM) * SCALE)
    low_q = _low_lanes((group, t, LANES))
    pieces = []
    for c in range(N_CHUNK):
        qc = q[:, :, c * LANES:(c + 1) * LANES]
        qs = _swap_halves(qc)
        if c // CHUNKS_PER_KV == 0:
            pieces += [jnp.where(low_q, qc, 0.0), jnp.where(low_q, qs, 0.0)]
        else:
            pieces += [jnp.where(low_q, 0.0, qs), jnp.where(low_q, 0.0, qc)]
    qbd = jnp.concatenate(pieces, axis=1).astype(BF16)
    nrow = ATTN_HEADS * t
    s_old = jnp.einsum('gqd,gdk->gqk', qbd, kt_old.astype(BF16), preferred_element_type=F32)
    s_new = _dot_nt(qbd.reshape(group * nrow, KV_DIM), k_new.astype(BF16)).reshape(group, nrow, group * t)
    tq = lax.broadcasted_iota(jnp.int32, (group, nrow, WINDOW), 1) % t
    col = lax.broadcasted_iota(jnp.int32, (group, nrow, WINDOW), 2)
    first_new = lax.broadcasted_iota(jnp.int32, (group, nrow, WINDOW), 0) * t
    s_old = jnp.where(col > tq, s_old, -jnp.inf)
    s_new = jnp.where((col >= first_new) & (col <= first_new + tq), s_new, -jnp.inf)
    sink = sink_ref[...]
    m = jnp.maximum(jnp.max(jnp.maximum(s_old, s_new), axis=2, keepdims=True), sink)
    p_old = jnp.exp(s_old - m)
    p_new = jnp.exp(s_new - m)
    d = jnp.sum(p_old + p_new, axis=2, keepdims=True) + jnp.exp(sink - m)
    o = jnp.einsum('gqk,gdk->gqd', p_old.astype(BF16), vt_old.astype(BF16), preferred_element_type=F32)
    o = o + _dot(p_new.astype(BF16).reshape(group * nrow, group * t), v_new.astype(BF16)).reshape(group, nrow, KV_DIM)
    o = o * (1.0 / d)
    ob_chunks = []
    for c in range(N_CHUNK):
        o_even = o[:, 2 * c * t:(2 * c + 1) * t]
        o_odd = o[:, (2 * c + 1) * t:(2 * c + 2) * t]
        if c // CHUNKS_PER_KV == 0:
            ob_chunks.append(jnp.where(low_q, o_even, _swap_halves(o_odd)))
        else:
            ob_chunks.append(jnp.where(low_q, _swap_halves(o_even), o_odd))
    o_b = jnp.concatenate(ob_chunks, axis=2)
    out_b = per_seq(_silu(zc(OFF_AG, ATTN_DIM))) * o_b

    mq = per_seq(zc(OFF_MQ, MEM_DIM) * SCALE)
    head_of_lane = lax.broadcasted_iota(jnp.int32, (group, t, MEM_DIM), 2) // HEAD_DIM
    qm = jnp.concatenate([jnp.where(head_of_lane == hh, mq, 0.0) for hh in range(MEM_HEADS)],
                         axis=1).astype(BF16)
    s = jnp.einsum('gqd,gdk->gqk', qm, mkt_ref[...].astype(BF16), preferred_element_type=F32)
    p = jnp.exp(s - jnp.max(s, axis=2, keepdims=True))
    d = jnp.sum(p, axis=2, keepdims=True)
    o = jnp.einsum('gqk,gdk->gqd', p.astype(BF16), mvt_ref[...].astype(BF16), preferred_element_type=F32)
    o = o * (1.0 / d)
    o_c = jnp.zeros((group, t, MEM_DIM), F32)
    for hh in range(MEM_HEADS):
        o_c = jnp.where(head_of_lane == hh, o[:, hh * t:(hh + 1) * t], o_c)
    out_c = per_seq(_silu(zc(OFF_MG, MEM_DIM))) * o_c

    mix_ref[rows_s, :] = jnp.concatenate([out_a, out_b, out_c], axis=2).reshape(nrows, MIX_DIM).astype(BF16)

    @pl.when(ss == n_sub - 1)
    def _():
        y = _dot(mix_ref[...], wout_ref[...])
        x_new = xs_ref[rows_p, :] + _rmsnorm(y, gpost_ref[...])
        xs_ref[rows_p, :] = x_new

        @pl.when(l == DEPTH - 1)
        def _():
            y_ref[...] = x_new


def _sample_stream(x2d, gpre3, gpost3, w_in_bf, w_out_bf, conv_w, sink_rows, convp, cache_kt, cache_vt,
                   cache_mkt, cache_mvt, n_seq, t, n_mem):
    group = SAMPLE_GROUP
    n_sub = SAMPLE_SUBSTEPS
    n_proj = n_seq // (group * n_sub)
    nrows = group * t
    prows = n_sub * nrows
    nrow_attn = ATTN_HEADS * t
    per_layer = lambda l, p, ss: (l, 0, 0)
    per_group = lambda l, p, ss: (l, p * n_sub + ss, 0, 0)
    return pl.pallas_call(
        functools.partial(_sample_kernel, group=group, t=t, n_mem=n_mem, n_sub=n_sub),
        grid=(DEPTH, n_proj, n_sub),
        in_specs=[
            pl.BlockSpec((prows, D_MODEL), lambda l, p, ss: (jnp.where(l == 0, p, n_proj - 1), 0)),
            pl.BlockSpec((None, 1, D_MODEL), per_layer),
            pl.BlockSpec((None, 1, D_MODEL), per_layer),
            pl.BlockSpec((None, D_MODEL, IN_DIM), per_layer),
            pl.BlockSpec((None, MIX_DIM, D_MODEL), per_layer),
            pl.BlockSpec((None, CONV_W, CONV_DIM), per_layer),
            pl.BlockSpec((None, nrow_attn, 1), per_layer),
            pl.BlockSpec((None, group, t, CONV_DIM), per_group),
            pl.BlockSpec((None, group, KV_DIM, WINDOW), per_group),
            pl.BlockSpec((None, group, KV_DIM, WINDOW), per_group),
            pl.BlockSpec((None, group, MEM_DIM, n_mem), per_group),
            pl.BlockSpec((None, group, MEM_DIM, n_mem), per_group),
        ],
        out_specs=[
            pl.BlockSpec((prows, D_MODEL), lambda l, p, ss: (jnp.where(l == DEPTH - 1, p, 0), 0)),
            pl.BlockSpec((None, group, KV_DIM, WINDOW), per_group),
            pl.BlockSpec((None, group, KV_DIM, WINDOW), per_group),
            pl.BlockSpec((None, nrows, CONV_DIM), lambda l, p, ss: (l, p * n_sub + ss, 0)),
        ],
        out_shape=[
            jax.ShapeDtypeStruct((n_seq * t, D_MODEL), F32),
            jax.ShapeDtypeStruct((DEPTH, n_seq, KV_DIM, WINDOW), F32),
            jax.ShapeDtypeStruct((DEPTH, n_seq, KV_DIM, WINDOW), F32),
            jax.ShapeDtypeStruct((DEPTH, n_seq * t, CONV_DIM), F32),
        ],
        scratch_shapes=[
            pltpu.VMEM((n_seq * t, D_MODEL), F32),
            pltpu.VMEM((prows, IN_DIM), F32),
            pltpu.VMEM((prows, MIX_DIM), BF16),
        ],
        compiler_params=pltpu.CompilerParams(
            dimension_semantics=("arbitrary", "arbitrary", "arbitrary"), vmem_limit_bytes=VMEM_LIMIT),
        name="sample_stream",
    )(x2d, gpre3, gpost3, w_in_bf, w_out_bf, conv_w, sink_rows, convp, cache_kt, cache_vt, cache_mkt, cache_mvt)


def _keys_minor(a):
    lead = a.shape[:-3]
    n, heads, hd = a.shape[-3:]
    nd = a.ndim
    perm = tuple(range(nd - 3)) + (nd - 2, nd - 1, nd - 3)
    return jnp.transpose(a, perm).reshape(*lead, heads * hd, n)


def _keys_major(a, heads):
    lead = a.shape[:-2]
    n = a.shape[-1]
    a = a.reshape(*lead, heads, HEAD_DIM, n)
    nd = a.ndim
    perm = tuple(range(nd - 3)) + (nd - 1, nd - 3, nd - 2)
    return jnp.transpose(a, perm)


def kernel(x_prompt, x_sample, mem_prompt, cache_win_k, cache_win_v, state_conv, cache_mem_k, cache_mem_v,
           norm_pre, norm_post, norm_mem, w_in, conv_w, attn_sinks, w_mem_kv, w_out):
    batch, seq, _ = x_prompt.shape
    n_seq, t, _ = x_sample.shape
    n_mem = mem_prompt.shape[1]
    assert seq % PROMPT_TILE == 0 and PROMPT_TILE % WINDOW == 0
    assert n_seq % (SAMPLE_GROUP * SAMPLE_SUBSTEPS) == 0 and t == SUBLANES

    w_in_bf = w_in.astype(BF16)
    w_out_bf = w_out.astype(BF16)
    w_mem_bf = w_mem_kv.astype(BF16)
    sink_rows = jnp.repeat(attn_sinks, t, axis=1)[:, :, None]
    gpre3 = norm_pre[:, None, :]
    gpost3 = norm_post[:, None, :]
    gmem3 = norm_mem[:, None, :]

    mkt, mvt, kcat_t, vcat = _memkv(mem_prompt.reshape(batch * n_mem, D_MODEL), gmem3, w_mem_bf, batch, n_mem)

    convp = jnp.pad(state_conv, ((0, 0), (0, 0), (t - CONV_BUF, 0), (0, 0)))
    ys, kts, vts, us = _sample_stream(
        x_sample.reshape(n_seq * t, D_MODEL), gpre3, gpost3, w_in_bf, w_out_bf, conv_w, sink_rows, convp,
        _keys_minor(cache_win_k), _keys_minor(cache_win_v), _keys_minor(cache_mem_k), _keys_minor(cache_mem_v),
        n_seq, t, n_mem)

    xp = x_prompt
    ktp, vtp, cvp = [], [], []
    for l in range(DEPTH):
        xp, k_p, v_p, c_p = _prompt_layer(l, xp, attn_sinks, gpre3, gpost3, w_in_bf, conv_w, kcat_t, vcat,
                                          w_out_bf, n_mem)
        ktp.append(k_p)
        vtp.append(v_p)
        cvp.append(c_p[:, SUBLANES - CONV_BUF:, :])

    return (xp,
            ys.reshape(n_seq, t, D_MODEL),
            _keys_major(jnp.stack(ktp), KV_HEADS),
            _keys_major(jnp.stack(vtp), KV_HEADS),
            jnp.stack(cvp),
            _keys_major(mkt, MEM_HEADS),
            _keys_major(mvt, MEM_HEADS),
            _keys_major(kts, KV_HEADS),
            _keys_major(vts, KV_HEADS),
            us.reshape(DEPTH, n_seq, t, CONV_DIM)[:, :, t - CONV_BUF:, :])
```

```python
import functools

import jax
import jax.numpy as jnp
from jax import lax
from jax.experimental import pallas as pl
from jax.experimental.pallas import tpu as pltpu

D_MODEL = 1024
DEPTH = 4
HEAD_DIM = 64
ATTN_HEADS = 8
KV_HEADS = 2
ATTN_DIM = ATTN_HEADS * HEAD_DIM
KV_DIM = KV_HEADS * HEAD_DIM
WINDOW = 128
MEM_HEADS = 4
MEM_DIM = MEM_HEADS * HEAD_DIM
CONV_DIM = 256
CONV_W = 3
CONV_BUF = CONV_W - 1
MIX_DIM = CONV_DIM + ATTN_DIM + MEM_DIM
IN_DIM = 4 * CONV_DIM + 2 * ATTN_DIM + 2 * KV_DIM + 2 * MEM_DIM
RMS_EPS = 1e-6
SCALE = HEAD_DIM ** -0.5
LOG2E = 1.4426950408889634

LANES = 128
SUBLANES = 8
VMEM_LIMIT = 56 * 1024 * 1024

OFF_CB, OFF_CC, OFF_CH, OFF_CG = 0, 256, 512, 768
OFF_Q, OFF_K, OFF_V, OFF_AG = 1024, 1536, 1664, 1792
OFF_MQ, OFF_MG = 2304, 2560

N_CHUNK = ATTN_DIM // LANES
M_CHUNK = MEM_DIM // LANES
CHUNKS_PER_KV = N_CHUNK // KV_HEADS

PROMPT_TILE = 1024
PROMPT_PARTS = 2
SAMPLE_GROUP = 16
SAMPLE_SUBSTEPS = 2

F32 = jnp.float32
BF16 = jnp.bfloat16


def _rmsnorm(x, g):
    r = lax.rsqrt(jnp.mean(x * x, axis=-1, keepdims=True) + RMS_EPS)
    return (x * r) * g


def _silu(x):
    return x * jax.nn.sigmoid(x)


def _dot(a, b):
    return jnp.dot(a, b, preferred_element_type=F32)


def _dot_nt(a, b):
    return lax.dot_general(a, b, (((1,), (1,)), ((), ())), preferred_element_type=F32)


def _low_lanes(shape):
    return lax.broadcasted_iota(jnp.int32, shape, len(shape) - 1) < HEAD_DIM


def _swap_halves(a):
    return pltpu.roll(a, HEAD_DIM, a.ndim - 1)


def _memkv_kernel(mem_ref, g_ref, w_ref, mkt_ref, mvt_ref, kcat_t_ref, vcat_ref, *, batch, n_mem):
    h = _rmsnorm(mem_ref[...], g_ref[...]).astype(BF16)
    kv = _dot(h, w_ref[...])
    mk = kv[:, :MEM_DIM]
    mv = kv[:, MEM_DIM:]
    low = _low_lanes((n_mem, LANES))
    for b in range(batch):
        mkt_ref[b] = mk[b * n_mem:(b + 1) * n_mem, :].T
        mvt_ref[b] = mv[b * n_mem:(b + 1) * n_mem, :].T
        for c in range(M_CHUNK):
            kc = mk[b * n_mem:(b + 1) * n_mem, c * LANES:(c + 1) * LANES]
            vc = mv[b * n_mem:(b + 1) * n_mem, c * LANES:(c + 1) * LANES]
            kcat = jnp.concatenate([jnp.where(low, kc, 0.0), jnp.where(low, 0.0, kc)], axis=0)
            vcat = jnp.concatenate([jnp.where(low, vc, 0.0), jnp.where(low, 0.0, vc)], axis=0)
            kcat_t_ref[b, c] = kcat.T.astype(BF16)
            vcat_ref[b, c] = vcat.astype(BF16)


def _memkv(mem2d, norm_mem3, w_mem_bf, batch, n_mem):
    rows = batch * n_mem
    return pl.pallas_call(
        functools.partial(_memkv_kernel, batch=batch, n_mem=n_mem),
        grid=(DEPTH,),
        in_specs=[
            pl.BlockSpec((rows, D_MODEL), lambda l: (0, 0)),
            pl.BlockSpec((None, 1, D_MODEL), lambda l: (l, 0, 0)),
            pl.BlockSpec((None, D_MODEL, 2 * MEM_DIM), lambda l: (l, 0, 0)),
        ],
        out_specs=[
            pl.BlockSpec((None, batch, MEM_DIM, n_mem), lambda l: (l, 0, 0, 0)),
            pl.BlockSpec((None, batch, MEM_DIM, n_mem), lambda l: (l, 0, 0, 0)),
            pl.BlockSpec((None, batch, M_CHUNK, LANES, 2 * n_mem), lambda l: (l, 0, 0, 0, 0)),
            pl.BlockSpec((None, batch, M_CHUNK, 2 * n_mem, LANES), lambda l: (l, 0, 0, 0, 0)),
        ],
        out_shape=[
            jax.ShapeDtypeStruct((DEPTH, batch, MEM_DIM, n_mem), F32),
            jax.ShapeDtypeStruct((DEPTH, batch, MEM_DIM, n_mem), F32),
            jax.ShapeDtypeStruct((DEPTH, batch, M_CHUNK, LANES, 2 * n_mem), BF16),
            jax.ShapeDtypeStruct((DEPTH, batch, M_CHUNK, 2 * n_mem, LANES), BF16),
        ],
        compiler_params=pltpu.CompilerParams(dimension_semantics=("arbitrary",)),
        name="memkv",
    )(mem2d, norm_mem3, w_mem_bf)


def _prompt_layer_kernel(sink_ref, x_ref, gpre_ref, gpost_ref, win_ref, convw_ref, kcat_t_ref,
                         vcat_ref, wout_ref,
                         xo_ref, klast_ref, vlast_ref, convlast_ref,
                         kprev_ref, vprev_ref, ubuf_ref, *, tile, n_mem, n_tiles):
    i = pl.program_id(1)
    part = tile // PROMPT_PARTS
    n_blocks = part // WINDOW

    @pl.when(i == 0)
    def _():
        kprev_ref[...] = jnp.zeros_like(kprev_ref)
        vprev_ref[...] = jnp.zeros_like(vprev_ref)
        ubuf_ref[0:SUBLANES, :] = jnp.zeros((SUBLANES, CONV_DIM), F32)

    rows = N_CHUNK * WINDOW
    qpos = lax.broadcasted_iota(jnp.int32, (rows, 2 * WINDOW), 0) % WINDOW
    kpos = lax.broadcasted_iota(jnp.int32, (rows, 2 * WINDOW), 1)
    band = (kpos > qpos) & (kpos <= qpos + WINDOW)
    band_first = band & ((kpos >= WINDOW) | (i > 0))
    chunk_of_row = lax.broadcasted_iota(jnp.int32, (rows, 1), 0) // WINDOW
    sink_lo = jnp.zeros((rows, 1), F32)
    sink_hi = jnp.zeros((rows, 1), F32)
    for p in range(N_CHUNK):
        sink_lo = jnp.where(chunk_of_row == p, sink_ref[p] * LOG2E, sink_lo)
        sink_hi = jnp.where(chunk_of_row == p, sink_ref[N_CHUNK + p] * LOG2E, sink_hi)
    low_o = _low_lanes((rows, LANES))
    low_w = _low_lanes((WINDOW, LANES))
    low_t = _low_lanes((part, LANES))
    low_kv = _low_lanes((WINDOW + part, KV_DIM))

    def start(pi):
        x = x_ref[pi * part:(pi + 1) * part, :]
        h = _rmsnorm(x, gpre_ref[...]).astype(BF16)
        proj = lambda off, width: _dot(h, win_ref[:, off:off + width])
        return dict(x=x, proj=proj, q_raw=proj(OFF_Q, ATTN_DIM), kv=proj(OFF_K, 2 * KV_DIM),
                    mq_raw=proj(OFF_MQ, MEM_DIM))

    def mix_part(pi, st, k_prev, v_prev, fill_one):
        k = st["kv"][:, :KV_DIM]
        v = st["kv"][:, KV_DIM:]
        kfull = jnp.concatenate([k_prev, k], axis=0)
        vfull = jnp.concatenate([v_prev, v], axis=0)
        klo = jnp.where(low_kv, kfull, 0.0).astype(BF16)
        khi = jnp.where(low_kv, 0.0, kfull).astype(BF16)
        vlo = jnp.where(low_kv, vfull, 0.0).astype(BF16)
        vhi = jnp.where(low_kv, 0.0, vfull).astype(BF16)

        q = st["q_raw"] * (SCALE * LOG2E)
        nat = [q[:, c * LANES:(c + 1) * LANES] for c in range(N_CHUNK)]
        qp = []
        for p in range(N_CHUNK):
            a, b = nat[p // 2], nat[CHUNKS_PER_KV + p // 2]
            pair = jnp.where(low_t, a, _swap_halves(b)) if p % 2 == 0 else jnp.where(low_t, _swap_halves(a), b)
            qp.append(pair.astype(BF16))

        def swa_scores(j):
            r0 = j * WINDOW
            q_all = jnp.concatenate([qp[p][r0:r0 + WINDOW] for p in range(N_CHUNK)], axis=0)
            k_cat = jnp.concatenate([klo[r0:r0 + 2 * WINDOW], khi[r0:r0 + 2 * WINDOW]], axis=0)
            return _dot_nt(q_all, k_cat)

        o_blocks = []
        fill_one()
        s = swa_scores(0)
        for j in range(n_blocks):
            r0 = j * WINDOW
            fill_one()
            s_next = swa_scores(j + 1) if j + 1 < n_blocks else None
            v_cat = jnp.concatenate([vlo[r0:r0 + 2 * WINDOW], vhi[r0:r0 + 2 * WINDOW]], axis=0)
            mask = band_first if (pi == 0 and j == 0) else band
            s_lo = jnp.where(mask, s[:, :2 * WINDOW], -jnp.inf)
            s_hi = jnp.where(mask, s[:, 2 * WINDOW:], -jnp.inf)
            m_lo = jnp.maximum(jnp.max(s_lo, axis=1, keepdims=True), sink_lo)
            m_hi = jnp.maximum(jnp.max(s_hi, axis=1, keepdims=True), sink_hi)
            p_lo = jnp.exp2(s_lo - m_lo)
            p_hi = jnp.exp2(s_hi - m_hi)
            d_lo = jnp.sum(p_lo, axis=1, keepdims=True) + jnp.exp2(sink_lo - m_lo)
            d_hi = jnp.sum(p_hi, axis=1, keepdims=True) + jnp.exp2(sink_hi - m_hi)
            pr = jnp.concatenate([p_lo, p_hi], axis=1).astype(BF16)
            o = _dot(pr, v_cat)
            o = o * jnp.where(low_o, 1.0 / d_lo, 1.0 / d_hi)
            op = [o[p * WINDOW:(p + 1) * WINDOW] for p in range(N_CHUNK)]
            o_blocks.append(jnp.concatenate(
                [jnp.where(low_w, op[0], _swap_halves(op[1])), jnp.where(low_w, op[2], _swap_halves(op[3])),
                 jnp.where(low_w, _swap_halves(op[0]), op[1]), jnp.where(low_w, _swap_halves(op[2]), op[3])],
                axis=1))
            s = s_next
        o_b = jnp.concatenate(o_blocks, axis=0) if len(o_blocks) > 1 else o_blocks[0]

        mq = (st["mq_raw"] * (SCALE * LOG2E)).astype(BF16)
        oc_chunks = []
        for c in range(M_CHUNK):
            s = _dot(mq[:, c * LANES:(c + 1) * LANES], kcat_t_ref[c])
            fill_one()
            s0 = s[:, :n_mem]
            s1 = s[:, n_mem:]
            p0 = jnp.exp2(s0 - jnp.max(s0, axis=1, keepdims=True))
            p1 = jnp.exp2(s1 - jnp.max(s1, axis=1, keepdims=True))
            d0 = jnp.sum(p0, axis=1, keepdims=True)
            d1 = jnp.sum(p1, axis=1, keepdims=True)
            pm = jnp.concatenate([p0, p1], axis=1).astype(BF16)
            o = _dot(pm, vcat_ref[c])
            oc_chunks.append(o * jnp.where(low_t, 1.0 / d0, 1.0 / d1))
        o_c = jnp.concatenate(oc_chunks, axis=1)
        return o_b, o_c, k[part - WINDOW:part, :], v[part - WINDOW:part, :]

    def gate_mix(pi, filled, o_b, o_c):
        out_b = _silu(jnp.concatenate([filled["ag0"], filled["ag1"]], axis=1)) * o_b
        out_c = _silu(filled["mg"]) * o_c
        u = filled["cc"] * filled["ch"]
        base = SUBLANES + pi * part
        ubuf_ref[base:base + part, :] = u
        u1 = ubuf_ref[base - 1:base - 1 + part, :]
        u2 = ubuf_ref[base - 2:base - 2 + part, :]
        cw = convw_ref[...]
        conv = cw[0:1, :] * u2 + cw[1:2, :] * u1 + cw[2:3, :] * u
        out_a = _silu(filled["cg"]) * filled["cb"] * conv
        return jnp.concatenate([out_a, out_b, out_c], axis=1).astype(BF16), u[part - SUBLANES:part, :]

    def finish(pi, st, mix):
        half = part // 2
        y0 = _dot(mix[:half], wout_ref[...])
        y1 = _dot(mix[half:], wout_ref[...])
        r0 = pi * part
        xo_ref[r0:r0 + half, :] = st["x"][:half] + _rmsnorm(y0, gpost_ref[...])
        xo_ref[r0 + half:r0 + part, :] = st["x"][half:] + _rmsnorm(y1, gpost_ref[...])

    filler_cols = [("cg", OFF_CG), ("cb", OFF_CB), ("cc", OFF_CC), ("ch", OFF_CH),
                   ("ag0", OFF_AG), ("ag1", OFF_AG + 2 * LANES), ("mg", OFF_MG)]

    states = [start(0)]
    k_prev, v_prev = kprev_ref[...], vprev_ref[...]
    u_tail = None
    for pi in range(PROMPT_PARTS):
        st = states[pi]
        pending = list(filler_cols)
        filled = {}

        def fill_one():
            if pending:
                name, off = pending.pop(0)
                filled[name] = st["proj"](off, 2 * LANES)

        if pi + 1 < PROMPT_PARTS:
            states.append(start(pi + 1))
        o_b, o_c, k_prev, v_prev = mix_part(pi, st, k_prev, v_prev, fill_one)
        while pending:
            fill_one()
        mix, u_tail = gate_mix(pi, filled, o_b, o_c)
        finish(pi, st, mix)

    kprev_ref[...] = k_prev
    vprev_ref[...] = v_prev
    ubuf_ref[0:SUBLANES, :] = u_tail
    convlast_ref[...] = u_tail

    @pl.when(i == n_tiles - 1)
    def _():
        klast_ref[...] = kprev_ref[...].T
        vlast_ref[...] = vprev_ref[...].T


def _prompt_layer(l, x, sinks, gpre3, gpost3, w_in_bf, conv_w, kcat_t, vcat, w_out_bf, n_mem):
    batch, seq, _ = x.shape
    tile = PROMPT_TILE
    nt = seq // tile
    return pl.pallas_call(
        functools.partial(_prompt_layer_kernel, tile=tile, n_mem=n_mem, n_tiles=nt),
        grid=(batch, nt),
        in_specs=[
            pl.BlockSpec(memory_space=pltpu.SMEM),
            pl.BlockSpec((None, tile, D_MODEL), lambda b, i: (b, i, 0)),
            pl.BlockSpec((None, 1, D_MODEL), lambda b, i: (l, 0, 0)),
            pl.BlockSpec((None, 1, D_MODEL), lambda b, i: (l, 0, 0)),
            pl.BlockSpec((None, D_MODEL, IN_DIM), lambda b, i: (l, 0, 0)),
            pl.BlockSpec((None, CONV_W, CONV_DIM), lambda b, i: (l, 0, 0)),
            pl.BlockSpec((None, None, M_CHUNK, LANES, 2 * n_mem), lambda b, i: (l, b, 0, 0, 0)),
            pl.BlockSpec((None, None, M_CHUNK, 2 * n_mem, LANES), lambda b, i: (l, b, 0, 0, 0)),
            pl.BlockSpec((None, MIX_DIM, D_MODEL), lambda b, i: (l, 0, 0)),
        ],
        out_specs=[
            pl.BlockSpec((None, tile, D_MODEL), lambda b, i: (b, i, 0)),
            pl.BlockSpec((None, KV_DIM, WINDOW), lambda b, i: (b, 0, 0)),
            pl.BlockSpec((None, KV_DIM, WINDOW), lambda b, i: (b, 0, 0)),
            pl.BlockSpec((None, SUBLANES, CONV_DIM), lambda b, i: (b, 0, 0)),
        ],
        out_shape=[
            jax.ShapeDtypeStruct((batch, seq, D_MODEL), F32),
            jax.ShapeDtypeStruct((batch, KV_DIM, WINDOW), F32),
            jax.ShapeDtypeStruct((batch, KV_DIM, WINDOW), F32),
            jax.ShapeDtypeStruct((batch, SUBLANES, CONV_DIM), F32),
        ],
        scratch_shapes=[
            pltpu.VMEM((WINDOW, KV_DIM), F32),
            pltpu.VMEM((WINDOW, KV_DIM), F32),
            pltpu.VMEM((SUBLANES + tile, CONV_DIM), F32),
        ],
        compiler_params=pltpu.CompilerParams(
            dimension_semantics=("arbitrary", "arbitrary"), vmem_limit_bytes=VMEM_LIMIT),
        name="prompt_layer",
    )(sinks[l], x, gpre3, gpost3, w_in_bf, conv_w, kcat_t, vcat, w_out_bf)


def _sample_kernel(x_ref, gpre_ref, gpost_ref, win_ref, wout_ref, convw_ref, sink_ref, convp_ref,
                   kt_ref, vt_ref, mkt_ref, mvt_ref,
                   y_ref, kto_ref, vto_ref, uo_ref,
                   xs_ref, z_ref, mix_ref, *, group, t, n_mem, n_sub):
    l = pl.program_id(0)
    p = pl.program_id(1)
    ss = pl.program_id(2)
    nrows = group * t
    prows = n_sub * nrows
    rows_p = pl.ds(pl.multiple_of(p * prows, prows), prows)
    rows_s = pl.ds(pl.multiple_of(ss * nrows, nrows), nrows)

    @pl.when(ss == 0)
    def _():
        @pl.when(l == 0)
        def _():
            xs_ref[rows_p, :] = x_ref[...]

        h = _rmsnorm(xs_ref[rows_p, :], gpre_ref[...]).astype(BF16)
        z_ref[...] = _dot(h, win_ref[...])

    def zc(off, width):
        return z_ref[rows_s, off:off + width]

    def per_seq(a):
        return a.reshape(group, t, a.shape[-1])

    u2d = zc(OFF_CC, CONV_DIM) * zc(OFF_CH, CONV_DIM)
    uo_ref[...] = u2d
    u = per_seq(u2d)
    prev = convp_ref[...]
    tpos = lax.broadcasted_iota(jnp.int32, (group, t, CONV_DIM), 1)
    u1 = jnp.where(tpos >= 1, pltpu.roll(u, 1, 1), pltpu.roll(prev, 1, 1))
    u2 = jnp.where(tpos >= 2, pltpu.roll(u, 2, 1), pltpu.roll(prev, 2, 1))
    cw = convw_ref[...]
    conv = cw[0:1, :] * u2 + cw[1:2, :] * u1 + cw[2:3, :] * u
    out_a = per_seq(_silu(zc(OFF_CG, CONV_DIM)) * zc(OFF_CB, CONV_DIM)) * conv

    kt_old = kt_ref[...]
    vt_old = vt_ref[...]
    k_new = zc(OFF_K, KV_DIM)
    v_new = zc(OFF_V, KV_DIM)
    k_new_t = jnp.swapaxes(per_seq(k_new), 1, 2)
    v_new_t = jnp.swapaxes(per_seq(v_new), 1, 2)
    kto_ref[...] = pltpu.roll(jnp.concatenate([k_new_t, kt_old[:, :, t:]], axis=2), WINDOW - t, 2)
    vto_ref[...] = pltpu.roll(jnp.concatenate([v_new_t, vt_old[:, :, t:]], axis=2), WINDOW - t, 2)

    q = per_seq(zc(OFF_Q, ATTN_DIM) * SCALE)
    low_q = _low_lanes((group, t, LANES))
    pieces = []
    for c in range(N_CHUNK):
        qc = q[:, :, c * LANES:(c + 1) * LANES]
        qs = _swap_halves(qc)
        if c // CHUNKS_PER_KV == 0:
            pieces += [jnp.where(low_q, qc, 0.0), jnp.where(low_q, qs, 0.0)]
        else:
            pieces += [jnp.where(low_q, 0.0, qs), jnp.where(low_q, 0.0, qc)]
    qbd = jnp.concatenate(pieces, axis=1).astype(BF16)
    nrow = ATTN_HEADS * t
    s_old = jnp.einsum('gqd,gdk->gqk', qbd, kt_old.astype(BF16), preferred_element_type=F32)
    s_new = _dot_nt(qbd.reshape(group * nrow, KV_DIM), k_new.astype(BF16)).reshape(group, nrow, group * t)
    tq = lax.broadcasted_iota(jnp.int32, (group, nrow, WINDOW), 1) % t
    col = lax.broadcasted_iota(jnp.int32, (group, nrow, WINDOW), 2)
    first_new = lax.broadcasted_iota(jnp.int32, (group, nrow, WINDOW), 0) * t
    s_old = jnp.where(col > tq, s_old, -jnp.inf)
    s_new = jnp.where((col >= first_new) & (col <= first_new + tq), s_new, -jnp.inf)
    sink = sink_ref[...]
    m = jnp.maximum(jnp.max(jnp.maximum(s_old, s_new), axis=2, keepdims=True), sink)
    p_old = jnp.exp(s_old - m)
    p_new = jnp.exp(s_new - m)
    d = jnp.sum(p_old + p_new, axis=2, keepdims=True) + jnp.exp(sink - m)
    o = jnp.einsum('gqk,gdk->gqd', p_old.astype(BF16), vt_old.astype(BF16), preferred_element_type=F32)
    o = o + _dot(p_new.astype(BF16).reshape(group * nrow, group * t), v_new.astype(BF16)).reshape(group, nrow, KV_DIM)
    o = o * (1.0 / d)
    ob_chunks = []
    for c in range(N_CHUNK):
        o_even = o[:, 2 * c * t:(2 * c + 1) * t]
        o_odd = o[:, (2 * c + 1) * t:(2 * c + 2) * t]
        if c // CHUNKS_PER_KV == 0:
            ob_chunks.append(jnp.where(low_q, o_even, _swap_halves(o_odd)))
        else:
            ob_chunks.append(jnp.where(low_q, _swap_halves(o_even), o_odd))
    o_b = jnp.concatenate(ob_chunks, axis=2)
    out_b = per_seq(_silu(zc(OFF_AG, ATTN_DIM))) * o_b

    mq = per_seq(zc(OFF_MQ, MEM_DIM) * SCALE)
    head_of_lane = lax.broadcasted_iota(jnp.int32, (group, t, MEM_DIM), 2) // HEAD_DIM
    qm = jnp.concatenate([jnp.where(head_of_lane == hh, mq, 0.0) for hh in range(MEM_HEADS)],
                         axis=1).astype(BF16)
    s = jnp.einsum('gqd,gdk->gqk', qm, mkt_ref[...].astype(BF16), preferred_element_type=F32)
    p = jnp.exp(s - jnp.max(s, axis=2, keepdims=True))
    d = jnp.sum(p, axis=2, keepdims=True)
    o = jnp.einsum('gqk,gdk->gqd', p.astype(BF16), mvt_ref[...].astype(BF16), preferred_element_type=F32)
    o = o * (1.0 / d)
    o_c = jnp.zeros((group, t, MEM_DIM), F32)
    for hh in range(MEM_HEADS):
        o_c = jnp.where(head_of_lane == hh, o[:, hh * t:(hh + 1) * t], o_c)
    out_c = per_seq(_silu(zc(OFF_MG, MEM_DIM))) * o_c

    mix_ref[rows_s, :] = jnp.concatenate([out_a, out_b, out_c], axis=2).reshape(nrows, MIX_DIM).astype(BF16)

    @pl.when(ss == n_sub - 1)
    def _():
        y = _dot(mix_ref[...], wout_ref[...])
        x_new = xs_ref[rows_p, :] + _rmsnorm(y, gpost_ref[...])
        xs_ref[rows_p, :] = x_new

        @pl.when(l == DEPTH - 1)
        def _():
            y_ref[...] = x_new


def _sample_stream(x2d, gpre3, gpost3, w_in_bf, w_out_bf, conv_w, sink_rows, convp, cache_kt, cache_vt,
                   cache_mkt, cache_mvt, n_seq, t, n_mem):
    group = SAMPLE_GROUP
    n_sub = SAMPLE_SUBSTEPS
    n_proj = n_seq // (group * n_sub)
    nrows = group * t
    prows = n_sub * nrows
    nrow_attn = ATTN_HEADS * t
    per_layer = lambda l, p, ss: (l, 0, 0)
    per_group = lambda l, p, ss: (l, p * n_sub + ss, 0, 0)
    return pl.pallas_call(
        functools.partial(_sample_kernel, group=group, t=t, n_mem=n_mem, n_sub=n_sub),
        grid=(DEPTH, n_proj, n_sub),
        in_specs=[
            pl.BlockSpec((prows, D_MODEL), lambda l, p, ss: (jnp.where(l == 0, p, n_proj - 1), 0)),
            pl.BlockSpec((None, 1, D_MODEL), per_layer),
            pl.BlockSpec((None, 1, D_MODEL), per_layer),
            pl.BlockSpec((None, D_MODEL, IN_DIM), per_layer),
            pl.BlockSpec((None, MIX_DIM, D_MODEL), per_layer),
            pl.BlockSpec((None, CONV_W, CONV_DIM), per_layer),
            pl.BlockSpec((None, nrow_attn, 1), per_layer),
            pl.BlockSpec((None, group, t, CONV_DIM), per_group),
            pl.BlockSpec((None, group, KV_DIM, WINDOW), per_group),
            pl.BlockSpec((None, group, KV_DIM, WINDOW), per_group),
            pl.BlockSpec((None, group, MEM_DIM, n_mem), per_group),
            pl.BlockSpec((None, group, MEM_DIM, n_mem), per_group),
        ],
        out_specs=[
            pl.BlockSpec((prows, D_MODEL), lambda l, p, ss: (jnp.where(l == DEPTH - 1, p, 0), 0)),
            pl.BlockSpec((None, group, KV_DIM, WINDOW), per_group),
            pl.BlockSpec((None, group, KV_DIM, WINDOW), per_group),
            pl.BlockSpec((None, nrows, CONV_DIM), lambda l, p, ss: (l, p * n_sub + ss, 0)),
        ],
        out_shape=[
            jax.ShapeDtypeStruct((n_seq * t, D_MODEL), F32),
            jax.ShapeDtypeStruct((DEPTH, n_seq, KV_DIM, WINDOW), F32),
            jax.ShapeDtypeStruct((DEPTH, n_seq, KV_DIM, WINDOW), F32),
            jax.ShapeDtypeStruct((DEPTH, n_seq * t, CONV_DIM), F32),
        ],
        scratch_shapes=[
            pltpu.VMEM((n_seq * t, D_MODEL), F32),
            pltpu.VMEM((prows, IN_DIM), F32),
            pltpu.VMEM((prows, MIX_DIM), BF16),
        ],
        compiler_params=pltpu.CompilerParams(
            dimension_semantics=("arbitrary", "arbitrary", "arbitrary"), vmem_limit_bytes=VMEM_LIMIT),
        name="sample_stream",
    )(x2d, gpre3, gpost3, w_in_bf, w_out_bf, conv_w, sink_rows, convp, cache_kt, cache_vt, cache_mkt, cache_mvt)


def _keys_minor(a):
    lead = a.shape[:-3]
    n, heads, hd = a.shape[-3:]
    nd = a.ndim
    perm = tuple(range(nd - 3)) + (nd - 2, nd - 1, nd - 3)
    return jnp.transpose(a, perm).reshape(*lead, heads * hd, n)


def _keys_major(a, heads):
    lead = a.shape[:-2]
    n = a.shape[-1]
    a = a.reshape(*lead, heads, HEAD_DIM, n)
    nd = a.ndim
    perm = tuple(range(nd - 3)) + (nd - 1, nd - 3, nd - 2)
    return jnp.transpose(a, perm)


def kernel(x_prompt, x_sample, mem_prompt, cache_win_k, cache_win_v, state_conv, cache_mem_k, cache_mem_v,
           norm_pre, norm_post, norm_mem, w_in, conv_w, attn_sinks, w_mem_kv, w_out):
    batch, seq, _ = x_prompt.shape
    n_seq, t, _ = x_sample.shape
    n_mem = mem_prompt.shape[1]
    assert seq % PROMPT_TILE == 0 and PROMPT_TILE % (PROMPT_PARTS * 2 * WINDOW) == 0
    assert n_seq % (SAMPLE_GROUP * SAMPLE_SUBSTEPS) == 0 and t == SUBLANES

    w_in_bf = w_in.astype(BF16)
    w_out_bf = w_out.astype(BF16)
    w_mem_bf = w_mem_kv.astype(BF16)
    sink_rows = jnp.repeat(attn_sinks, t, axis=1)[:, :, None]
    gpre3 = norm_pre[:, None, :]
    gpost3 = norm_post[:, None, :]
    gmem3 = norm_mem[:, None, :]

    mkt, mvt, kcat_t, vcat = _memkv(mem_prompt.reshape(batch * n_mem, D_MODEL), gmem3, w_mem_bf, batch, n_mem)

    convp = jnp.pad(state_conv, ((0, 0), (0, 0), (t - CONV_BUF, 0), (0, 0)))
    ys, kts, vts, us = _sample_stream(
        x_sample.reshape(n_seq * t, D_MODEL), gpre3, gpost3, w_in_bf, w_out_bf, conv_w, sink_rows, convp,
        _keys_minor(cache_win_k), _keys_minor(cache_win_v), _keys_minor(cache_mem_k), _keys_minor(cache_mem_v),
        n_seq, t, n_mem)

    xp = x_prompt
    ktp, vtp, cvp = [], [], []
    for l in range(DEPTH):
        xp, k_p, v_p, c_p = _prompt_layer(l, xp, attn_sinks, gpre3, gpost3, w_in_bf, conv_w, kcat_t, vcat,
                                          w_out_bf, n_mem)
        ktp.append(k_p)
        vtp.append(v_p)
        cvp.append(c_p[:, SUBLANES - CONV_BUF:, :])

    return (xp,
            ys.reshape(n_seq, t, D_MODEL),
            _keys_major(jnp.stack(ktp), KV_HEADS),
            _keys_major(jnp.stack(vtp), KV_HEADS),
            jnp.stack(cvp),
            _keys_major(mkt, MEM_HEADS),
            _keys_major(mvt, MEM_HEADS),
            _keys_major(kts, KV_HEADS),
            _keys_major(vts, KV_HEADS),
            us.reshape(DEPTH, n_seq, t, CONV_DIM)[:, :, t - CONV_BUF:, :])
```

```python
import functools

import jax
import jax.numpy as jnp
from jax import lax
from jax.experimental import pallas as pl
from jax.experimental.pallas import tpu as pltpu

D_MODEL = 1024
DEPTH = 4
HEAD_DIM = 64
ATTN_HEADS = 8
KV_HEADS = 2
ATTN_DIM = ATTN_HEADS * HEAD_DIM
KV_DIM = KV_HEADS * HEAD_DIM
WINDOW = 128
MEM_HEADS = 4
MEM_DIM = MEM_HEADS * HEAD_DIM
CONV_DIM = 256
CONV_W = 3
CONV_BUF = CONV_W - 1
MIX_DIM = CONV_DIM + ATTN_DIM + MEM_DIM
IN_DIM = 4 * CONV_DIM + 2 * ATTN_DIM + 2 * KV_DIM + 2 * MEM_DIM
RMS_EPS = 1e-6
SCALE = HEAD_DIM ** -0.5
LOG2E = 1.4426950408889634

LANES = 128
SUBLANES = 8
VMEM_LIMIT = 56 * 1024 * 1024

OFF_CB, OFF_CC, OFF_CH, OFF_CG = 0, 256, 512, 768
OFF_Q, OFF_K, OFF_V, OFF_AG = 1024, 1536, 1664, 1792
OFF_MQ, OFF_MG = 2304, 2560

N_CHUNK = ATTN_DIM // LANES
M_CHUNK = MEM_DIM // LANES
CHUNKS_PER_KV = N_CHUNK // KV_HEADS

PROMPT_TILE = 1024
PROMPT_PARTS = 2
SAMPLE_GROUP = 16
SAMPLE_SUBSTEPS = 2

F32 = jnp.float32
BF16 = jnp.bfloat16


def _rmsnorm(x, g):
    r = lax.rsqrt(jnp.mean(x * x, axis=-1, keepdims=True) + RMS_EPS)
    return (x * r) * g


def _silu(x):
    return x * jax.nn.sigmoid(x)


def _dot(a, b):
    return jnp.dot(a, b, preferred_element_type=F32)


def _dot_nt(a, b):
    return lax.dot_general(a, b, (((1,), (1,)), ((), ())), preferred_element_type=F32)


def _low_lanes(shape):
    return lax.broadcasted_iota(jnp.int32, shape, len(shape) - 1) < HEAD_DIM


def _swap_halves(a):
    return pltpu.roll(a, HEAD_DIM, a.ndim - 1)


def _memkv_kernel(mem_ref, g_ref, w_ref, mkt_ref, mvt_ref, kcat_t_ref, vcat_ref, *, batch, n_mem):
    h = _rmsnorm(mem_ref[...], g_ref[pl.ds(pl.program_id(0), 1), :]).astype(BF16)
    kv = _dot(h, w_ref[...].astype(BF16))
    mk = kv[:, :MEM_DIM]
    mv = kv[:, MEM_DIM:]
    low = _low_lanes((n_mem, LANES))
    for b in range(batch):
        mkt_ref[b] = mk[b * n_mem:(b + 1) * n_mem, :].T
        mvt_ref[b] = mv[b * n_mem:(b + 1) * n_mem, :].T
        for c in range(M_CHUNK):
            kc = mk[b * n_mem:(b + 1) * n_mem, c * LANES:(c + 1) * LANES]
            vc = mv[b * n_mem:(b + 1) * n_mem, c * LANES:(c + 1) * LANES]
            kcat = jnp.concatenate([jnp.where(low, kc, 0.0), jnp.where(low, 0.0, kc)], axis=0)
            vcat = jnp.concatenate([jnp.where(low, vc, 0.0), jnp.where(low, 0.0, vc)], axis=0)
            kcat_t_ref[b, c] = kcat.T.astype(BF16)
            vcat_ref[b, c] = vcat.astype(BF16)


def _memkv(mem2d, norm_mem, w_mem_kv, batch, n_mem):
    rows = batch * n_mem
    return pl.pallas_call(
        functools.partial(_memkv_kernel, batch=batch, n_mem=n_mem),
        grid=(DEPTH,),
        in_specs=[
            pl.BlockSpec((rows, D_MODEL), lambda l: (0, 0)),
            pl.BlockSpec((DEPTH, D_MODEL), lambda l: (0, 0)),
            pl.BlockSpec((None, D_MODEL, 2 * MEM_DIM), lambda l: (l, 0, 0)),
        ],
        out_specs=[
            pl.BlockSpec((None, batch, MEM_DIM, n_mem), lambda l: (l, 0, 0, 0)),
            pl.BlockSpec((None, batch, MEM_DIM, n_mem), lambda l: (l, 0, 0, 0)),
            pl.BlockSpec((None, batch, M_CHUNK, LANES, 2 * n_mem), lambda l: (l, 0, 0, 0, 0)),
            pl.BlockSpec((None, batch, M_CHUNK, 2 * n_mem, LANES), lambda l: (l, 0, 0, 0, 0)),
        ],
        out_shape=[
            jax.ShapeDtypeStruct((DEPTH, batch, MEM_DIM, n_mem), F32),
            jax.ShapeDtypeStruct((DEPTH, batch, MEM_DIM, n_mem), F32),
            jax.ShapeDtypeStruct((DEPTH, batch, M_CHUNK, LANES, 2 * n_mem), BF16),
            jax.ShapeDtypeStruct((DEPTH, batch, M_CHUNK, 2 * n_mem, LANES), BF16),
        ],
        compiler_params=pltpu.CompilerParams(dimension_semantics=("arbitrary",)),
        name="memkv",
    )(mem2d, norm_mem, w_mem_kv)


def _prompt_layer_kernel(sink_ref, x_ref, gpre_ref, gpost_ref, win_ref, convw_ref, kcat_t_ref,
                         vcat_ref, wout_ref,
                         xo_ref, klast_ref, vlast_ref, convlast_ref,
                         kprev_ref, vprev_ref, ubuf_ref, *, layer, tile, n_mem, n_tiles):
    i = pl.program_id(1)
    part = tile // PROMPT_PARTS
    n_blocks = part // WINDOW
    g_pre = gpre_ref[layer:layer + 1, :]
    g_post = gpost_ref[layer:layer + 1, :]

    @pl.when(i == 0)
    def _():
        kprev_ref[...] = jnp.zeros_like(kprev_ref)
        vprev_ref[...] = jnp.zeros_like(vprev_ref)
        ubuf_ref[0:SUBLANES, :] = jnp.zeros((SUBLANES, CONV_DIM), F32)

    rows = N_CHUNK * WINDOW
    qpos = lax.broadcasted_iota(jnp.int32, (rows, 2 * WINDOW), 0) % WINDOW
    kpos = lax.broadcasted_iota(jnp.int32, (rows, 2 * WINDOW), 1)
    band = (kpos > qpos) & (kpos <= qpos + WINDOW)
    band_first = band & ((kpos >= WINDOW) | (i > 0))
    chunk_of_row = lax.broadcasted_iota(jnp.int32, (rows, 1), 0) // WINDOW
    sink_lo = jnp.zeros((rows, 1), F32)
    sink_hi = jnp.zeros((rows, 1), F32)
    for p in range(N_CHUNK):
        sink_lo = jnp.where(chunk_of_row == p, sink_ref[layer, p] * LOG2E, sink_lo)
        sink_hi = jnp.where(chunk_of_row == p, sink_ref[layer, N_CHUNK + p] * LOG2E, sink_hi)
    low_o = _low_lanes((rows, LANES))
    low_w = _low_lanes((WINDOW, LANES))
    low_t = _low_lanes((part, LANES))
    low_kv = _low_lanes((WINDOW + part, KV_DIM))

    def start(pi):
        x = x_ref[pi * part:(pi + 1) * part, :]
        h = _rmsnorm(x, g_pre).astype(BF16)
        proj = lambda off, width: _dot(h, win_ref[:, off:off + width])
        return dict(x=x, proj=proj, q_raw=proj(OFF_Q, ATTN_DIM), kv=proj(OFF_K, 2 * KV_DIM),
                    mq_raw=proj(OFF_MQ, MEM_DIM))

    def mix_part(pi, st, k_prev, v_prev, fill_one):
        k = st["kv"][:, :KV_DIM]
        v = st["kv"][:, KV_DIM:]
        kfull = jnp.concatenate([k_prev, k], axis=0)
        vfull = jnp.concatenate([v_prev, v], axis=0)
        klo = jnp.where(low_kv, kfull, 0.0).astype(BF16)
        khi = jnp.where(low_kv, 0.0, kfull).astype(BF16)
        vlo = jnp.where(low_kv, vfull, 0.0).astype(BF16)
        vhi = jnp.where(low_kv, 0.0, vfull).astype(BF16)

        q = st["q_raw"] * (SCALE * LOG2E)
        nat = [q[:, c * LANES:(c + 1) * LANES] for c in range(N_CHUNK)]
        qp = []
        for p in range(N_CHUNK):
            a, b = nat[p // 2], nat[CHUNKS_PER_KV + p // 2]
            pair = jnp.where(low_t, a, _swap_halves(b)) if p % 2 == 0 else jnp.where(low_t, _swap_halves(a), b)
            qp.append(pair.astype(BF16))

        def swa_scores(j):
            r0 = j * WINDOW
            q_all = jnp.concatenate([qp[p][r0:r0 + WINDOW] for p in range(N_CHUNK)], axis=0)
            k_cat = jnp.concatenate([klo[r0:r0 + 2 * WINDOW], khi[r0:r0 + 2 * WINDOW]], axis=0)
            return _dot_nt(q_all, k_cat)

        o_blocks = []
        fill_one()
        s = swa_scores(0)
        for j in range(n_blocks):
            r0 = j * WINDOW
            fill_one()
            s_next = swa_scores(j + 1) if j + 1 < n_blocks else None
            v_cat = jnp.concatenate([vlo[r0:r0 + 2 * WINDOW], vhi[r0:r0 + 2 * WINDOW]], axis=0)
            mask = band_first if (pi == 0 and j == 0) else band
            s_lo = jnp.where(mask, s[:, :2 * WINDOW], -jnp.inf)
            s_hi = jnp.where(mask, s[:, 2 * WINDOW:], -jnp.inf)
            m_lo = jnp.maximum(jnp.max(s_lo, axis=1, keepdims=True), sink_lo)
            m_hi = jnp.maximum(jnp.max(s_hi, axis=1, keepdims=True), sink_hi)
            p_lo = jnp.exp2(s_lo - m_lo)
            p_hi = jnp.exp2(s_hi - m_hi)
            d_lo = jnp.sum(p_lo, axis=1, keepdims=True) + jnp.exp2(sink_lo - m_lo)
            d_hi = jnp.sum(p_hi, axis=1, keepdims=True) + jnp.exp2(sink_hi - m_hi)
            pr = jnp.concatenate([p_lo, p_hi], axis=1).astype(BF16)
            o = _dot(pr, v_cat)
            o = o * jnp.where(low_o, 1.0 / d_lo, 1.0 / d_hi)
            op = [o[p * WINDOW:(p + 1) * WINDOW] for p in range(N_CHUNK)]
            o_blocks.append(jnp.concatenate(
                [jnp.where(low_w, op[0], _swap_halves(op[1])), jnp.where(low_w, op[2], _swap_halves(op[3])),
                 jnp.where(low_w, _swap_halves(op[0]), op[1]), jnp.where(low_w, _swap_halves(op[2]), op[3])],
                axis=1))
            s = s_next
        o_b = jnp.concatenate(o_blocks, axis=0) if len(o_blocks) > 1 else o_blocks[0]

        mq = (st["mq_raw"] * (SCALE * LOG2E)).astype(BF16)
        oc_chunks = []
        for c in range(M_CHUNK):
            s = _dot(mq[:, c * LANES:(c + 1) * LANES], kcat_t_ref[c])
            fill_one()
            s0 = s[:, :n_mem]
            s1 = s[:, n_mem:]
            p0 = jnp.exp2(s0 - jnp.max(s0, axis=1, keepdims=True))
            p1 = jnp.exp2(s1 - jnp.max(s1, axis=1, keepdims=True))
            d0 = jnp.sum(p0, axis=1, keepdims=True)
            d1 = jnp.sum(p1, axis=1, keepdims=True)
            pm = jnp.concatenate([p0, p1], axis=1).astype(BF16)
            o = _dot(pm, vcat_ref[c])
            oc_chunks.append(o * jnp.where(low_t, 1.0 / d0, 1.0 / d1))
        o_c = jnp.concatenate(oc_chunks, axis=1)
        return o_b, o_c, k[part - WINDOW:part, :], v[part - WINDOW:part, :]

    def gate_mix(pi, filled, o_b, o_c):
        out_b = _silu(jnp.concatenate([filled["ag0"], filled["ag1"]], axis=1)) * o_b
        out_c = _silu(filled["mg"]) * o_c
        u = filled["cc"] * filled["ch"]
        base = SUBLANES + pi * part
        ubuf_ref[base:base + part, :] = u
        u1 = ubuf_ref[base - 1:base - 1 + part, :]
        u2 = ubuf_ref[base - 2:base - 2 + part, :]
        cw = convw_ref[...]
        conv = cw[0:1, :] * u2 + cw[1:2, :] * u1 + cw[2:3, :] * u
        out_a = _silu(filled["cg"]) * filled["cb"] * conv
        return jnp.concatenate([out_a, out_b, out_c], axis=1).astype(BF16), u[part - SUBLANES:part, :]

    def finish(pi, st, mix):
        half = part // 2
        y0 = _dot(mix[:half], wout_ref[...])
        y1 = _dot(mix[half:], wout_ref[...])
        r0 = pi * part
        xo_ref[r0:r0 + half, :] = st["x"][:half] + _rmsnorm(y0, g_post)
        xo_ref[r0 + half:r0 + part, :] = st["x"][half:] + _rmsnorm(y1, g_post)

    filler_cols = [("cg", OFF_CG), ("cb", OFF_CB), ("cc", OFF_CC), ("ch", OFF_CH),
                   ("ag0", OFF_AG), ("ag1", OFF_AG + 2 * LANES), ("mg", OFF_MG)]

    states = [start(0)]
    k_prev, v_prev = kprev_ref[...], vprev_ref[...]
    u_tail = None
    for pi in range(PROMPT_PARTS):
        st = states[pi]
        pending = list(filler_cols)
        filled = {}

        def fill_one():
            if pending:
                name, off = pending.pop(0)
                filled[name] = st["proj"](off, 2 * LANES)

        if pi + 1 < PROMPT_PARTS:
            states.append(start(pi + 1))
        o_b, o_c, k_prev, v_prev = mix_part(pi, st, k_prev, v_prev, fill_one)
        while pending:
            fill_one()
        mix, u_tail = gate_mix(pi, filled, o_b, o_c)
        finish(pi, st, mix)

    kprev_ref[...] = k_prev
    vprev_ref[...] = v_prev
    ubuf_ref[0:SUBLANES, :] = u_tail
    convlast_ref[...] = u_tail

    @pl.when(i == n_tiles - 1)
    def _():
        klast_ref[...] = kprev_ref[...].T
        vlast_ref[...] = vprev_ref[...].T


def _prompt_layer(l, x, sinks, norm_pre, norm_post, w_in_bf, conv_w, kcat_t, vcat, w_out_bf, n_mem):
    batch, seq, _ = x.shape
    tile = PROMPT_TILE
    nt = seq // tile
    return pl.pallas_call(
        functools.partial(_prompt_layer_kernel, layer=l, tile=tile, n_mem=n_mem, n_tiles=nt),
        grid=(batch, nt),
        in_specs=[
            pl.BlockSpec(memory_space=pltpu.SMEM),
            pl.BlockSpec((None, tile, D_MODEL), lambda b, i: (b, i, 0)),
            pl.BlockSpec((DEPTH, D_MODEL), lambda b, i: (0, 0)),
            pl.BlockSpec((DEPTH, D_MODEL), lambda b, i: (0, 0)),
            pl.BlockSpec((None, D_MODEL, IN_DIM), lambda b, i: (l, 0, 0)),
            pl.BlockSpec((None, CONV_W, CONV_DIM), lambda b, i: (l, 0, 0)),
            pl.BlockSpec((None, None, M_CHUNK, LANES, 2 * n_mem), lambda b, i: (l, b, 0, 0, 0)),
            pl.BlockSpec((None, None, M_CHUNK, 2 * n_mem, LANES), lambda b, i: (l, b, 0, 0, 0)),
            pl.BlockSpec((None, MIX_DIM, D_MODEL), lambda b, i: (l, 0, 0)),
        ],
        out_specs=[
            pl.BlockSpec((None, tile, D_MODEL), lambda b, i: (b, i, 0)),
            pl.BlockSpec((None, KV_DIM, WINDOW), lambda b, i: (b, 0, 0)),
            pl.BlockSpec((None, KV_DIM, WINDOW), lambda b, i: (b, 0, 0)),
            pl.BlockSpec((None, SUBLANES, CONV_DIM), lambda b, i: (b, 0, 0)),
        ],
        out_shape=[
            jax.ShapeDtypeStruct((batch, seq, D_MODEL), F32),
            jax.ShapeDtypeStruct((batch, KV_DIM, WINDOW), F32),
            jax.ShapeDtypeStruct((batch, KV_DIM, WINDOW), F32),
            jax.ShapeDtypeStruct((batch, SUBLANES, CONV_DIM), F32),
        ],
        scratch_shapes=[
            pltpu.VMEM((WINDOW, KV_DIM), F32),
            pltpu.VMEM((WINDOW, KV_DIM), F32),
            pltpu.VMEM((SUBLANES + tile, CONV_DIM), F32),
        ],
        compiler_params=pltpu.CompilerParams(
            dimension_semantics=("arbitrary", "arbitrary"), vmem_limit_bytes=VMEM_LIMIT),
        name="prompt_layer",
    )(sinks, x, norm_pre, norm_post, w_in_bf, conv_w, kcat_t, vcat, w_out_bf)


def _sample_kernel(x_ref, gpre_ref, gpost_ref, win_ref, wout_ref, convw_ref, sink_ref, convp_ref,
                   kt_ref, vt_ref, mkt_ref, mvt_ref,
                   y_ref, kto_ref, vto_ref, uo_ref,
                   xs_ref, z_ref, mix_ref, *, group, t, n_mem, n_sub):
    l = pl.program_id(0)
    p = pl.program_id(1)
    ss = pl.program_id(2)
    nrows = group * t
    prows = n_sub * nrows
    rows_p = pl.ds(pl.multiple_of(p * prows, prows), prows)
    rows_s = pl.ds(pl.multiple_of(ss * nrows, nrows), nrows)

    @pl.when(ss == 0)
    def _():
        @pl.when(l == 0)
        def _():
            xs_ref[rows_p, :] = x_ref[...]

        h = _rmsnorm(xs_ref[rows_p, :], gpre_ref[pl.ds(l, 1), :]).astype(BF16)
        z_ref[...] = _dot(h, win_ref[...])

    def zc(off, width):
        return z_ref[rows_s, off:off + width]

    def per_seq(a):
        return a.reshape(group, t, a.shape[-1])

    u2d = zc(OFF_CC, CONV_DIM) * zc(OFF_CH, CONV_DIM)
    uo_ref[...] = u2d
    u = per_seq(u2d)
    prev = convp_ref[...]
    tpos = lax.broadcasted_iota(jnp.int32, (group, t, CONV_DIM), 1)
    u1 = jnp.where(tpos >= 1, pltpu.roll(u, 1, 1), pltpu.roll(prev, 1, 1))
    u2 = jnp.where(tpos >= 2, pltpu.roll(u, 2, 1), pltpu.roll(prev, 2, 1))
    cw = convw_ref[...]
    conv = cw[0:1, :] * u2 + cw[1:2, :] * u1 + cw[2:3, :] * u
    out_a = per_seq(_silu(zc(OFF_CG, CONV_DIM)) * zc(OFF_CB, CONV_DIM)) * conv

    kt_old = kt_ref[...]
    vt_old = vt_ref[...]
    k_new = zc(OFF_K, KV_DIM)
    v_new = zc(OFF_V, KV_DIM)
    k_new_t = jnp.swapaxes(per_seq(k_new), 1, 2)
    v_new_t = jnp.swapaxes(per_seq(v_new), 1, 2)
    kto_ref[...] = pltpu.roll(jnp.concatenate([k_new_t, kt_old[:, :, t:]], axis=2), WINDOW - t, 2)
    vto_ref[...] = pltpu.roll(jnp.concatenate([v_new_t, vt_old[:, :, t:]], axis=2), WINDOW - t, 2)

    q = per_seq(zc(OFF_Q, ATTN_DIM) * SCALE)
    low_q = _low_lanes((group, t, LANES))
    pieces = []
    for c in range(N_CHUNK):
        qc = q[:, :, c * LANES:(c + 1) * LANES]
        qs = _swap_halves(qc)
        if c // CHUNKS_PER_KV == 0:
            pieces += [jnp.where(low_q, qc, 0.0), jnp.where(low_q, qs, 0.0)]
        else:
            pieces += [jnp.where(low_q, 0.0, qs), jnp.where(low_q, 0.0, qc)]
    qbd = jnp.concatenate(pieces, axis=1).astype(BF16)
    nrow = ATTN_HEADS * t
    s_old = jnp.einsum('gqd,gdk->gqk', qbd, kt_old.astype(BF16), preferred_element_type=F32)
    s_new = _dot_nt(qbd.reshape(group * nrow, KV_DIM), k_new.astype(BF16)).reshape(group, nrow, group * t)
    tq = lax.broadcasted_iota(jnp.int32, (group, nrow, WINDOW), 1) % t
    col = lax.broadcasted_iota(jnp.int32, (group, nrow, WINDOW), 2)
    first_new = lax.broadcasted_iota(jnp.int32, (group, nrow, WINDOW), 0) * t
    s_old = jnp.where(col > tq, s_old, -jnp.inf)
    s_new = jnp.where((col >= first_new) & (col <= first_new + tq), s_new, -jnp.inf)
    sink = sink_ref[...]
    m = jnp.maximum(jnp.max(jnp.maximum(s_old, s_new), axis=2, keepdims=True), sink)
    p_old = jnp.exp(s_old - m)
    p_new = jnp.exp(s_new - m)
    d = jnp.sum(p_old + p_new, axis=2, keepdims=True) + jnp.exp(sink - m)
    o = jnp.einsum('gqk,gdk->gqd', p_old.astype(BF16), vt_old.astype(BF16), preferred_element_type=F32)
    o = o + _dot(p_new.astype(BF16).reshape(group * nrow, group * t), v_new.astype(BF16)).reshape(group, nrow, KV_DIM)
    o = o * (1.0 / d)
    ob_chunks = []
    for c in range(N_CHUNK):
        o_even = o[:, 2 * c * t:(2 * c + 1) * t]
        o_odd = o[:, (2 * c + 1) * t:(2 * c + 2) * t]
        if c // CHUNKS_PER_KV == 0:
            ob_chunks.append(jnp.where(low_q, o_even, _swap_halves(o_odd)))
        else:
            ob_chunks.append(jnp.where(low_q, _swap_halves(o_even), o_odd))
    o_b = jnp.concatenate(ob_chunks, axis=2)
    out_b = per_seq(_silu(zc(OFF_AG, ATTN_DIM))) * o_b

    mq = per_seq(zc(OFF_MQ, MEM_DIM) * SCALE)
    head_of_lane = lax.broadcasted_iota(jnp.int32, (group, t, MEM_DIM), 2) // HEAD_DIM
    qm = jnp.concatenate([jnp.where(head_of_lane == hh, mq, 0.0) for hh in range(MEM_HEADS)],
                         axis=1).astype(BF16)
    s = jnp.einsum('gqd,gdk->gqk', qm, mkt_ref[...].astype(BF16), preferred_element_type=F32)
    p = jnp.exp(s - jnp.max(s, axis=2, keepdims=True))
    d = jnp.sum(p, axis=2, keepdims=True)
    o = jnp.einsum('gqk,gdk->gqd', p.astype(BF16), mvt_ref[...].astype(BF16), preferred_element_type=F32)
    o = o * (1.0 / d)
    o_c = jnp.zeros((group, t, MEM_DIM), F32)
    for hh in range(MEM_HEADS):
        o_c = jnp.where(head_of_lane == hh, o[:, hh * t:(hh + 1) * t], o_c)
    out_c = per_seq(_silu(zc(OFF_MG, MEM_DIM))) * o_c

    mix_ref[rows_s, :] = jnp.concatenate([out_a, out_b, out_c], axis=2).reshape(nrows, MIX_DIM).astype(BF16)

    @pl.when(ss == n_sub - 1)
    def _():
        y = _dot(mix_ref[...], wout_ref[...])
        x_new = xs_ref[rows_p, :] + _rmsnorm(y, gpost_ref[pl.ds(l, 1), :])
        xs_ref[rows_p, :] = x_new

        @pl.when(l == DEPTH - 1)
        def _():
            y_ref[...] = x_new


def _sample_stream(x2d, norm_pre, norm_post, w_in_bf, w_out_bf, conv_w, sink_rows, convp, cache_kt, cache_vt,
                   cache_mkt, cache_mvt, n_seq, t, n_mem):
    group = SAMPLE_GROUP
    n_sub = SAMPLE_SUBSTEPS
    n_proj = n_seq // (group * n_sub)
    nrows = group * t
    prows = n_sub * nrows
    nrow_attn = ATTN_HEADS * t
    per_layer = lambda l, p, ss: (l, 0, 0)
    per_group = lambda l, p, ss: (l, p * n_sub + ss, 0, 0)
    return pl.pallas_call(
        functools.partial(_sample_kernel, group=group, t=t, n_mem=n_mem, n_sub=n_sub),
        grid=(DEPTH, n_proj, n_sub),
        in_specs=[
            pl.BlockSpec((prows, D_MODEL), lambda l, p, ss: (jnp.where(l == 0, p, n_proj - 1), 0)),
            pl.BlockSpec((DEPTH, D_MODEL), lambda l, p, ss: (0, 0)),
            pl.BlockSpec((DEPTH, D_MODEL), lambda l, p, ss: (0, 0)),
            pl.BlockSpec((None, D_MODEL, IN_DIM), per_layer),
            pl.BlockSpec((None, MIX_DIM, D_MODEL), per_layer),
            pl.BlockSpec((None, CONV_W, CONV_DIM), per_layer),
            pl.BlockSpec((None, nrow_attn, 1), per_layer),
            pl.BlockSpec((None, group, t, CONV_DIM), per_group),
            pl.BlockSpec((None, group, KV_DIM, WINDOW), per_group),
            pl.BlockSpec((None, group, KV_DIM, WINDOW), per_group),
            pl.BlockSpec((None, group, MEM_DIM, n_mem), per_group),
            pl.BlockSpec((None, group, MEM_DIM, n_mem), per_group),
        ],
        out_specs=[
            pl.BlockSpec((prows, D_MODEL), lambda l, p, ss: (jnp.where(l == DEPTH - 1, p, 0), 0)),
            pl.BlockSpec((None, group, KV_DIM, WINDOW), per_group),
            pl.BlockSpec((None, group, KV_DIM, WINDOW), per_group),
            pl.BlockSpec((None, nrows, CONV_DIM), lambda l, p, ss: (l, p * n_sub + ss, 0)),
        ],
        out_shape=[
            jax.ShapeDtypeStruct((n_seq * t, D_MODEL), F32),
            jax.ShapeDtypeStruct((DEPTH, n_seq, KV_DIM, WINDOW), F32),
            jax.ShapeDtypeStruct((DEPTH, n_seq, KV_DIM, WINDOW), F32),
            jax.ShapeDtypeStruct((DEPTH, n_seq * t, CONV_DIM), F32),
        ],
        scratch_shapes=[
            pltpu.VMEM((n_seq * t, D_MODEL), F32),
            pltpu.VMEM((prows, IN_DIM), F32),
            pltpu.VMEM((prows, MIX_DIM), BF16),
        ],
        compiler_params=pltpu.CompilerParams(
            dimension_semantics=("arbitrary", "arbitrary", "arbitrary"), vmem_limit_bytes=VMEM_LIMIT),
        name="sample_stream",
    )(x2d, norm_pre, norm_post, w_in_bf, w_out_bf, conv_w, sink_rows, convp, cache_kt, cache_vt, cache_mkt, cache_mvt)


def _keys_minor(a):
    lead = a.shape[:-3]
    n, heads, hd = a.shape[-3:]
    nd = a.ndim
    perm = tuple(range(nd - 3)) + (nd - 2, nd - 1, nd - 3)
    return jnp.transpose(a, perm).reshape(*lead, heads * hd, n)


def _keys_major(a, heads):
    lead = a.shape[:-2]
    n = a.shape[-1]
    a = a.reshape(*lead, heads, HEAD_DIM, n)
    nd = a.ndim
    perm = tuple(range(nd - 3)) + (nd - 1, nd - 3, nd - 2)
    return jnp.transpose(a, perm)


def kernel(x_prompt, x_sample, mem_prompt, cache_win_k, cache_win_v, state_conv, cache_mem_k, cache_mem_v,
           norm_pre, norm_post, norm_mem, w_in, conv_w, attn_sinks, w_mem_kv, w_out):
    batch, seq, _ = x_prompt.shape
    n_seq, t, _ = x_sample.shape
    n_mem = mem_prompt.shape[1]
    assert seq % PROMPT_TILE == 0 and PROMPT_TILE % (PROMPT_PARTS * 2 * WINDOW) == 0
    assert n_seq % (SAMPLE_GROUP * SAMPLE_SUBSTEPS) == 0 and t == SUBLANES

    w_in_bf = w_in.astype(BF16)
    w_out_bf = w_out.astype(BF16)
    sink_rows = jnp.repeat(attn_sinks, t, axis=1)[:, :, None]

    mkt, mvt, kcat_t, vcat = _memkv(mem_prompt.reshape(batch * n_mem, D_MODEL), norm_mem, w_mem_kv, batch, n_mem)

    convp = jnp.pad(state_conv, ((0, 0), (0, 0), (t - CONV_BUF, 0), (0, 0)))
    ys, kts, vts, us = _sample_stream(
        x_sample.reshape(n_seq * t, D_MODEL), norm_pre, norm_post, w_in_bf, w_out_bf, conv_w, sink_rows, convp,
        _keys_minor(cache_win_k), _keys_minor(cache_win_v), _keys_minor(cache_mem_k), _keys_minor(cache_mem_v),
        n_seq, t, n_mem)

    xp = x_prompt
    ktp, vtp, cvp = [], [], []
    for l in range(DEPTH):
        xp, k_p, v_p, c_p = _prompt_layer(l, xp, attn_sinks, norm_pre, norm_post, w_in_bf, conv_w, kcat_t, vcat,
                                          w_out_bf, n_mem)
        ktp.append(k_p)
        vtp.append(v_p)
        cvp.append(c_p[:, SUBLANES - CONV_BUF:, :])

    return (xp,
            ys.reshape(n_seq, t, D_MODEL),
            _keys_major(jnp.stack(ktp), KV_HEADS),
            _keys_major(jnp.stack(vtp), KV_HEADS),
            jnp.stack(cvp),
            _keys_major(mkt, MEM_HEADS),
            _keys_major(mvt, MEM_HEADS),
            _keys_major(kts, KV_HEADS),
            _keys_major(vts, KV_HEADS),
            us.reshape(DEPTH, n_seq, t, CONV_DIM)[:, :, t - CONV_BUF:, :])
```

```python
import functools

import jax
import jax.numpy as jnp
from jax import lax
from jax.experimental import pallas as pl
from jax.experimental.pallas import tpu as pltpu

D_MODEL = 1024
DEPTH = 4
HEAD_DIM = 64
ATTN_HEADS = 8
KV_HEADS = 2
ATTN_DIM = ATTN_HEADS * HEAD_DIM
KV_DIM = KV_HEADS * HEAD_DIM
WINDOW = 128
MEM_HEADS = 4
MEM_DIM = MEM_HEADS * HEAD_DIM
CONV_DIM = 256
CONV_W = 3
CONV_BUF = CONV_W - 1
MIX_DIM = CONV_DIM + ATTN_DIM + MEM_DIM
IN_DIM = 4 * CONV_DIM + 2 * ATTN_DIM + 2 * KV_DIM + 2 * MEM_DIM
RMS_EPS = 1e-6
SCALE = HEAD_DIM ** -0.5
LOG2E = 1.4426950408889634

LANES = 128
SUBLANES = 8
VMEM_LIMIT = 56 * 1024 * 1024

OFF_CB, OFF_CC, OFF_CH, OFF_CG = 0, 256, 512, 768
OFF_Q, OFF_K, OFF_V, OFF_AG = 1024, 1536, 1664, 1792
OFF_MQ, OFF_MG = 2304, 2560

N_CHUNK = ATTN_DIM // LANES
M_CHUNK = MEM_DIM // LANES
CHUNKS_PER_KV = N_CHUNK // KV_HEADS

PROMPT_TILE = 1024
PROMPT_PART_ROWS = (512, 512)
SAMPLE_GROUP = 16
SAMPLE_SUBSTEPS = 2

F32 = jnp.float32
BF16 = jnp.bfloat16


def _rmsnorm(x, g):
    r = lax.rsqrt(jnp.mean(x * x, axis=-1, keepdims=True) + RMS_EPS)
    return (x * r) * g


def _silu(x):
    return x * jax.nn.sigmoid(x)


def _dot(a, b):
    return jnp.dot(a, b, preferred_element_type=F32)


def _dot_nt(a, b):
    return lax.dot_general(a, b, (((1,), (1,)), ((), ())), preferred_element_type=F32)


def _low_lanes(shape):
    return lax.broadcasted_iota(jnp.int32, shape, len(shape) - 1) < HEAD_DIM


def _swap_halves(a):
    return pltpu.roll(a, HEAD_DIM, a.ndim - 1)


def _memkv_kernel(mem_ref, g_ref, w_ref, win_ref, wout_ref, mkt_ref, mvt_ref, kcat_t_ref, vcat_ref,
                  win_bf_ref, wout_bf_ref, *, batch, n_mem):
    win_bf_ref[...] = win_ref[...].astype(BF16)
    wout_bf_ref[...] = wout_ref[...].astype(BF16)
    h = _rmsnorm(mem_ref[...], g_ref[pl.ds(pl.program_id(0), 1), :]).astype(BF16)
    kv = _dot(h, w_ref[...].astype(BF16))
    mk = kv[:, :MEM_DIM]
    mv = kv[:, MEM_DIM:]
    low = _low_lanes((n_mem, LANES))
    for b in range(batch):
        mkt_ref[b] = mk[b * n_mem:(b + 1) * n_mem, :].T
        mvt_ref[b] = mv[b * n_mem:(b + 1) * n_mem, :].T
        for c in range(M_CHUNK):
            kc = mk[b * n_mem:(b + 1) * n_mem, c * LANES:(c + 1) * LANES]
            vc = mv[b * n_mem:(b + 1) * n_mem, c * LANES:(c + 1) * LANES]
            kcat = jnp.concatenate([jnp.where(low, kc, 0.0), jnp.where(low, 0.0, kc)], axis=0)
            vcat = jnp.concatenate([jnp.where(low, vc, 0.0), jnp.where(low, 0.0, vc)], axis=0)
            kcat_t_ref[b, c] = kcat.T.astype(BF16)
            vcat_ref[b, c] = vcat.astype(BF16)


def _memkv(mem2d, norm_mem, w_mem_kv, w_in, w_out, batch, n_mem):
    rows = batch * n_mem
    chunk = D_MODEL // DEPTH
    return pl.pallas_call(
        functools.partial(_memkv_kernel, batch=batch, n_mem=n_mem),
        grid=(DEPTH,),
        in_specs=[
            pl.BlockSpec((rows, D_MODEL), lambda l: (0, 0)),
            pl.BlockSpec((DEPTH, D_MODEL), lambda l: (0, 0)),
            pl.BlockSpec((None, D_MODEL, 2 * MEM_DIM), lambda l: (l, 0, 0)),
            pl.BlockSpec((None, chunk, IN_DIM), lambda l: (0, l, 0)),
            pl.BlockSpec((None, chunk, D_MODEL), lambda l: (0, l, 0)),
        ],
        out_specs=[
            pl.BlockSpec((None, batch, MEM_DIM, n_mem), lambda l: (l, 0, 0, 0)),
            pl.BlockSpec((None, batch, MEM_DIM, n_mem), lambda l: (l, 0, 0, 0)),
            pl.BlockSpec((None, batch, M_CHUNK, LANES, 2 * n_mem), lambda l: (l, 0, 0, 0, 0)),
            pl.BlockSpec((None, batch, M_CHUNK, 2 * n_mem, LANES), lambda l: (l, 0, 0, 0, 0)),
            pl.BlockSpec((None, chunk, IN_DIM), lambda l: (0, l, 0)),
            pl.BlockSpec((None, chunk, D_MODEL), lambda l: (0, l, 0)),
        ],
        out_shape=[
            jax.ShapeDtypeStruct((DEPTH, batch, MEM_DIM, n_mem), F32),
            jax.ShapeDtypeStruct((DEPTH, batch, MEM_DIM, n_mem), F32),
            jax.ShapeDtypeStruct((DEPTH, batch, M_CHUNK, LANES, 2 * n_mem), BF16),
            jax.ShapeDtypeStruct((DEPTH, batch, M_CHUNK, 2 * n_mem, LANES), BF16),
            jax.ShapeDtypeStruct((DEPTH, D_MODEL, IN_DIM), BF16),
            jax.ShapeDtypeStruct((DEPTH, MIX_DIM, D_MODEL), BF16),
        ],
        compiler_params=pltpu.CompilerParams(dimension_semantics=("arbitrary",)),
        name="memkv",
    )(mem2d, norm_mem, w_mem_kv, w_in, w_out)


def _prompt_layer_kernel(sink_ref, x_ref, gpre_ref, gpost_ref, win_ref, convw_ref, kcat_t_ref,
                         vcat_ref, wout_ref, *rest, layer, tile, n_mem, n_tiles, cast_next):
    if cast_next:
        next_win_ref, next_wout_ref, *rest = rest
        xo_ref, klast_ref, vlast_ref, convlast_ref, next_win_bf_ref, next_wout_bf_ref, *rest = rest
        next_win_bf_ref[...] = next_win_ref[...].astype(BF16)
        next_wout_bf_ref[...] = next_wout_ref[...].astype(BF16)
    else:
        xo_ref, klast_ref, vlast_ref, convlast_ref, *rest = rest
    kprev_ref, vprev_ref, ubuf_ref = rest
    _prompt_tile(sink_ref, x_ref, gpre_ref, gpost_ref, win_ref, convw_ref, kcat_t_ref, vcat_ref, wout_ref,
                 xo_ref, klast_ref, vlast_ref, convlast_ref, kprev_ref, vprev_ref, ubuf_ref,
                 layer=layer, tile=tile, n_mem=n_mem, n_tiles=n_tiles)


def _prompt_tile(sink_ref, x_ref, gpre_ref, gpost_ref, win_ref, convw_ref, kcat_t_ref,
                 vcat_ref, wout_ref,
                 xo_ref, klast_ref, vlast_ref, convlast_ref,
                 kprev_ref, vprev_ref, ubuf_ref, *, layer, tile, n_mem, n_tiles):
    i = pl.program_id(1)
    sizes = PROMPT_PART_ROWS
    starts = [sum(sizes[:p]) for p in range(len(sizes))]
    g_pre = gpre_ref[layer:layer + 1, :]
    g_post = gpost_ref[layer:layer + 1, :]

    @pl.when(i == 0)
    def _():
        kprev_ref[...] = jnp.zeros_like(kprev_ref)
        vprev_ref[...] = jnp.zeros_like(vprev_ref)
        ubuf_ref[0:SUBLANES, :] = jnp.zeros((SUBLANES, CONV_DIM), F32)

    rows = N_CHUNK * WINDOW
    qpos = lax.broadcasted_iota(jnp.int32, (rows, 2 * WINDOW), 0) % WINDOW
    kpos = lax.broadcasted_iota(jnp.int32, (rows, 2 * WINDOW), 1)
    band = (kpos > qpos) & (kpos <= qpos + WINDOW)
    band_first = band & ((kpos >= WINDOW) | (i > 0))
    chunk_of_row = lax.broadcasted_iota(jnp.int32, (rows, 1), 0) // WINDOW
    sink_lo = jnp.zeros((rows, 1), F32)
    sink_hi = jnp.zeros((rows, 1), F32)
    for p in range(N_CHUNK):
        sink_lo = jnp.where(chunk_of_row == p, sink_ref[layer, p] * LOG2E, sink_lo)
        sink_hi = jnp.where(chunk_of_row == p, sink_ref[layer, N_CHUNK + p] * LOG2E, sink_hi)
    low_o = _low_lanes((rows, LANES))
    low_w = _low_lanes((WINDOW, LANES))

    def start(pi):
        x = x_ref[starts[pi]:starts[pi] + sizes[pi], :]
        h = _rmsnorm(x, g_pre).astype(BF16)
        proj = lambda off, width: _dot(h, win_ref[:, off:off + width])
        return dict(x=x, proj=proj, q_raw=proj(OFF_Q, ATTN_DIM), kv=proj(OFF_K, 2 * KV_DIM),
                    mq_raw=proj(OFF_MQ, MEM_DIM))

    def mix_part(pi, st, k_prev, v_prev, fill_one):
        part = sizes[pi]
        n_blocks = part // WINDOW
        low_t = _low_lanes((part, LANES))
        low_kv = _low_lanes((WINDOW + part, KV_DIM))
        k = st["kv"][:, :KV_DIM]
        v = st["kv"][:, KV_DIM:]
        kfull = jnp.concatenate([k_prev, k], axis=0)
        vfull = jnp.concatenate([v_prev, v], axis=0)
        klo = jnp.where(low_kv, kfull, 0.0).astype(BF16)
        khi = jnp.where(low_kv, 0.0, kfull).astype(BF16)
        vlo = jnp.where(low_kv, vfull, 0.0).astype(BF16)
        vhi = jnp.where(low_kv, 0.0, vfull).astype(BF16)

        q = st["q_raw"] * (SCALE * LOG2E)
        nat = [q[:, c * LANES:(c + 1) * LANES] for c in range(N_CHUNK)]
        qp = []
        for p in range(N_CHUNK):
            a, b = nat[p // 2], nat[CHUNKS_PER_KV + p // 2]
            pair = jnp.where(low_t, a, _swap_halves(b)) if p % 2 == 0 else jnp.where(low_t, _swap_halves(a), b)
            qp.append(pair.astype(BF16))

        def swa_scores(j):
            r0 = j * WINDOW
            q_all = jnp.concatenate([qp[p][r0:r0 + WINDOW] for p in range(N_CHUNK)], axis=0)
            k_cat = jnp.concatenate([klo[r0:r0 + 2 * WINDOW], khi[r0:r0 + 2 * WINDOW]], axis=0)
            return _dot_nt(q_all, k_cat)

        o_blocks = []
        fill_one()
        s = swa_scores(0)
        for j in range(n_blocks):
            r0 = j * WINDOW
            fill_one()
            s_next = swa_scores(j + 1) if j + 1 < n_blocks else None
            v_cat = jnp.concatenate([vlo[r0:r0 + 2 * WINDOW], vhi[r0:r0 + 2 * WINDOW]], axis=0)
            mask = band_first if (pi == 0 and j == 0) else band
            s_lo = jnp.where(mask, s[:, :2 * WINDOW], -jnp.inf)
            s_hi = jnp.where(mask, s[:, 2 * WINDOW:], -jnp.inf)
            m_lo = jnp.maximum(jnp.max(s_lo, axis=1, keepdims=True), sink_lo)
            m_hi = jnp.maximum(jnp.max(s_hi, axis=1, keepdims=True), sink_hi)
            p_lo = jnp.exp2(s_lo - m_lo)
            p_hi = jnp.exp2(s_hi - m_hi)
            d_lo = jnp.sum(p_lo, axis=1, keepdims=True) + jnp.exp2(sink_lo - m_lo)
            d_hi = jnp.sum(p_hi, axis=1, keepdims=True) + jnp.exp2(sink_hi - m_hi)
            pr = jnp.concatenate([p_lo, p_hi], axis=1).astype(BF16)
            o = _dot(pr, v_cat)
            o = o * jnp.where(low_o, 1.0 / d_lo, 1.0 / d_hi)
            op = [o[p * WINDOW:(p + 1) * WINDOW] for p in range(N_CHUNK)]
            o_blocks.append(jnp.concatenate(
                [jnp.where(low_w, op[0], _swap_halves(op[1])), jnp.where(low_w, op[2], _swap_halves(op[3])),
                 jnp.where(low_w, _swap_halves(op[0]), op[1]), jnp.where(low_w, _swap_halves(op[2]), op[3])],
                axis=1))
            s = s_next
        o_b = jnp.concatenate(o_blocks, axis=0) if len(o_blocks) > 1 else o_blocks[0]

        mq = (st["mq_raw"] * (SCALE * LOG2E)).astype(BF16)
        oc_chunks = []
        for c in range(M_CHUNK):
            s = _dot(mq[:, c * LANES:(c + 1) * LANES], kcat_t_ref[c])
            fill_one()
            s0 = s[:, :n_mem]
            s1 = s[:, n_mem:]
            p0 = jnp.exp2(s0 - jnp.max(s0, axis=1, keepdims=True))
            p1 = jnp.exp2(s1 - jnp.max(s1, axis=1, keepdims=True))
            d0 = jnp.sum(p0, axis=1, keepdims=True)
            d1 = jnp.sum(p1, axis=1, keepdims=True)
            pm = jnp.concatenate([p0, p1], axis=1).astype(BF16)
            o = _dot(pm, vcat_ref[c])
            oc_chunks.append(o * jnp.where(low_t, 1.0 / d0, 1.0 / d1))
        o_c = jnp.concatenate(oc_chunks, axis=1)
        return o_b, o_c, k[part - WINDOW:part, :], v[part - WINDOW:part, :]

    def gate_mix(pi, filled, o_b, o_c):
        out_b = _silu(jnp.concatenate([filled["ag0"], filled["ag1"]], axis=1)) * o_b
        out_c = _silu(filled["mg"]) * o_c
        u = filled["cc"] * filled["ch"]
        part = sizes[pi]
        base = SUBLANES + starts[pi]
        ubuf_ref[base:base + part, :] = u
        u1 = ubuf_ref[base - 1:base - 1 + part, :]
        u2 = ubuf_ref[base - 2:base - 2 + part, :]
        cw = convw_ref[...]
        conv = cw[0:1, :] * u2 + cw[1:2, :] * u1 + cw[2:3, :] * u
        out_a = _silu(filled["cg"]) * filled["cb"] * conv
        return jnp.concatenate([out_a, out_b, out_c], axis=1).astype(BF16), u[part - SUBLANES:part, :]

    def finish(pi, st, mix):
        part = sizes[pi]
        half = part // 2
        y0 = _dot(mix[:half], wout_ref[...])
        y1 = _dot(mix[half:], wout_ref[...])
        r0 = starts[pi]
        xo_ref[r0:r0 + half, :] = st["x"][:half] + _rmsnorm(y0, g_post)
        xo_ref[r0 + half:r0 + part, :] = st["x"][half:] + _rmsnorm(y1, g_post)

    filler_cols = [("cg", OFF_CG), ("cb", OFF_CB), ("cc", OFF_CC), ("ch", OFF_CH),
                   ("ag0", OFF_AG), ("ag1", OFF_AG + 2 * LANES), ("mg", OFF_MG)]

    states = [start(0)]
    k_prev, v_prev = kprev_ref[...], vprev_ref[...]
    u_tail = None
    for pi in range(len(sizes)):
        st = states[pi]
        pending = list(filler_cols)
        filled = {}

        def fill_one():
            if pending:
                name, off = pending.pop(0)
                filled[name] = st["proj"](off, 2 * LANES)

        if pi + 1 < len(sizes):
            states.append(start(pi + 1))
        o_b, o_c, k_prev, v_prev = mix_part(pi, st, k_prev, v_prev, fill_one)
        while pending:
            fill_one()
        mix, u_tail = gate_mix(pi, filled, o_b, o_c)
        finish(pi, st, mix)

    kprev_ref[...] = k_prev
    vprev_ref[...] = v_prev
    ubuf_ref[0:SUBLANES, :] = u_tail
    convlast_ref[...] = u_tail

    @pl.when(i == n_tiles - 1)
    def _():
        klast_ref[...] = kprev_ref[...].T
        vlast_ref[...] = vprev_ref[...].T


def _prompt_layer(l, x, sinks, norm_pre, norm_post, w_in_bf, conv_w, kcat_t, vcat, w_out_bf, w_in, w_out, n_mem):
    batch, seq, _ = x.shape
    tile = PROMPT_TILE
    nt = seq // tile
    cast_next = l + 1 < DEPTH
    chunk = D_MODEL // (batch * nt)
    next_chunk = lambda b, i: (l + 1, b * nt + i, 0)
    in_specs = [
        pl.BlockSpec(memory_space=pltpu.SMEM),
        pl.BlockSpec((None, tile, D_MODEL), lambda b, i: (b, i, 0)),
        pl.BlockSpec((DEPTH, D_MODEL), lambda b, i: (0, 0)),
        pl.BlockSpec((DEPTH, D_MODEL), lambda b, i: (0, 0)),
        pl.BlockSpec((None, D_MODEL, IN_DIM), lambda b, i: (l, 0, 0)),
        pl.BlockSpec((None, CONV_W, CONV_DIM), lambda b, i: (l, 0, 0)),
        pl.BlockSpec((None, None, M_CHUNK, LANES, 2 * n_mem), lambda b, i: (l, b, 0, 0, 0)),
        pl.BlockSpec((None, None, M_CHUNK, 2 * n_mem, LANES), lambda b, i: (l, b, 0, 0, 0)),
        pl.BlockSpec((None, MIX_DIM, D_MODEL), lambda b, i: (l, 0, 0)),
    ]
    out_specs = [
        pl.BlockSpec((None, tile, D_MODEL), lambda b, i: (b, i, 0)),
        pl.BlockSpec((None, KV_DIM, WINDOW), lambda b, i: (b, 0, 0)),
        pl.BlockSpec((None, KV_DIM, WINDOW), lambda b, i: (b, 0, 0)),
        pl.BlockSpec((None, SUBLANES, CONV_DIM), lambda b, i: (b, 0, 0)),
    ]
    out_shape = [
        jax.ShapeDtypeStruct((batch, seq, D_MODEL), F32),
        jax.ShapeDtypeStruct((batch, KV_DIM, WINDOW), F32),
        jax.ShapeDtypeStruct((batch, KV_DIM, WINDOW), F32),
        jax.ShapeDtypeStruct((batch, SUBLANES, CONV_DIM), F32),
    ]
    operands = [sinks, x, norm_pre, norm_post, w_in_bf, conv_w, kcat_t, vcat, w_out_bf]
    aliases = {}
    if cast_next:
        assert chunk * batch * nt == D_MODEL and chunk % (2 * SUBLANES) == 0
        in_specs += [pl.BlockSpec((None, chunk, IN_DIM), next_chunk),
                     pl.BlockSpec((None, chunk, D_MODEL), next_chunk)]
        out_specs += [pl.BlockSpec((None, chunk, IN_DIM), next_chunk),
                      pl.BlockSpec((None, chunk, D_MODEL), next_chunk)]
        out_shape += [jax.ShapeDtypeStruct(w_in_bf.shape, BF16), jax.ShapeDtypeStruct(w_out_bf.shape, BF16)]
        operands += [w_in, w_out]
        aliases = {4: 4, 8: 5}
    outs = pl.pallas_call(
        functools.partial(_prompt_layer_kernel, layer=l, tile=tile, n_mem=n_mem, n_tiles=nt, cast_next=cast_next),
        grid=(batch, nt),
        in_specs=in_specs,
        out_specs=out_specs,
        out_shape=out_shape,
        input_output_aliases=aliases,
        scratch_shapes=[
            pltpu.VMEM((WINDOW, KV_DIM), F32),
            pltpu.VMEM((WINDOW, KV_DIM), F32),
            pltpu.VMEM((SUBLANES + tile, CONV_DIM), F32),
        ],
        compiler_params=pltpu.CompilerParams(
            dimension_semantics=("arbitrary", "arbitrary"), vmem_limit_bytes=VMEM_LIMIT),
        name="prompt_layer",
    )(*operands)
    return tuple(outs) if cast_next else (*outs, w_in_bf, w_out_bf)


def _sample_kernel(x_ref, gpre_ref, gpost_ref, win_ref, wout_ref, convw_ref, sink_ref, convp_ref,
                   kt_ref, vt_ref, mkt_ref, mvt_ref,
                   y_ref, kto_ref, vto_ref, uo_ref,
                   xs_ref, z_ref, mix_ref, *, group, t, n_mem, n_sub):
    l = pl.program_id(0)
    p = pl.program_id(1)
    ss = pl.program_id(2)
    nrows = group * t
    prows = n_sub * nrows
    rows_p = pl.ds(pl.multiple_of(p * prows, prows), prows)
    rows_s = pl.ds(pl.multiple_of(ss * nrows, nrows), nrows)

    @pl.when(ss == 0)
    def _():
        @pl.when(l == 0)
        def _():
            xs_ref[rows_p, :] = x_ref[...]

        h = _rmsnorm(xs_ref[rows_p, :], gpre_ref[pl.ds(l, 1), :]).astype(BF16)
        z_ref[...] = _dot(h, win_ref[...])

    def zc(off, width):
        return z_ref[rows_s, off:off + width]

    def per_seq(a):
        return a.reshape(group, t, a.shape[-1])

    u2d = zc(OFF_CC, CONV_DIM) * zc(OFF_CH, CONV_DIM)
    uo_ref[...] = u2d
    u = per_seq(u2d)
    prev = convp_ref[...]
    tpos = lax.broadcasted_iota(jnp.int32, (group, t, CONV_DIM), 1)
    u1 = jnp.where(tpos >= 1, pltpu.roll(u, 1, 1), pltpu.roll(prev, 1, 1))
    u2 = jnp.where(tpos >= 2, pltpu.roll(u, 2, 1), pltpu.roll(prev, 2, 1))
    cw = convw_ref[...]
    conv = cw[0:1, :] * u2 + cw[1:2, :] * u1 + cw[2:3, :] * u
    out_a = per_seq(_silu(zc(OFF_CG, CONV_DIM)) * zc(OFF_CB, CONV_DIM)) * conv

    kt_old = kt_ref[...]
    vt_old = vt_ref[...]
    k_new = zc(OFF_K, KV_DIM)
    v_new = zc(OFF_V, KV_DIM)
    k_new_t = jnp.swapaxes(per_seq(k_new), 1, 2)
    v_new_t = jnp.swapaxes(per_seq(v_new), 1, 2)
    kto_ref[...] = pltpu.roll(jnp.concatenate([k_new_t, kt_old[:, :, t:]], axis=2), WINDOW - t, 2)
    vto_ref[...] = pltpu.roll(jnp.concatenate([v_new_t, vt_old[:, :, t:]], axis=2), WINDOW - t, 2)

    q = per_seq(zc(OFF_Q, ATTN_DIM) * SCALE)
    low_q = _low_lanes((group, t, LANES))
    pieces = []
    for c in range(N_CHUNK):
        qc = q[:, :, c * LANES:(c + 1) * LANES]
        qs = _swap_halves(qc)
        if c // CHUNKS_PER_KV == 0:
            pieces += [jnp.where(low_q, qc, 0.0), jnp.where(low_q, qs, 0.0)]
        else:
            pieces += [jnp.where(low_q, 0.0, qs), jnp.where(low_q, 0.0, qc)]
    qbd = jnp.concatenate(pieces, axis=1).astype(BF16)
    nrow = ATTN_HEADS * t
    s_old = jnp.einsum('gqd,gdk->gqk', qbd, kt_old.astype(BF16), preferred_element_type=F32)
    s_new = _dot_nt(qbd.reshape(group * nrow, KV_DIM), k_new.astype(BF16)).reshape(group, nrow, group * t)
    tq = lax.broadcasted_iota(jnp.int32, (group, nrow, WINDOW), 1) % t
    col = lax.broadcasted_iota(jnp.int32, (group, nrow, WINDOW), 2)
    first_new = lax.broadcasted_iota(jnp.int32, (group, nrow, WINDOW), 0) * t
    s_old = jnp.where(col > tq, s_old, -jnp.inf)
    s_new = jnp.where((col >= first_new) & (col <= first_new + tq), s_new, -jnp.inf)
    sink = sink_ref[...]
    m = jnp.maximum(jnp.max(jnp.maximum(s_old, s_new), axis=2, keepdims=True), sink)
    p_old = jnp.exp(s_old - m)
    p_new = jnp.exp(s_new - m)
    d = jnp.sum(p_old + p_new, axis=2, keepdims=True) + jnp.exp(sink - m)
    o = jnp.einsum('gqk,gdk->gqd', p_old.astype(BF16), vt_old.astype(BF16), preferred_element_type=F32)
    o = o + _dot(p_new.astype(BF16).reshape(group * nrow, group * t), v_new.astype(BF16)).reshape(group, nrow, KV_DIM)
    o = o * (1.0 / d)
    ob_chunks = []
    for c in range(N_CHUNK):
        o_even = o[:, 2 * c * t:(2 * c + 1) * t]
        o_odd = o[:, (2 * c + 1) * t:(2 * c + 2) * t]
        if c // CHUNKS_PER_KV == 0:
            ob_chunks.append(jnp.where(low_q, o_even, _swap_halves(o_odd)))
        else:
            ob_chunks.append(jnp.where(low_q, _swap_halves(o_even), o_odd))
    o_b = jnp.concatenate(ob_chunks, axis=2)
    out_b = per_seq(_silu(zc(OFF_AG, ATTN_DIM))) * o_b

    mq = per_seq(zc(OFF_MQ, MEM_DIM) * SCALE)
    head_of_lane = lax.broadcasted_iota(jnp.int32, (group, t, MEM_DIM), 2) // HEAD_DIM
    qm = jnp.concatenate([jnp.where(head_of_lane == hh, mq, 0.0) for hh in range(MEM_HEADS)],
                         axis=1).astype(BF16)
    s = jnp.einsum('gqd,gdk->gqk', qm, mkt_ref[...].astype(BF16), preferred_element_type=F32)
    p = jnp.exp(s - jnp.max(s, axis=2, keepdims=True))
    d = jnp.sum(p, axis=2, keepdims=True)
    o = jnp.einsum('gqk,gdk->gqd', p.astype(BF16), mvt_ref[...].astype(BF16), preferred_element_type=F32)
    o = o * (1.0 / d)
    o_c = jnp.zeros((group, t, MEM_DIM), F32)
    for hh in range(MEM_HEADS):
        o_c = jnp.where(head_of_lane == hh, o[:, hh * t:(hh + 1) * t], o_c)
    out_c = per_seq(_silu(zc(OFF_MG, MEM_DIM))) * o_c

    mix_ref[rows_s, :] = jnp.concatenate([out_a, out_b, out_c], axis=2).reshape(nrows, MIX_DIM).astype(BF16)

    @pl.when(ss == n_sub - 1)
    def _():
        y = _dot(mix_ref[...], wout_ref[...])
        x_new = xs_ref[rows_p, :] + _rmsnorm(y, gpost_ref[pl.ds(l, 1), :])
        xs_ref[rows_p, :] = x_new

        @pl.when(l == DEPTH - 1)
        def _():
            y_ref[...] = x_new


def _sample_stream(x2d, norm_pre, norm_post, w_in_bf, w_out_bf, conv_w, sink_rows, convp, cache_kt, cache_vt,
                   cache_mkt, cache_mvt, n_seq, t, n_mem):
    group = SAMPLE_GROUP
    n_sub = SAMPLE_SUBSTEPS
    n_proj = n_seq // (group * n_sub)
    nrows = group * t
    prows = n_sub * nrows
    nrow_attn = ATTN_HEADS * t
    per_layer = lambda l, p, ss: (l, 0, 0)
    per_group = lambda l, p, ss: (l, p * n_sub + ss, 0, 0)
    return pl.pallas_call(
        functools.partial(_sample_kernel, group=group, t=t, n_mem=n_mem, n_sub=n_sub),
        grid=(DEPTH, n_proj, n_sub),
        in_specs=[
            pl.BlockSpec((prows, D_MODEL), lambda l, p, ss: (jnp.where(l == 0, p, n_proj - 1), 0)),
            pl.BlockSpec((DEPTH, D_MODEL), lambda l, p, ss: (0, 0)),
            pl.BlockSpec((DEPTH, D_MODEL), lambda l, p, ss: (0, 0)),
            pl.BlockSpec((None, D_MODEL, IN_DIM), per_layer),
            pl.BlockSpec((None, MIX_DIM, D_MODEL), per_layer),
            pl.BlockSpec((None, CONV_W, CONV_DIM), per_layer),
            pl.BlockSpec((None, nrow_attn, 1), per_layer),
            pl.BlockSpec((None, group, t, CONV_DIM), per_group),
            pl.BlockSpec((None, group, KV_DIM, WINDOW), per_group),
            pl.BlockSpec((None, group, KV_DIM, WINDOW), per_group),
            pl.BlockSpec((None, group, MEM_DIM, n_mem), per_group),
            pl.BlockSpec((None, group, MEM_DIM, n_mem), per_group),
        ],
        out_specs=[
            pl.BlockSpec((prows, D_MODEL), lambda l, p, ss: (jnp.where(l == DEPTH - 1, p, 0), 0)),
            pl.BlockSpec((None, group, KV_DIM, WINDOW), per_group),
            pl.BlockSpec((None, group, KV_DIM, WINDOW), per_group),
            pl.BlockSpec((None, nrows, CONV_DIM), lambda l, p, ss: (l, p * n_sub + ss, 0)),
        ],
        out_shape=[
            jax.ShapeDtypeStruct((n_seq * t, D_MODEL), F32),
            jax.ShapeDtypeStruct((DEPTH, n_seq, KV_DIM, WINDOW), F32),
            jax.ShapeDtypeStruct((DEPTH, n_seq, KV_DIM, WINDOW), F32),
            jax.ShapeDtypeStruct((DEPTH, n_seq * t, CONV_DIM), F32),
        ],
        scratch_shapes=[
            pltpu.VMEM((n_seq * t, D_MODEL), F32),
            pltpu.VMEM((prows, IN_DIM), F32),
            pltpu.VMEM((prows, MIX_DIM), BF16),
        ],
        compiler_params=pltpu.CompilerParams(
            dimension_semantics=("arbitrary", "arbitrary", "arbitrary"), vmem_limit_bytes=VMEM_LIMIT),
        name="sample_stream",
    )(x2d, norm_pre, norm_post, w_in_bf, w_out_bf, conv_w, sink_rows, convp, cache_kt, cache_vt, cache_mkt, cache_mvt)


def _keys_minor(a):
    lead = a.shape[:-3]
    n, heads, hd = a.shape[-3:]
    nd = a.ndim
    perm = tuple(range(nd - 3)) + (nd - 2, nd - 1, nd - 3)
    return jnp.transpose(a, perm).reshape(*lead, heads * hd, n)


def _keys_major(a, heads):
    lead = a.shape[:-2]
    n = a.shape[-1]
    a = a.reshape(*lead, heads, HEAD_DIM, n)
    nd = a.ndim
    perm = tuple(range(nd - 3)) + (nd - 1, nd - 3, nd - 2)
    return jnp.transpose(a, perm)


def kernel(x_prompt, x_sample, mem_prompt, cache_win_k, cache_win_v, state_conv, cache_mem_k, cache_mem_v,
           norm_pre, norm_post, norm_mem, w_in, conv_w, attn_sinks, w_mem_kv, w_out):
    batch, seq, _ = x_prompt.shape
    n_seq, t, _ = x_sample.shape
    n_mem = mem_prompt.shape[1]
    assert seq % PROMPT_TILE == 0 and sum(PROMPT_PART_ROWS) == PROMPT_TILE
    assert all(rows % WINDOW == 0 for rows in PROMPT_PART_ROWS)
    assert n_seq % (SAMPLE_GROUP * SAMPLE_SUBSTEPS) == 0 and t == SUBLANES

    sink_rows = jnp.repeat(attn_sinks, t, axis=1)[:, :, None]

    mkt, mvt, kcat_t, vcat, w_in_bf, w_out_bf = _memkv(
        mem_prompt.reshape(batch * n_mem, D_MODEL), norm_mem, w_mem_kv, w_in, w_out, batch, n_mem)

    xp = x_prompt
    ktp, vtp, cvp = [], [], []
    for l in range(DEPTH):
        xp, k_p, v_p, c_p, w_in_bf, w_out_bf = _prompt_layer(
            l, xp, attn_sinks, norm_pre, norm_post, w_in_bf, conv_w, kcat_t, vcat, w_out_bf, w_in, w_out, n_mem)
        ktp.append(k_p)
        vtp.append(v_p)
        cvp.append(c_p[:, SUBLANES - CONV_BUF:, :])

    convp = jnp.pad(state_conv, ((0, 0), (0, 0), (t - CONV_BUF, 0), (0, 0)))
    ys, kts, vts, us = _sample_stream(
        x_sample.reshape(n_seq * t, D_MODEL), norm_pre, norm_post, w_in_bf, w_out_bf, conv_w, sink_rows, convp,
        _keys_minor(cache_win_k), _keys_minor(cache_win_v), _keys_minor(cache_mem_k), _keys_minor(cache_mem_v),
        n_seq, t, n_mem)

    return (xp,
            ys.reshape(n_seq, t, D_MODEL),
            _keys_major(jnp.stack(ktp), KV_HEADS),
            _keys_major(jnp.stack(vtp), KV_HEADS),
            jnp.stack(cvp),
            _keys_major(mkt, MEM_HEADS),
            _keys_major(mvt, MEM_HEADS),
            _keys_major(kts, KV_HEADS),
            _keys_major(vts, KV_HEADS),
            us.reshape(DEPTH, n_seq, t, CONV_DIM)[:, :, t - CONV_BUF:, :])
```

```python
import functools

import jax
import jax.numpy as jnp
from jax import lax
from jax.experimental import pallas as pl
from jax.experimental.pallas import tpu as pltpu

D_MODEL = 1024
DEPTH = 4
HEAD_DIM = 64
ATTN_HEADS = 8
KV_HEADS = 2
ATTN_DIM = ATTN_HEADS * HEAD_DIM
KV_DIM = KV_HEADS * HEAD_DIM
WINDOW = 128
MEM_HEADS = 4
MEM_DIM = MEM_HEADS * HEAD_DIM
CONV_DIM = 256
CONV_W = 3
CONV_BUF = CONV_W - 1
MIX_DIM = CONV_DIM + ATTN_DIM + MEM_DIM
IN_DIM = 4 * CONV_DIM + 2 * ATTN_DIM + 2 * KV_DIM + 2 * MEM_DIM
RMS_EPS = 1e-6
SCALE = HEAD_DIM ** -0.5
LOG2E = 1.4426950408889634

LANES = 128
SUBLANES = 8
VMEM_LIMIT = 56 * 1024 * 1024

OFF_CB, OFF_CC, OFF_CH, OFF_CG = 0, 256, 512, 768
OFF_Q, OFF_K, OFF_V, OFF_AG = 1024, 1536, 1664, 1792
OFF_MQ, OFF_MG = 2304, 2560

N_CHUNK = ATTN_DIM // LANES
M_CHUNK = MEM_DIM // LANES
CHUNKS_PER_KV = N_CHUNK // KV_HEADS

PROMPT_TILE = 1024
PROMPT_PART_ROWS = (512, 512)
SAMPLE_GROUP = 16
SAMPLE_SUBSTEPS = 2

F32 = jnp.float32
BF16 = jnp.bfloat16


def _rmsnorm(x, g):
    r = lax.rsqrt(jnp.mean(x * x, axis=-1, keepdims=True) + RMS_EPS)
    return (x * r) * g


def _silu(x):
    return x * jax.nn.sigmoid(x)


def _dot(a, b):
    return jnp.dot(a, b, preferred_element_type=F32)


def _dot_nt(a, b):
    return lax.dot_general(a, b, (((1,), (1,)), ((), ())), preferred_element_type=F32)


def _low_lanes(shape):
    return lax.broadcasted_iota(jnp.int32, shape, len(shape) - 1) < HEAD_DIM


def _swap_halves(a):
    return pltpu.roll(a, HEAD_DIM, a.ndim - 1)


def _memkv_kernel(mem_ref, g_ref, w_ref, win_ref, wout_ref, mkt_ref, mvt_ref, kcat_t_ref, vcat_ref,
                  win_bf_ref, wout_bf_ref, *, batch, n_mem):
    win_bf_ref[0] = win_ref[...].astype(BF16)
    wout_bf_ref[0] = wout_ref[...].astype(BF16)
    win_bf_ref[1:] = jnp.zeros((DEPTH - 1,) + win_ref.shape, BF16)
    wout_bf_ref[1:] = jnp.zeros((DEPTH - 1,) + wout_ref.shape, BF16)
    h = _rmsnorm(mem_ref[...], g_ref[pl.ds(pl.program_id(0), 1), :]).astype(BF16)
    kv = _dot(h, w_ref[...].astype(BF16))
    mk = kv[:, :MEM_DIM]
    mv = kv[:, MEM_DIM:]
    low = _low_lanes((n_mem, LANES))
    for b in range(batch):
        mkt_ref[b] = mk[b * n_mem:(b + 1) * n_mem, :].T
        mvt_ref[b] = mv[b * n_mem:(b + 1) * n_mem, :].T
        for c in range(M_CHUNK):
            kc = mk[b * n_mem:(b + 1) * n_mem, c * LANES:(c + 1) * LANES]
            vc = mv[b * n_mem:(b + 1) * n_mem, c * LANES:(c + 1) * LANES]
            kcat = jnp.concatenate([jnp.where(low, kc, 0.0), jnp.where(low, 0.0, kc)], axis=0)
            vcat = jnp.concatenate([jnp.where(low, vc, 0.0), jnp.where(low, 0.0, vc)], axis=0)
            kcat_t_ref[b, c] = kcat.T.astype(BF16)
            vcat_ref[b, c] = vcat.astype(BF16)


def _memkv(mem2d, norm_mem, w_mem_kv, w_in, w_out, batch, n_mem):
    rows = batch * n_mem
    chunk = D_MODEL // DEPTH
    return pl.pallas_call(
        functools.partial(_memkv_kernel, batch=batch, n_mem=n_mem),
        grid=(DEPTH,),
        in_specs=[
            pl.BlockSpec((rows, D_MODEL), lambda l: (0, 0)),
            pl.BlockSpec((DEPTH, D_MODEL), lambda l: (0, 0)),
            pl.BlockSpec((None, D_MODEL, 2 * MEM_DIM), lambda l: (l, 0, 0)),
            pl.BlockSpec((None, chunk, IN_DIM), lambda l: (0, l, 0)),
            pl.BlockSpec((None, chunk, D_MODEL), lambda l: (0, l, 0)),
        ],
        out_specs=[
            pl.BlockSpec((None, batch, MEM_DIM, n_mem), lambda l: (l, 0, 0, 0)),
            pl.BlockSpec((None, batch, MEM_DIM, n_mem), lambda l: (l, 0, 0, 0)),
            pl.BlockSpec((None, batch, M_CHUNK, LANES, 2 * n_mem), lambda l: (l, 0, 0, 0, 0)),
            pl.BlockSpec((None, batch, M_CHUNK, 2 * n_mem, LANES), lambda l: (l, 0, 0, 0, 0)),
            pl.BlockSpec((DEPTH, chunk, IN_DIM), lambda l: (0, l, 0)),
            pl.BlockSpec((DEPTH, chunk, D_MODEL), lambda l: (0, l, 0)),
        ],
        out_shape=[
            jax.ShapeDtypeStruct((DEPTH, batch, MEM_DIM, n_mem), F32),
            jax.ShapeDtypeStruct((DEPTH, batch, MEM_DIM, n_mem), F32),
            jax.ShapeDtypeStruct((DEPTH, batch, M_CHUNK, LANES, 2 * n_mem), BF16),
            jax.ShapeDtypeStruct((DEPTH, batch, M_CHUNK, 2 * n_mem, LANES), BF16),
            jax.ShapeDtypeStruct((DEPTH, D_MODEL, IN_DIM), BF16),
            jax.ShapeDtypeStruct((DEPTH, MIX_DIM, D_MODEL), BF16),
        ],
        compiler_params=pltpu.CompilerParams(dimension_semantics=("arbitrary",)),
        name="memkv",
    )(mem2d, norm_mem, w_mem_kv, w_in, w_out)


N_MAIN_IN, N_MAIN_OUT = 9, 4


def _prompt_layer_kernel(*refs, layer, tile, n_mem, n_tiles, n_cast):
    main_in = refs[:N_MAIN_IN]
    cast_in = refs[N_MAIN_IN:N_MAIN_IN + n_cast]
    outs = refs[N_MAIN_IN + n_cast:]
    main_out, cast_out, scratch = outs[:N_MAIN_OUT], outs[N_MAIN_OUT:N_MAIN_OUT + n_cast], outs[N_MAIN_OUT + n_cast:]
    for src, dst in zip(cast_in, cast_out):
        dst[...] = src[...].astype(BF16)
    _prompt_tile(*main_in, *main_out, *scratch, layer=layer, tile=tile, n_mem=n_mem, n_tiles=n_tiles)


def _prompt_tile(sink_ref, x_ref, gpre_ref, gpost_ref, win_ref, convw_ref, kcat_t_ref,
                 vcat_ref, wout_ref,
                 xo_ref, klast_ref, vlast_ref, convlast_ref,
                 kprev_ref, vprev_ref, ubuf_ref, *, layer, tile, n_mem, n_tiles):
    i = pl.program_id(1)
    sizes = PROMPT_PART_ROWS
    starts = [sum(sizes[:p]) for p in range(len(sizes))]
    g_pre = gpre_ref[layer:layer + 1, :]
    g_post = gpost_ref[layer:layer + 1, :]

    @pl.when(i == 0)
    def _():
        kprev_ref[...] = jnp.zeros_like(kprev_ref)
        vprev_ref[...] = jnp.zeros_like(vprev_ref)
        ubuf_ref[0:SUBLANES, :] = jnp.zeros((SUBLANES, CONV_DIM), F32)

    rows = N_CHUNK * WINDOW
    qpos = lax.broadcasted_iota(jnp.int32, (rows, 2 * WINDOW), 0) % WINDOW
    kpos = lax.broadcasted_iota(jnp.int32, (rows, 2 * WINDOW), 1)
    band = (kpos > qpos) & (kpos <= qpos + WINDOW)
    band_first = band & ((kpos >= WINDOW) | (i > 0))
    chunk_of_row = lax.broadcasted_iota(jnp.int32, (rows, 1), 0) // WINDOW
    sink_lo = jnp.zeros((rows, 1), F32)
    sink_hi = jnp.zeros((rows, 1), F32)
    for p in range(N_CHUNK):
        sink_lo = jnp.where(chunk_of_row == p, sink_ref[layer, p] * LOG2E, sink_lo)
        sink_hi = jnp.where(chunk_of_row == p, sink_ref[layer, N_CHUNK + p] * LOG2E, sink_hi)
    low_o = _low_lanes((rows, LANES))
    low_w = _low_lanes((WINDOW, LANES))

    def start(pi):
        x = x_ref[starts[pi]:starts[pi] + sizes[pi], :]
        h = _rmsnorm(x, g_pre).astype(BF16)
        proj = lambda off, width: _dot(h, win_ref[:, off:off + width])
        return dict(x=x, proj=proj, q_raw=proj(OFF_Q, ATTN_DIM), kv=proj(OFF_K, 2 * KV_DIM),
                    mq_raw=proj(OFF_MQ, MEM_DIM))

    def mix_part(pi, st, k_prev, v_prev, fill_one):
        part = sizes[pi]
        n_blocks = part // WINDOW
        low_t = _low_lanes((part, LANES))
        low_kv = _low_lanes((WINDOW + part, KV_DIM))
        k = st["kv"][:, :KV_DIM]
        v = st["kv"][:, KV_DIM:]
        kfull = jnp.concatenate([k_prev, k], axis=0)
        vfull = jnp.concatenate([v_prev, v], axis=0)
        klo = jnp.where(low_kv, kfull, 0.0).astype(BF16)
        khi = jnp.where(low_kv, 0.0, kfull).astype(BF16)
        vlo = jnp.where(low_kv, vfull, 0.0).astype(BF16)
        vhi = jnp.where(low_kv, 0.0, vfull).astype(BF16)

        q = st["q_raw"] * (SCALE * LOG2E)
        nat = [q[:, c * LANES:(c + 1) * LANES] for c in range(N_CHUNK)]
        qp = []
        for p in range(N_CHUNK):
            a, b = nat[p // 2], nat[CHUNKS_PER_KV + p // 2]
            pair = jnp.where(low_t, a, _swap_halves(b)) if p % 2 == 0 else jnp.where(low_t, _swap_halves(a), b)
            qp.append(pair.astype(BF16))

        def swa_scores(j):
            r0 = j * WINDOW
            q_all = jnp.concatenate([qp[p][r0:r0 + WINDOW] for p in range(N_CHUNK)], axis=0)
            k_cat = jnp.concatenate([klo[r0:r0 + 2 * WINDOW], khi[r0:r0 + 2 * WINDOW]], axis=0)
            return _dot_nt(q_all, k_cat)

        o_blocks = []
        fill_one()
        s = swa_scores(0)
        for j in range(n_blocks):
            r0 = j * WINDOW
            fill_one()
            s_next = swa_scores(j + 1) if j + 1 < n_blocks else None
            v_cat = jnp.concatenate([vlo[r0:r0 + 2 * WINDOW], vhi[r0:r0 + 2 * WINDOW]], axis=0)
            mask = band_first if (pi == 0 and j == 0) else band
            s_lo = jnp.where(mask, s[:, :2 * WINDOW], -jnp.inf)
            s_hi = jnp.where(mask, s[:, 2 * WINDOW:], -jnp.inf)
            m_lo = jnp.maximum(jnp.max(s_lo, axis=1, keepdims=True), sink_lo)
            m_hi = jnp.maximum(jnp.max(s_hi, axis=1, keepdims=True), sink_hi)
            p_lo = jnp.exp2(s_lo - m_lo)
            p_hi = jnp.exp2(s_hi - m_hi)
            d_lo = jnp.sum(p_lo, axis=1, keepdims=True) + jnp.exp2(sink_lo - m_lo)
            d_hi = jnp.sum(p_hi, axis=1, keepdims=True) + jnp.exp2(sink_hi - m_hi)
            pr = jnp.concatenate([p_lo, p_hi], axis=1).astype(BF16)
            o = _dot(pr, v_cat)
            o = o * jnp.where(low_o, 1.0 / d_lo, 1.0 / d_hi)
            op = [o[p * WINDOW:(p + 1) * WINDOW] for p in range(N_CHUNK)]
            o_blocks.append(jnp.concatenate(
                [jnp.where(low_w, op[0], _swap_halves(op[1])), jnp.where(low_w, op[2], _swap_halves(op[3])),
                 jnp.where(low_w, _swap_halves(op[0]), op[1]), jnp.where(low_w, _swap_halves(op[2]), op[3])],
                axis=1))
            s = s_next
        o_b = jnp.concatenate(o_blocks, axis=0) if len(o_blocks) > 1 else o_blocks[0]

        mq = (st["mq_raw"] * (SCALE * LOG2E)).astype(BF16)
        oc_chunks = []
        for c in range(M_CHUNK):
            s = _dot(mq[:, c * LANES:(c + 1) * LANES], kcat_t_ref[c])
            fill_one()
            s0 = s[:, :n_mem]
            s1 = s[:, n_mem:]
            p0 = jnp.exp2(s0 - jnp.max(s0, axis=1, keepdims=True))
            p1 = jnp.exp2(s1 - jnp.max(s1, axis=1, keepdims=True))
            d0 = jnp.sum(p0, axis=1, keepdims=True)
            d1 = jnp.sum(p1, axis=1, keepdims=True)
            pm = jnp.concatenate([p0, p1], axis=1).astype(BF16)
            o = _dot(pm, vcat_ref[c])
            oc_chunks.append(o * jnp.where(low_t, 1.0 / d0, 1.0 / d1))
        o_c = jnp.concatenate(oc_chunks, axis=1)
        return o_b, o_c, k[part - WINDOW:part, :], v[part - WINDOW:part, :]

    def gate_mix(pi, filled, o_b, o_c):
        out_b = _silu(jnp.concatenate([filled["ag0"], filled["ag1"]], axis=1)) * o_b
        out_c = _silu(filled["mg"]) * o_c
        u = filled["cc"] * filled["ch"]
        part = sizes[pi]
        base = SUBLANES + starts[pi]
        ubuf_ref[base:base + part, :] = u
        u1 = ubuf_ref[base - 1:base - 1 + part, :]
        u2 = ubuf_ref[base - 2:base - 2 + part, :]
        cw = convw_ref[...]
        conv = cw[0:1, :] * u2 + cw[1:2, :] * u1 + cw[2:3, :] * u
        out_a = _silu(filled["cg"]) * filled["cb"] * conv
        return jnp.concatenate([out_a, out_b, out_c], axis=1).astype(BF16), u[part - SUBLANES:part, :]

    def finish(pi, st, mix):
        part = sizes[pi]
        r0 = starts[pi]
        half = part // 2
        y0 = _dot(mix[:half], wout_ref[...])
        y1 = _dot(mix[half:], wout_ref[...])
        xo_ref[r0:r0 + half, :] = st["x"][:half] + _rmsnorm(y0, g_post)
        xo_ref[r0 + half:r0 + part, :] = st["x"][half:] + _rmsnorm(y1, g_post)

    filler_cols = [("cg", OFF_CG), ("cb", OFF_CB), ("cc", OFF_CC), ("ch", OFF_CH),
                   ("ag0", OFF_AG), ("ag1", OFF_AG + 2 * LANES), ("mg", OFF_MG)]

    states = [start(0)]
    k_prev, v_prev = kprev_ref[...], vprev_ref[...]
    u_tail = None
    for pi in range(len(sizes)):
        st = states[pi]
        pending = list(filler_cols)
        filled = {}

        def fill_one():
            if pending:
                name, off = pending.pop(0)
                filled[name] = st["proj"](off, 2 * LANES)

        if pi + 1 < len(sizes):
            states.append(start(pi + 1))
        o_b, o_c, k_prev, v_prev = mix_part(pi, st, k_prev, v_prev, fill_one)
        while pending:
            fill_one()
        mix, u_tail = gate_mix(pi, filled, o_b, o_c)
        finish(pi, st, mix)

    kprev_ref[...] = k_prev
    vprev_ref[...] = v_prev
    ubuf_ref[0:SUBLANES, :] = u_tail
    convlast_ref[...] = u_tail

    @pl.when(i == n_tiles - 1)
    def _():
        klast_ref[...] = kprev_ref[...].T
        vlast_ref[...] = vprev_ref[...].T


def _prompt_layer(l, x, sinks, norm_pre, norm_post, w_in_bf, conv_w, kcat_t, vcat, w_out_bf, w_in, w_out, n_mem):
    batch, seq, _ = x.shape
    tile = PROMPT_TILE
    nt = seq // tile
    steps = batch * nt
    chunk = D_MODEL // steps
    assert chunk * steps == D_MODEL and chunk % (2 * SUBLANES) == 0
    next_chunk = lambda b, i: (l + 1, b * nt + i, 0)
    casts = []
    if l + 1 < DEPTH:
        casts += [(w_in, (None, chunk, IN_DIM), next_chunk, w_in_bf.shape, w_in_bf),
                  (w_out, (None, chunk, D_MODEL), next_chunk, w_out_bf.shape, w_out_bf)]
    in_specs = [
        pl.BlockSpec(memory_space=pltpu.SMEM),
        pl.BlockSpec((None, tile, D_MODEL), lambda b, i: (b, i, 0)),
        pl.BlockSpec((DEPTH, D_MODEL), lambda b, i: (0, 0)),
        pl.BlockSpec((DEPTH, D_MODEL), lambda b, i: (0, 0)),
        pl.BlockSpec((None, D_MODEL, IN_DIM), lambda b, i: (l, 0, 0)),
        pl.BlockSpec((None, CONV_W, CONV_DIM), lambda b, i: (l, 0, 0)),
        pl.BlockSpec((None, None, M_CHUNK, LANES, 2 * n_mem), lambda b, i: (l, b, 0, 0, 0)),
        pl.BlockSpec((None, None, M_CHUNK, 2 * n_mem, LANES), lambda b, i: (l, b, 0, 0, 0)),
        pl.BlockSpec((None, MIX_DIM, D_MODEL), lambda b, i: (l, 0, 0)),
    ]
    out_specs = [
        pl.BlockSpec((None, tile, D_MODEL), lambda b, i: (b, i, 0)),
        pl.BlockSpec((None, KV_DIM, WINDOW), lambda b, i: (b, 0, 0)),
        pl.BlockSpec((None, KV_DIM, WINDOW), lambda b, i: (b, 0, 0)),
        pl.BlockSpec((None, SUBLANES, CONV_DIM), lambda b, i: (b, 0, 0)),
    ]
    out_shape = [
        jax.ShapeDtypeStruct((batch, seq, D_MODEL), F32),
        jax.ShapeDtypeStruct((batch, KV_DIM, WINDOW), F32),
        jax.ShapeDtypeStruct((batch, KV_DIM, WINDOW), F32),
        jax.ShapeDtypeStruct((batch, SUBLANES, CONV_DIM), F32),
    ]
    operands = [sinks, x, norm_pre, norm_post, w_in_bf, conv_w, kcat_t, vcat, w_out_bf]
    assert len(operands) == N_MAIN_IN and len(out_specs) == N_MAIN_OUT
    aliases = {}
    for n, (src, block, index_map, dst_shape, dst) in enumerate(casts):
        operands.append(src)
        in_specs.append(pl.BlockSpec(block, index_map))
        out_specs.append(pl.BlockSpec(block, index_map))
        out_shape.append(jax.ShapeDtypeStruct(dst_shape, BF16))
        (dst_operand,) = [k for k, op in enumerate(operands[:N_MAIN_IN]) if op is dst]
        aliases[dst_operand] = N_MAIN_OUT + n
    outs = pl.pallas_call(
        functools.partial(_prompt_layer_kernel, layer=l, tile=tile, n_mem=n_mem, n_tiles=nt, n_cast=len(casts)),
        grid=(batch, nt),
        in_specs=in_specs,
        out_specs=out_specs,
        out_shape=out_shape,
        input_output_aliases=aliases,
        scratch_shapes=[
            pltpu.VMEM((WINDOW, KV_DIM), F32),
            pltpu.VMEM((WINDOW, KV_DIM), F32),
            pltpu.VMEM((SUBLANES + tile, CONV_DIM), F32),
        ],
        compiler_params=pltpu.CompilerParams(
            dimension_semantics=("arbitrary", "arbitrary"), vmem_limit_bytes=VMEM_LIMIT),
        name="prompt_layer",
    )(*operands)
    main, cast_out = outs[:N_MAIN_OUT], outs[N_MAIN_OUT:]
    if l + 1 < DEPTH:
        w_in_bf, w_out_bf = cast_out
    return (*main, w_in_bf, w_out_bf)


def _sample_kernel(x_ref, gpre_ref, gpost_ref, win_ref, wout_ref, convw_ref, sink_ref, convp_ref,
                   kt_ref, vt_ref, mkt_ref, mvt_ref,
                   y_ref, kto_ref, vto_ref, uo_ref,
                   xs_ref, z_ref, mix_ref, *, group, t, n_mem, n_sub):
    l = pl.program_id(0)
    p = pl.program_id(1)
    ss = pl.program_id(2)
    nrows = group * t
    prows = n_sub * nrows
    rows_p = pl.ds(pl.multiple_of(p * prows, prows), prows)
    rows_s = pl.ds(pl.multiple_of(ss * nrows, nrows), nrows)

    @pl.when(ss == 0)
    def _():
        @pl.when(l == 0)
        def _():
            xs_ref[rows_p, :] = x_ref[...]

        h = _rmsnorm(xs_ref[rows_p, :], gpre_ref[pl.ds(l, 1), :]).astype(BF16)
        z_ref[...] = _dot(h, win_ref[...])

    def zc(off, width):
        return z_ref[rows_s, off:off + width]

    def per_seq(a):
        return a.reshape(group, t, a.shape[-1])

    u2d = zc(OFF_CC, CONV_DIM) * zc(OFF_CH, CONV_DIM)
    uo_ref[...] = u2d
    u = per_seq(u2d)
    prev = convp_ref[...]
    tpos = lax.broadcasted_iota(jnp.int32, (group, t, CONV_DIM), 1)
    u1 = jnp.where(tpos >= 1, pltpu.roll(u, 1, 1), pltpu.roll(prev, 1, 1))
    u2 = jnp.where(tpos >= 2, pltpu.roll(u, 2, 1), pltpu.roll(prev, 2, 1))
    cw = convw_ref[...]
    conv = cw[0:1, :] * u2 + cw[1:2, :] * u1 + cw[2:3, :] * u
    out_a = per_seq(_silu(zc(OFF_CG, CONV_DIM)) * zc(OFF_CB, CONV_DIM)) * conv

    kt_old = kt_ref[...]
    vt_old = vt_ref[...]
    k_new = zc(OFF_K, KV_DIM)
    v_new = zc(OFF_V, KV_DIM)
    k_new_t = jnp.swapaxes(per_seq(k_new), 1, 2)
    v_new_t = jnp.swapaxes(per_seq(v_new), 1, 2)
    kto_ref[...] = pltpu.roll(jnp.concatenate([k_new_t, kt_old[:, :, t:]], axis=2), WINDOW - t, 2)
    vto_ref[...] = pltpu.roll(jnp.concatenate([v_new_t, vt_old[:, :, t:]], axis=2), WINDOW - t, 2)

    q = per_seq(zc(OFF_Q, ATTN_DIM) * SCALE)
    low_q = _low_lanes((group, t, LANES))
    pieces = []
    for c in range(N_CHUNK):
        qc = q[:, :, c * LANES:(c + 1) * LANES]
        qs = _swap_halves(qc)
        if c // CHUNKS_PER_KV == 0:
            pieces += [jnp.where(low_q, qc, 0.0), jnp.where(low_q, qs, 0.0)]
        else:
            pieces += [jnp.where(low_q, 0.0, qs), jnp.where(low_q, 0.0, qc)]
    qbd = jnp.concatenate(pieces, axis=1).astype(BF16)
    nrow = ATTN_HEADS * t
    s_old = jnp.einsum('gqd,gdk->gqk', qbd, kt_old.astype(BF16), preferred_element_type=F32)
    s_new = _dot_nt(qbd.reshape(group * nrow, KV_DIM), k_new.astype(BF16)).reshape(group, nrow, group * t)
    tq = lax.broadcasted_iota(jnp.int32, (group, nrow, WINDOW), 1) % t
    col = lax.broadcasted_iota(jnp.int32, (group, nrow, WINDOW), 2)
    first_new = lax.broadcasted_iota(jnp.int32, (group, nrow, WINDOW), 0) * t
    s_old = jnp.where(col > tq, s_old, -jnp.inf)
    s_new = jnp.where((col >= first_new) & (col <= first_new + tq), s_new, -jnp.inf)
    sink = sink_ref[...]
    m = jnp.maximum(jnp.max(jnp.maximum(s_old, s_new), axis=2, keepdims=True), sink)
    p_old = jnp.exp(s_old - m)
    p_new = jnp.exp(s_new - m)
    d = jnp.sum(p_old + p_new, axis=2, keepdims=True) + jnp.exp(sink - m)
    o = jnp.einsum('gqk,gdk->gqd', p_old.astype(BF16), vt_old.astype(BF16), preferred_element_type=F32)
    o = o + _dot(p_new.astype(BF16).reshape(group * nrow, group * t), v_new.astype(BF16)).reshape(group, nrow, KV_DIM)
    o = o * (1.0 / d)
    ob_chunks = []
    for c in range(N_CHUNK):
        o_even = o[:, 2 * c * t:(2 * c + 1) * t]
        o_odd = o[:, (2 * c + 1) * t:(2 * c + 2) * t]
        if c // CHUNKS_PER_KV == 0:
            ob_chunks.append(jnp.where(low_q, o_even, _swap_halves(o_odd)))
        else:
            ob_chunks.append(jnp.where(low_q, _swap_halves(o_even), o_odd))
    o_b = jnp.concatenate(ob_chunks, axis=2)
    out_b = per_seq(_silu(zc(OFF_AG, ATTN_DIM))) * o_b

    mq = per_seq(zc(OFF_MQ, MEM_DIM) * SCALE)
    head_of_lane = lax.broadcasted_iota(jnp.int32, (group, t, MEM_DIM), 2) // HEAD_DIM
    qm = jnp.concatenate([jnp.where(head_of_lane == hh, mq, 0.0) for hh in range(MEM_HEADS)],
                         axis=1).astype(BF16)
    s = jnp.einsum('gqd,gdk->gqk', qm, mkt_ref[...].astype(BF16), preferred_element_type=F32)
    p = jnp.exp(s - jnp.max(s, axis=2, keepdims=True))
    d = jnp.sum(p, axis=2, keepdims=True)
    o = jnp.einsum('gqk,gdk->gqd', p.astype(BF16), mvt_ref[...].astype(BF16), preferred_element_type=F32)
    o = o * (1.0 / d)
    o_c = jnp.zeros((group, t, MEM_DIM), F32)
    for hh in range(MEM_HEADS):
        o_c = jnp.where(head_of_lane == hh, o[:, hh * t:(hh + 1) * t], o_c)
    out_c = per_seq(_silu(zc(OFF_MG, MEM_DIM))) * o_c

    mix_ref[rows_s, :] = jnp.concatenate([out_a, out_b, out_c], axis=2).reshape(nrows, MIX_DIM).astype(BF16)

    @pl.when(ss == n_sub - 1)
    def _():
        y = _dot(mix_ref[...], wout_ref[...])
        x_new = xs_ref[rows_p, :] + _rmsnorm(y, gpost_ref[pl.ds(l, 1), :])
        xs_ref[rows_p, :] = x_new

        @pl.when(l == DEPTH - 1)
        def _():
            y_ref[...] = x_new


def _sample_stream(x2d, norm_pre, norm_post, w_in_bf, w_out_bf, conv_w, sink_rows, convp, cache_kt, cache_vt,
                   cache_mkt, cache_mvt, n_seq, t, n_mem):
    group = SAMPLE_GROUP
    n_sub = SAMPLE_SUBSTEPS
    n_proj = n_seq // (group * n_sub)
    nrows = group * t
    prows = n_sub * nrows
    nrow_attn = ATTN_HEADS * t
    per_layer = lambda l, p, ss: (l, 0, 0)
    per_group = lambda l, p, ss: (l, p * n_sub + ss, 0, 0)
    return pl.pallas_call(
        functools.partial(_sample_kernel, group=group, t=t, n_mem=n_mem, n_sub=n_sub),
        grid=(DEPTH, n_proj, n_sub),
        in_specs=[
            pl.BlockSpec((prows, D_MODEL), lambda l, p, ss: (jnp.where(l == 0, p, n_proj - 1), 0)),
            pl.BlockSpec((DEPTH, D_MODEL), lambda l, p, ss: (0, 0)),
            pl.BlockSpec((DEPTH, D_MODEL), lambda l, p, ss: (0, 0)),
            pl.BlockSpec((None, D_MODEL, IN_DIM), per_layer),
            pl.BlockSpec((None, MIX_DIM, D_MODEL), per_layer),
            pl.BlockSpec((None, CONV_W, CONV_DIM), per_layer),
            pl.BlockSpec((None, nrow_attn, 1), per_layer),
            pl.BlockSpec((None, group, t, CONV_DIM), per_group),
            pl.BlockSpec((None, group, KV_DIM, WINDOW), per_group),
            pl.BlockSpec((None, group, KV_DIM, WINDOW), per_group),
            pl.BlockSpec((None, group, MEM_DIM, n_mem), per_group),
            pl.BlockSpec((None, group, MEM_DIM, n_mem), per_group),
        ],
        out_specs=[
            pl.BlockSpec((prows, D_MODEL), lambda l, p, ss: (jnp.where(l == DEPTH - 1, p, 0), 0)),
            pl.BlockSpec((None, group, KV_DIM, WINDOW), per_group),
            pl.BlockSpec((None, group, KV_DIM, WINDOW), per_group),
            pl.BlockSpec((None, nrows, CONV_DIM), lambda l, p, ss: (l, p * n_sub + ss, 0)),
        ],
        out_shape=[
            jax.ShapeDtypeStruct((n_seq * t, D_MODEL), F32),
            jax.ShapeDtypeStruct((DEPTH, n_seq, KV_DIM, WINDOW), F32),
            jax.ShapeDtypeStruct((DEPTH, n_seq, KV_DIM, WINDOW), F32),
            jax.ShapeDtypeStruct((DEPTH, n_seq * t, CONV_DIM), F32),
        ],
        scratch_shapes=[
            pltpu.VMEM((n_seq * t, D_MODEL), F32),
            pltpu.VMEM((prows, IN_DIM), F32),
            pltpu.VMEM((prows, MIX_DIM), BF16),
        ],
        compiler_params=pltpu.CompilerParams(
            dimension_semantics=("arbitrary", "arbitrary", "arbitrary"), vmem_limit_bytes=VMEM_LIMIT),
        name="sample_stream",
    )(x2d, norm_pre, norm_post, w_in_bf, w_out_bf, conv_w, sink_rows, convp, cache_kt, cache_vt, cache_mkt, cache_mvt)


def _keys_minor(a):
    lead = a.shape[:-3]
    n, heads, hd = a.shape[-3:]
    nd = a.ndim
    perm = tuple(range(nd - 3)) + (nd - 2, nd - 1, nd - 3)
    return jnp.transpose(a, perm).reshape(*lead, heads * hd, n)


def _keys_major(a, heads):
    lead = a.shape[:-2]
    n = a.shape[-1]
    a = a.reshape(*lead, heads, HEAD_DIM, n)
    nd = a.ndim
    perm = tuple(range(nd - 3)) + (nd - 1, nd - 3, nd - 2)
    return jnp.transpose(a, perm)


def kernel(x_prompt, x_sample, mem_prompt, cache_win_k, cache_win_v, state_conv, cache_mem_k, cache_mem_v,
           norm_pre, norm_post, norm_mem, w_in, conv_w, attn_sinks, w_mem_kv, w_out):
    batch, seq, _ = x_prompt.shape
    n_seq, t, _ = x_sample.shape
    n_mem = mem_prompt.shape[1]
    assert seq % PROMPT_TILE == 0 and sum(PROMPT_PART_ROWS) == PROMPT_TILE
    assert all(rows % WINDOW == 0 for rows in PROMPT_PART_ROWS)
    assert n_seq % (SAMPLE_GROUP * SAMPLE_SUBSTEPS) == 0 and t == SUBLANES

    sink_rows = jnp.repeat(attn_sinks, t, axis=1)[:, :, None]

    mkt, mvt, kcat_t, vcat, w_in_bf, w_out_bf = _memkv(
        mem_prompt.reshape(batch * n_mem, D_MODEL), norm_mem, w_mem_kv, w_in, w_out, batch, n_mem)

    xp = x_prompt
    ktp, vtp, cvp = [], [], []
    for l in range(DEPTH):
        xp, k_p, v_p, c_p, w_in_bf, w_out_bf = _prompt_layer(
            l, xp, attn_sinks, norm_pre, norm_post, w_in_bf, conv_w, kcat_t, vcat, w_out_bf, w_in, w_out, n_mem)
        ktp.append(k_p)
        vtp.append(v_p)
        cvp.append(c_p[:, SUBLANES - CONV_BUF:, :])

    convp = jnp.pad(state_conv, ((0, 0), (0, 0), (t - CONV_BUF, 0), (0, 0)))
    ys, kts, vts, us = _sample_stream(
        x_sample.reshape(n_seq * t, D_MODEL), norm_pre, norm_post, w_in_bf, w_out_bf, conv_w, sink_rows, convp,
        _keys_minor(cache_win_k), _keys_minor(cache_win_v), _keys_minor(cache_mem_k), _keys_minor(cache_mem_v),
        n_seq, t, n_mem)

    return (xp,
            ys.reshape(n_seq, t, D_MODEL),
            _keys_major(jnp.stack(ktp), KV_HEADS),
            _keys_major(jnp.stack(vtp), KV_HEADS),
            jnp.stack(cvp),
            _keys_major(mkt, MEM_HEADS),
            _keys_major(mvt, MEM_HEADS),
            _keys_major(kts, KV_HEADS),
            _keys_major(vts, KV_HEADS),
            us.reshape(DEPTH, n_seq, t, CONV_DIM)[:, :, t - CONV_BUF:, :])
```

```python
import functools

import jax
import jax.numpy as jnp
from jax import lax
from jax.experimental import pallas as pl
from jax.experimental.pallas import tpu as pltpu

D_MODEL = 1024
DEPTH = 4
HEAD_DIM = 64
ATTN_HEADS = 8
KV_HEADS = 2
ATTN_DIM = ATTN_HEADS * HEAD_DIM
KV_DIM = KV_HEADS * HEAD_DIM
WINDOW = 128
MEM_HEADS = 4
MEM_DIM = MEM_HEADS * HEAD_DIM
CONV_DIM = 256
CONV_W = 3
CONV_BUF = CONV_W - 1
MIX_DIM = CONV_DIM + ATTN_DIM + MEM_DIM
IN_DIM = 4 * CONV_DIM + 2 * ATTN_DIM + 2 * KV_DIM + 2 * MEM_DIM
RMS_EPS = 1e-6
SCALE = HEAD_DIM ** -0.5
LOG2E = 1.4426950408889634

LANES = 128
SUBLANES = 8
VMEM_LIMIT = 56 * 1024 * 1024

OFF_CB, OFF_CC, OFF_CH, OFF_CG = 0, CONV_DIM, 2 * CONV_DIM, 3 * CONV_DIM
OFF_Q = 4 * CONV_DIM
OFF_K = OFF_Q + ATTN_DIM
OFF_V = OFF_K + KV_DIM
OFF_AG = OFF_V + KV_DIM
OFF_MQ = OFF_AG + ATTN_DIM
OFF_MG = OFF_MQ + MEM_DIM
assert OFF_MG + MEM_DIM == IN_DIM and OFF_V == OFF_K + LANES

N_CHUNK = ATTN_DIM // LANES
M_CHUNK = MEM_DIM // LANES
CHUNKS_PER_KV = N_CHUNK // KV_HEADS

PROMPT_TILE = 1024
PROMPT_PART_ROWS = (512, 512)
SAMPLE_GROUP = 16
SAMPLE_SUBSTEPS = 2

F32 = jnp.float32
BF16 = jnp.bfloat16


def _rmsnorm(x, g):
    r = lax.rsqrt(jnp.mean(x * x, axis=-1, keepdims=True) + RMS_EPS)
    return (x * r) * g


def _silu(x):
    return x * jax.nn.sigmoid(x)


def _dot(a, b):
    return jnp.dot(a, b, preferred_element_type=F32)


def _dot_nt(a, b):
    return lax.dot_general(a, b, (((1,), (1,)), ((), ())), preferred_element_type=F32)


def _low_lanes(shape):
    return lax.broadcasted_iota(jnp.int32, shape, len(shape) - 1) < HEAD_DIM


def _swap_halves(a):
    return pltpu.roll(a, HEAD_DIM, a.ndim - 1)


def _memkv_kernel(mem_ref, g_ref, w_ref, win_ref, wout_ref, mkt_ref, mvt_ref, kcat_t_ref, vcat_ref,
                  win_bf_ref, wout_bf_ref, *, batch, n_mem):
    win_bf_ref[0] = win_ref[...].astype(BF16)
    wout_bf_ref[0] = wout_ref[...].astype(BF16)
    win_bf_ref[1:] = jnp.zeros((DEPTH - 1,) + win_ref.shape, BF16)
    wout_bf_ref[1:] = jnp.zeros((DEPTH - 1,) + wout_ref.shape, BF16)
    h = _rmsnorm(mem_ref[...], g_ref[pl.ds(pl.program_id(0), 1), :]).astype(BF16)
    kv = _dot(h, w_ref[...].astype(BF16))
    mk = kv[:, :MEM_DIM]
    mv = kv[:, MEM_DIM:]
    low = _low_lanes((n_mem, LANES))
    for b in range(batch):
        mkt_ref[b] = mk[b * n_mem:(b + 1) * n_mem, :].T
        mvt_ref[b] = mv[b * n_mem:(b + 1) * n_mem, :].T
        for c in range(M_CHUNK):
            kc = mk[b * n_mem:(b + 1) * n_mem, c * LANES:(c + 1) * LANES]
            vc = mv[b * n_mem:(b + 1) * n_mem, c * LANES:(c + 1) * LANES]
            kcat = jnp.concatenate([jnp.where(low, kc, 0.0), jnp.where(low, 0.0, kc)], axis=0)
            vcat = jnp.concatenate([jnp.where(low, vc, 0.0), jnp.where(low, 0.0, vc)], axis=0)
            kcat_t_ref[b, c] = kcat.T.astype(BF16)
            vcat_ref[b, c] = vcat.astype(BF16)


def _memkv(mem2d, norm_mem, w_mem_kv, w_in, w_out, batch, n_mem):
    rows = batch * n_mem
    chunk = D_MODEL // DEPTH
    return pl.pallas_call(
        functools.partial(_memkv_kernel, batch=batch, n_mem=n_mem),
        grid=(DEPTH,),
        in_specs=[
            pl.BlockSpec((rows, D_MODEL), lambda l: (0, 0)),
            pl.BlockSpec((DEPTH, D_MODEL), lambda l: (0, 0)),
            pl.BlockSpec((None, D_MODEL, 2 * MEM_DIM), lambda l: (l, 0, 0)),
            pl.BlockSpec((None, chunk, IN_DIM), lambda l: (0, l, 0)),
            pl.BlockSpec((None, chunk, D_MODEL), lambda l: (0, l, 0)),
        ],
        out_specs=[
            pl.BlockSpec((None, batch, MEM_DIM, n_mem), lambda l: (l, 0, 0, 0)),
            pl.BlockSpec((None, batch, MEM_DIM, n_mem), lambda l: (l, 0, 0, 0)),
            pl.BlockSpec((None, batch, M_CHUNK, LANES, 2 * n_mem), lambda l: (l, 0, 0, 0, 0)),
            pl.BlockSpec((None, batch, M_CHUNK, 2 * n_mem, LANES), lambda l: (l, 0, 0, 0, 0)),
            pl.BlockSpec((DEPTH, chunk, IN_DIM), lambda l: (0, l, 0)),
            pl.BlockSpec((DEPTH, chunk, D_MODEL), lambda l: (0, l, 0)),
        ],
        out_shape=[
            jax.ShapeDtypeStruct((DEPTH, batch, MEM_DIM, n_mem), F32),
            jax.ShapeDtypeStruct((DEPTH, batch, MEM_DIM, n_mem), F32),
            jax.ShapeDtypeStruct((DEPTH, batch, M_CHUNK, LANES, 2 * n_mem), BF16),
            jax.ShapeDtypeStruct((DEPTH, batch, M_CHUNK, 2 * n_mem, LANES), BF16),
            jax.ShapeDtypeStruct((DEPTH, D_MODEL, IN_DIM), BF16),
            jax.ShapeDtypeStruct((DEPTH, MIX_DIM, D_MODEL), BF16),
        ],
        compiler_params=pltpu.CompilerParams(dimension_semantics=("arbitrary",)),
        name="memkv",
    )(mem2d, norm_mem, w_mem_kv, w_in, w_out)


N_MAIN_IN, N_MAIN_OUT = 9, 4


def _prompt_layer_kernel(*refs, layer, tile, n_mem, n_tiles, n_cast):
    main_in = refs[:N_MAIN_IN]
    cast_in = refs[N_MAIN_IN:N_MAIN_IN + n_cast]
    outs = refs[N_MAIN_IN + n_cast:]
    main_out, cast_out, scratch = outs[:N_MAIN_OUT], outs[N_MAIN_OUT:N_MAIN_OUT + n_cast], outs[N_MAIN_OUT + n_cast:]
    for src, dst in zip(cast_in, cast_out):
        dst[...] = src[...].astype(BF16)
    _prompt_tile(*main_in, *main_out, *scratch, layer=layer, tile=tile, n_mem=n_mem, n_tiles=n_tiles)


def _prompt_tile(sink_ref, x_ref, gpre_ref, gpost_ref, win_ref, convw_ref, kcat_t_ref,
                 vcat_ref, wout_ref,
                 xo_ref, klast_ref, vlast_ref, convlast_ref,
                 kprev_ref, vprev_ref, ubuf_ref, *, layer, tile, n_mem, n_tiles):
    i = pl.program_id(1)
    sizes = PROMPT_PART_ROWS
    starts = [sum(sizes[:p]) for p in range(len(sizes))]
    g_pre = gpre_ref[layer:layer + 1, :]
    g_post = gpost_ref[layer:layer + 1, :]

    @pl.when(i == 0)
    def _():
        kprev_ref[...] = jnp.zeros_like(kprev_ref)
        vprev_ref[...] = jnp.zeros_like(vprev_ref)
        ubuf_ref[0:SUBLANES, :] = jnp.zeros((SUBLANES, CONV_DIM), F32)

    rows = N_CHUNK * WINDOW
    qpos = lax.broadcasted_iota(jnp.int32, (rows, 2 * WINDOW), 0) % WINDOW
    kpos = lax.broadcasted_iota(jnp.int32, (rows, 2 * WINDOW), 1)
    band = (kpos > qpos) & (kpos <= qpos + WINDOW)
    band_first = band & ((kpos >= WINDOW) | (i > 0))
    in_hi_keys = lax.broadcasted_iota(jnp.int32, (4 * WINDOW, LANES), 0) >= 2 * WINDOW
    in_hi_lanes = lax.broadcasted_iota(jnp.int32, (4 * WINDOW, LANES), 1) >= HEAD_DIM
    row_sum_cols = jnp.where(in_hi_keys == in_hi_lanes, 1.0, 0.0).astype(BF16)
    chunk_of_row = lax.broadcasted_iota(jnp.int32, (rows, 1), 0) // WINDOW
    sink_lo = jnp.zeros((rows, 1), F32)
    sink_hi = jnp.zeros((rows, 1), F32)
    for p in range(N_CHUNK):
        sink_lo = jnp.where(chunk_of_row == p, sink_ref[layer, p] * LOG2E, sink_lo)
        sink_hi = jnp.where(chunk_of_row == p, sink_ref[layer, N_CHUNK + p] * LOG2E, sink_hi)
    low_o = _low_lanes((rows, LANES))
    low_w = _low_lanes((WINDOW, LANES))

    def start(pi):
        x = x_ref[starts[pi]:starts[pi] + sizes[pi], :]
        h = _rmsnorm(x, g_pre).astype(BF16)
        proj = lambda off, width: _dot(h, win_ref[:, off:off + width])
        return dict(x=x, proj=proj, q_raw=proj(OFF_Q, ATTN_DIM), kv=proj(OFF_K, 2 * KV_DIM),
                    mq_raw=proj(OFF_MQ, MEM_DIM))

    def mix_part(pi, st, k_prev, v_prev, fill_one):
        part = sizes[pi]
        n_blocks = part // WINDOW
        low_t = _low_lanes((part, LANES))
        low_kv = _low_lanes((WINDOW + part, KV_DIM))
        k = st["kv"][:, :KV_DIM]
        v = st["kv"][:, KV_DIM:]
        kfull = jnp.concatenate([k_prev, k], axis=0)
        vfull = jnp.concatenate([v_prev, v], axis=0)
        klo = jnp.where(low_kv, kfull, 0.0).astype(BF16)
        khi = jnp.where(low_kv, 0.0, kfull).astype(BF16)
        vlo = jnp.where(low_kv, vfull, 0.0).astype(BF16)
        vhi = jnp.where(low_kv, 0.0, vfull).astype(BF16)

        q = st["q_raw"] * (SCALE * LOG2E)
        nat = [q[:, c * LANES:(c + 1) * LANES] for c in range(N_CHUNK)]
        qp = []
        for p in range(N_CHUNK):
            a, b = nat[p // 2], nat[CHUNKS_PER_KV + p // 2]
            pair = jnp.where(low_t, a, _swap_halves(b)) if p % 2 == 0 else jnp.where(low_t, _swap_halves(a), b)
            qp.append(pair.astype(BF16))

        def swa_scores(j):
            r0 = j * WINDOW
            q_all = jnp.concatenate([qp[p][r0:r0 + WINDOW] for p in range(N_CHUNK)], axis=0)
            k_cat = jnp.concatenate([klo[r0:r0 + 2 * WINDOW], khi[r0:r0 + 2 * WINDOW]], axis=0)
            return _dot_nt(q_all, k_cat)

        o_blocks = []
        fill_one()
        s = swa_scores(0)
        for j in range(n_blocks):
            r0 = j * WINDOW
            fill_one()
            s_next = swa_scores(j + 1) if j + 1 < n_blocks else None
            v_cat = jnp.concatenate([vlo[r0:r0 + 2 * WINDOW], vhi[r0:r0 + 2 * WINDOW]], axis=0)
            mask = band_first if (pi == 0 and j == 0) else band
            s_lo = jnp.where(mask, s[:, :2 * WINDOW], -jnp.inf)
            s_hi = jnp.where(mask, s[:, 2 * WINDOW:], -jnp.inf)
            m_lo = jnp.maximum(jnp.max(s_lo, axis=1, keepdims=True), sink_lo)
            m_hi = jnp.maximum(jnp.max(s_hi, axis=1, keepdims=True), sink_hi)
            p_lo = jnp.exp2(s_lo - m_lo)
            p_hi = jnp.exp2(s_hi - m_hi)
            pr = jnp.concatenate([p_lo, p_hi], axis=1).astype(BF16)
            o = _dot(pr, jnp.concatenate([v_cat, row_sum_cols], axis=1))
            denom = o[:, LANES:] + jnp.where(low_o, jnp.exp2(sink_lo - m_lo), jnp.exp2(sink_hi - m_hi))
            o = o[:, :LANES] / denom
            op = [o[p * WINDOW:(p + 1) * WINDOW] for p in range(N_CHUNK)]
            o_blocks.append(jnp.concatenate(
                [jnp.where(low_w, op[0], _swap_halves(op[1])), jnp.where(low_w, op[2], _swap_halves(op[3])),
                 jnp.where(low_w, _swap_halves(op[0]), op[1]), jnp.where(low_w, _swap_halves(op[2]), op[3])],
                axis=1))
            s = s_next
        o_b = jnp.concatenate(o_blocks, axis=0) if len(o_blocks) > 1 else o_blocks[0]

        mq = (st["mq_raw"] * (SCALE * LOG2E)).astype(BF16)
        oc_chunks = []
        for c in range(M_CHUNK):
            s = _dot(mq[:, c * LANES:(c + 1) * LANES], kcat_t_ref[c])
            fill_one()
            s0 = s[:, :n_mem]
            s1 = s[:, n_mem:]
            p0 = jnp.exp2(s0 - jnp.max(s0, axis=1, keepdims=True))
            p1 = jnp.exp2(s1 - jnp.max(s1, axis=1, keepdims=True))
            pm = jnp.concatenate([p0, p1], axis=1).astype(BF16)
            o = _dot(pm, jnp.concatenate([vcat_ref[c], row_sum_cols], axis=1))
            oc_chunks.append(o[:, :LANES] / o[:, LANES:])
        o_c = jnp.concatenate(oc_chunks, axis=1)
        return o_b, o_c, k[part - WINDOW:part, :], v[part - WINDOW:part, :]

    def gate_mix(pi, filled, o_b, o_c):
        out_b = _silu(jnp.concatenate([filled["ag0"], filled["ag1"]], axis=1)) * o_b
        out_c = _silu(filled["mg"]) * o_c
        u = filled["cc"] * filled["ch"]
        part = sizes[pi]
        base = SUBLANES + starts[pi]
        ubuf_ref[base:base + part, :] = u
        u1 = ubuf_ref[base - 1:base - 1 + part, :]
        u2 = ubuf_ref[base - 2:base - 2 + part, :]
        cw = convw_ref[...]
        conv = cw[0:1, :] * u2 + cw[1:2, :] * u1 + cw[2:3, :] * u
        out_a = _silu(filled["cg"]) * filled["cb"] * conv
        return jnp.concatenate([out_a, out_b, out_c], axis=1).astype(BF16), u[part - SUBLANES:part, :]

    def finish(pi, st, mix):
        part = sizes[pi]
        r0 = starts[pi]
        half = part // 2
        y0 = _dot(mix[:half], wout_ref[...])
        y1 = _dot(mix[half:], wout_ref[...])
        xo_ref[r0:r0 + half, :] = st["x"][:half] + _rmsnorm(y0, g_post)
        xo_ref[r0 + half:r0 + part, :] = st["x"][half:] + _rmsnorm(y1, g_post)

    filler_cols = [("cg", OFF_CG), ("cb", OFF_CB), ("cc", OFF_CC), ("ch", OFF_CH),
                   ("ag0", OFF_AG), ("ag1", OFF_AG + 2 * LANES), ("mg", OFF_MG)]

    states = [start(0)]
    k_prev, v_prev = kprev_ref[...], vprev_ref[...]
    u_tail = None
    for pi in range(len(sizes)):
        st = states[pi]
        pending = list(filler_cols)
        filled = {}

        def fill_one():
            if pending:
                name, off = pending.pop(0)
                filled[name] = st["proj"](off, 2 * LANES)

        if pi + 1 < len(sizes):
            states.append(start(pi + 1))
        o_b, o_c, k_prev, v_prev = mix_part(pi, st, k_prev, v_prev, fill_one)
        while pending:
            fill_one()
        mix, u_tail = gate_mix(pi, filled, o_b, o_c)
        finish(pi, st, mix)

    kprev_ref[...] = k_prev
    vprev_ref[...] = v_prev
    ubuf_ref[0:SUBLANES, :] = u_tail
    convlast_ref[...] = u_tail

    @pl.when(i == n_tiles - 1)
    def _():
        klast_ref[...] = kprev_ref[...].T
        vlast_ref[...] = vprev_ref[...].T


def _prompt_layer(l, x, sinks, norm_pre, norm_post, w_in_bf, conv_w, kcat_t, vcat, w_out_bf, w_in, w_out, n_mem):
    batch, seq, _ = x.shape
    tile = PROMPT_TILE
    nt = seq // tile
    steps = batch * nt
    chunk = D_MODEL // steps
    assert chunk * steps == D_MODEL and chunk % (2 * SUBLANES) == 0
    next_chunk = lambda b, i: (l + 1, b * nt + i, 0)
    casts = []
    if l + 1 < DEPTH:
        casts += [(w_in, (None, chunk, IN_DIM), next_chunk, w_in_bf.shape, w_in_bf),
                  (w_out, (None, chunk, D_MODEL), next_chunk, w_out_bf.shape, w_out_bf)]
    in_specs = [
        pl.BlockSpec(memory_space=pltpu.SMEM),
        pl.BlockSpec((None, tile, D_MODEL), lambda b, i: (b, i, 0)),
        pl.BlockSpec((DEPTH, D_MODEL), lambda b, i: (0, 0)),
        pl.BlockSpec((DEPTH, D_MODEL), lambda b, i: (0, 0)),
        pl.BlockSpec((None, D_MODEL, IN_DIM), lambda b, i: (l, 0, 0)),
        pl.BlockSpec((None, CONV_W, CONV_DIM), lambda b, i: (l, 0, 0)),
        pl.BlockSpec((None, None, M_CHUNK, LANES, 2 * n_mem), lambda b, i: (l, b, 0, 0, 0)),
        pl.BlockSpec((None, None, M_CHUNK, 2 * n_mem, LANES), lambda b, i: (l, b, 0, 0, 0)),
        pl.BlockSpec((None, MIX_DIM, D_MODEL), lambda b, i: (l, 0, 0)),
    ]
    out_specs = [
        pl.BlockSpec((None, tile, D_MODEL), lambda b, i: (b, i, 0)),
        pl.BlockSpec((None, KV_DIM, WINDOW), lambda b, i: (b, 0, 0)),
        pl.BlockSpec((None, KV_DIM, WINDOW), lambda b, i: (b, 0, 0)),
        pl.BlockSpec((None, SUBLANES, CONV_DIM), lambda b, i: (b, 0, 0)),
    ]
    out_shape = [
        jax.ShapeDtypeStruct((batch, seq, D_MODEL), F32),
        jax.ShapeDtypeStruct((batch, KV_DIM, WINDOW), F32),
        jax.ShapeDtypeStruct((batch, KV_DIM, WINDOW), F32),
        jax.ShapeDtypeStruct((batch, SUBLANES, CONV_DIM), F32),
    ]
    operands = [sinks, x, norm_pre, norm_post, w_in_bf, conv_w, kcat_t, vcat, w_out_bf]
    assert len(operands) == N_MAIN_IN and len(out_specs) == N_MAIN_OUT
    aliases = {}
    for n, (src, block, index_map, dst_shape, dst) in enumerate(casts):
        operands.append(src)
        in_specs.append(pl.BlockSpec(block, index_map))
        out_specs.append(pl.BlockSpec(block, index_map))
        out_shape.append(jax.ShapeDtypeStruct(dst_shape, BF16))
        (dst_operand,) = [k for k, op in enumerate(operands[:N_MAIN_IN]) if op is dst]
        aliases[dst_operand] = N_MAIN_OUT + n
    outs = pl.pallas_call(
        functools.partial(_prompt_layer_kernel, layer=l, tile=tile, n_mem=n_mem, n_tiles=nt, n_cast=len(casts)),
        grid=(batch, nt),
        in_specs=in_specs,
        out_specs=out_specs,
        out_shape=out_shape,
        input_output_aliases=aliases,
        scratch_shapes=[
            pltpu.VMEM((WINDOW, KV_DIM), F32),
            pltpu.VMEM((WINDOW, KV_DIM), F32),
            pltpu.VMEM((SUBLANES + tile, CONV_DIM), F32),
        ],
        compiler_params=pltpu.CompilerParams(
            dimension_semantics=("arbitrary", "arbitrary"), vmem_limit_bytes=VMEM_LIMIT),
        name="prompt_layer",
    )(*operands)
    main, cast_out = outs[:N_MAIN_OUT], outs[N_MAIN_OUT:]
    if l + 1 < DEPTH:
        w_in_bf, w_out_bf = cast_out
    return (*main, w_in_bf, w_out_bf)


def _sample_kernel(x_ref, gpre_ref, gpost_ref, win_ref, wout_ref, convw_ref, sink_ref, convp_ref,
                   kt_ref, vt_ref, mkt_ref, mvt_ref,
                   y_ref, kto_ref, vto_ref, uo_ref,
                   xs_ref, z_ref, mix_ref, *, group, t, n_mem, n_sub):
    l = pl.program_id(0)
    p = pl.program_id(1)
    ss = pl.program_id(2)
    nrows = group * t
    prows = n_sub * nrows
    rows_p = pl.ds(pl.multiple_of(p * prows, prows), prows)
    rows_s = pl.ds(pl.multiple_of(ss * nrows, nrows), nrows)

    @pl.when(ss == 0)
    def _():
        @pl.when(l == 0)
        def _():
            xs_ref[rows_p, :] = x_ref[...]

        h = _rmsnorm(xs_ref[rows_p, :], gpre_ref[pl.ds(l, 1), :]).astype(BF16)
        z_ref[...] = _dot(h, win_ref[...])

    def zc(off, width):
        return z_ref[rows_s, off:off + width]

    def per_seq(a):
        return a.reshape(group, t, a.shape[-1])

    u2d = zc(OFF_CC, CONV_DIM) * zc(OFF_CH, CONV_DIM)
    uo_ref[...] = u2d
    u = per_seq(u2d)
    prev = convp_ref[...]
    tpos = lax.broadcasted_iota(jnp.int32, (group, t, CONV_DIM), 1)
    u1 = jnp.where(tpos >= 1, pltpu.roll(u, 1, 1), pltpu.roll(prev, 1, 1))
    u2 = jnp.where(tpos >= 2, pltpu.roll(u, 2, 1), pltpu.roll(prev, 2, 1))
    cw = convw_ref[...]
    conv = cw[0:1, :] * u2 + cw[1:2, :] * u1 + cw[2:3, :] * u
    out_a = per_seq(_silu(zc(OFF_CG, CONV_DIM)) * zc(OFF_CB, CONV_DIM)) * conv

    kt_old = kt_ref[...]
    vt_old = vt_ref[...]
    k_new = zc(OFF_K, KV_DIM)
    v_new = zc(OFF_V, KV_DIM)
    k_new_t = jnp.swapaxes(per_seq(k_new), 1, 2)
    v_new_t = jnp.swapaxes(per_seq(v_new), 1, 2)
    kto_ref[...] = pltpu.roll(jnp.concatenate([k_new_t, kt_old[:, :, t:]], axis=2), WINDOW - t, 2)
    vto_ref[...] = pltpu.roll(jnp.concatenate([v_new_t, vt_old[:, :, t:]], axis=2), WINDOW - t, 2)

    q = per_seq(zc(OFF_Q, ATTN_DIM) * SCALE)
    low_q = _low_lanes((group, t, LANES))
    pieces = []
    for c in range(N_CHUNK):
        qc = q[:, :, c * LANES:(c + 1) * LANES]
        qs = _swap_halves(qc)
        if c // CHUNKS_PER_KV == 0:
            pieces += [jnp.where(low_q, qc, 0.0), jnp.where(low_q, qs, 0.0)]
        else:
            pieces += [jnp.where(low_q, 0.0, qs), jnp.where(low_q, 0.0, qc)]
    qbd = jnp.concatenate(pieces, axis=1).astype(BF16)
    nrow = ATTN_HEADS * t
    s_old = jnp.einsum('gqd,gdk->gqk', qbd, kt_old.astype(BF16), preferred_element_type=F32)
    s_new = _dot_nt(qbd.reshape(group * nrow, KV_DIM), k_new.astype(BF16)).reshape(group, nrow, group * t)
    tq = lax.broadcasted_iota(jnp.int32, (group, nrow, WINDOW), 1) % t
    col = lax.broadcasted_iota(jnp.int32, (group, nrow, WINDOW), 2)
    first_new = lax.broadcasted_iota(jnp.int32, (group, nrow, WINDOW), 0) * t
    s_old = jnp.where(col > tq, s_old, -jnp.inf)
    s_new = jnp.where((col >= first_new) & (col <= first_new + tq), s_new, -jnp.inf)
    sink = sink_ref[...]
    m = jnp.maximum(jnp.max(jnp.maximum(s_old, s_new), axis=2, keepdims=True), sink)
    p_old = jnp.exp(s_old - m)
    p_new = jnp.exp(s_new - m)
    d = jnp.sum(p_old + p_new, axis=2, keepdims=True) + jnp.exp(sink - m)
    o = jnp.einsum('gqk,gdk->gqd', p_old.astype(BF16), vt_old.astype(BF16), preferred_element_type=F32)
    o = o + _dot(p_new.astype(BF16).reshape(group * nrow, group * t), v_new.astype(BF16)).reshape(group, nrow, KV_DIM)
    o = o * (1.0 / d)
    ob_chunks = []
    for c in range(N_CHUNK):
        o_even = o[:, 2 * c * t:(2 * c + 1) * t]
        o_odd = o[:, (2 * c + 1) * t:(2 * c + 2) * t]
        if c // CHUNKS_PER_KV == 0:
            ob_chunks.append(jnp.where(low_q, o_even, _swap_halves(o_odd)))
        else:
            ob_chunks.append(jnp.where(low_q, _swap_halves(o_even), o_odd))
    o_b = jnp.concatenate(ob_chunks, axis=2)
    out_b = per_seq(_silu(zc(OFF_AG, ATTN_DIM))) * o_b

    mq = per_seq(zc(OFF_MQ, MEM_DIM) * SCALE)
    head_of_lane = lax.broadcasted_iota(jnp.int32, (group, t, MEM_DIM), 2) // HEAD_DIM
    qm = jnp.concatenate([jnp.where(head_of_lane == hh, mq, 0.0) for hh in range(MEM_HEADS)],
                         axis=1).astype(BF16)
    s = jnp.einsum('gqd,gdk->gqk', qm, mkt_ref[...].astype(BF16), preferred_element_type=F32)
    p = jnp.exp(s - jnp.max(s, axis=2, keepdims=True))
    d = jnp.sum(p, axis=2, keepdims=True)
    o = jnp.einsum('gqk,gdk->gqd', p.astype(BF16), mvt_ref[...].astype(BF16), preferred_element_type=F32)
    o = o * (1.0 / d)
    o_c = jnp.zeros((group, t, MEM_DIM), F32)
    for hh in range(MEM_HEADS):
        o_c = jnp.where(head_of_lane == hh, o[:, hh * t:(hh + 1) * t], o_c)
    out_c = per_seq(_silu(zc(OFF_MG, MEM_DIM))) * o_c

    mix_ref[rows_s, :] = jnp.concatenate([out_a, out_b, out_c], axis=2).reshape(nrows, MIX_DIM).astype(BF16)

    @pl.when(ss == n_sub - 1)
    def _():
        y = _dot(mix_ref[...], wout_ref[...])
        x_new = xs_ref[rows_p, :] + _rmsnorm(y, gpost_ref[pl.ds(l, 1), :])
        xs_ref[rows_p, :] = x_new

        @pl.when(l == DEPTH - 1)
        def _():
            y_ref[...] = x_new


def _sample_stream(x2d, norm_pre, norm_post, w_in_bf, w_out_bf, conv_w, sink_rows, convp, cache_kt, cache_vt,
                   cache_mkt, cache_mvt, n_seq, t, n_mem):
    group = SAMPLE_GROUP
    n_sub = SAMPLE_SUBSTEPS
    n_proj = n_seq // (group * n_sub)
    nrows = group * t
    prows = n_sub * nrows
    nrow_attn = ATTN_HEADS * t
    per_layer = lambda l, p, ss: (l, 0, 0)
    per_group = lambda l, p, ss: (l, p * n_sub + ss, 0, 0)
    return pl.pallas_call(
        functools.partial(_sample_kernel, group=group, t=t, n_mem=n_mem, n_sub=n_sub),
        grid=(DEPTH, n_proj, n_sub),
        in_specs=[
            pl.BlockSpec((prows, D_MODEL), lambda l, p, ss: (jnp.where(l == 0, p, n_proj - 1), 0)),
            pl.BlockSpec((DEPTH, D_MODEL), lambda l, p, ss: (0, 0)),
            pl.BlockSpec((DEPTH, D_MODEL), lambda l, p, ss: (0, 0)),
            pl.BlockSpec((None, D_MODEL, IN_DIM), per_layer),
            pl.BlockSpec((None, MIX_DIM, D_MODEL), per_layer),
            pl.BlockSpec((None, CONV_W, CONV_DIM), per_layer),
            pl.BlockSpec((None, nrow_attn, 1), per_layer),
            pl.BlockSpec((None, group, t, CONV_DIM), per_group),
            pl.BlockSpec((None, group, KV_DIM, WINDOW), per_group),
            pl.BlockSpec((None, group, KV_DIM, WINDOW), per_group),
            pl.BlockSpec((None, group, MEM_DIM, n_mem), per_group),
            pl.BlockSpec((None, group, MEM_DIM, n_mem), per_group),
        ],
        out_specs=[
            pl.BlockSpec((prows, D_MODEL), lambda l, p, ss: (jnp.where(l == DEPTH - 1, p, 0), 0)),
            pl.BlockSpec((None, group, KV_DIM, WINDOW), per_group),
            pl.BlockSpec((None, group, KV_DIM, WINDOW), per_group),
            pl.BlockSpec((None, nrows, CONV_DIM), lambda l, p, ss: (l, p * n_sub + ss, 0)),
        ],
        out_shape=[
            jax.ShapeDtypeStruct((n_seq * t, D_MODEL), F32),
            jax.ShapeDtypeStruct((DEPTH, n_seq, KV_DIM, WINDOW), F32),
            jax.ShapeDtypeStruct((DEPTH, n_seq, KV_DIM, WINDOW), F32),
            jax.ShapeDtypeStruct((DEPTH, n_seq * t, CONV_DIM), F32),
        ],
        scratch_shapes=[
            pltpu.VMEM((n_seq * t, D_MODEL), F32),
            pltpu.VMEM((prows, IN_DIM), F32),
            pltpu.VMEM((prows, MIX_DIM), BF16),
        ],
        compiler_params=pltpu.CompilerParams(
            dimension_semantics=("arbitrary", "arbitrary", "arbitrary"), vmem_limit_bytes=VMEM_LIMIT),
        name="sample_stream",
    )(x2d, norm_pre, norm_post, w_in_bf, w_out_bf, conv_w, sink_rows, convp, cache_kt, cache_vt, cache_mkt, cache_mvt)


def _keys_minor(a):
    lead = a.shape[:-3]
    n, heads, hd = a.shape[-3:]
    nd = a.ndim
    perm = tuple(range(nd - 3)) + (nd - 2, nd - 1, nd - 3)
    return jnp.transpose(a, perm).reshape(*lead, heads * hd, n)


def _keys_major(a, heads):
    lead = a.shape[:-2]
    n = a.shape[-1]
    a = a.reshape(*lead, heads, HEAD_DIM, n)
    nd = a.ndim
    perm = tuple(range(nd - 3)) + (nd - 1, nd - 3, nd - 2)
    return jnp.transpose(a, perm)


def kernel(x_prompt, x_sample, mem_prompt, cache_win_k, cache_win_v, state_conv, cache_mem_k, cache_mem_v,
           norm_pre, norm_post, norm_mem, w_in, conv_w, attn_sinks, w_mem_kv, w_out):
    batch, seq, _ = x_prompt.shape
    n_seq, t, _ = x_sample.shape
    n_mem = mem_prompt.shape[1]
    assert seq % PROMPT_TILE == 0 and sum(PROMPT_PART_ROWS) == PROMPT_TILE
    assert all(rows % WINDOW == 0 for rows in PROMPT_PART_ROWS)
    assert n_mem == 2 * WINDOW
    assert n_seq % (SAMPLE_GROUP * SAMPLE_SUBSTEPS) == 0 and t == SUBLANES

    sink_rows = jnp.repeat(attn_sinks, t, axis=1)[:, :, None]

    mkt, mvt, kcat_t, vcat, w_in_bf, w_out_bf = _memkv(
        mem_prompt.reshape(batch * n_mem, D_MODEL), norm_mem, w_mem_kv, w_in, w_out, batch, n_mem)

    xp = x_prompt
    ktp, vtp, cvp = [], [], []
    for l in range(DEPTH):
        xp, k_p, v_p, c_p, w_in_bf, w_out_bf = _prompt_layer(
            l, xp, attn_sinks, norm_pre, norm_post, w_in_bf, conv_w, kcat_t, vcat, w_out_bf, w_in, w_out, n_mem)
        ktp.append(k_p)
        vtp.append(v_p)
        cvp.append(c_p[:, SUBLANES - CONV_BUF:, :])

    convp = jnp.pad(state_conv, ((0, 0), (0, 0), (t - CONV_BUF, 0), (0, 0)))
    ys, kts, vts, us = _sample_stream(
        x_sample.reshape(n_seq * t, D_MODEL), norm_pre, norm_post, w_in_bf, w_out_bf, conv_w, sink_rows, convp,
        _keys_minor(cache_win_k), _keys_minor(cache_win_v), _keys_minor(cache_mem_k), _keys_minor(cache_mem_v),
        n_seq, t, n_mem)

    return (xp,
            ys.reshape(n_seq, t, D_MODEL),
            _keys_major(jnp.stack(ktp), KV_HEADS),
            _keys_major(jnp.stack(vtp), KV_HEADS),
            jnp.stack(cvp),
            _keys_major(mkt, MEM_HEADS),
            _keys_major(mvt, MEM_HEADS),
            _keys_major(kts, KV_HEADS),
            _keys_major(vts, KV_HEADS),
            us.reshape(DEPTH, n_seq, t, CONV_DIM)[:, :, t - CONV_BUF:, :])
```

```python
import functools

import jax
import jax.numpy as jnp
from jax import lax
from jax.experimental import pallas as pl
from jax.experimental.pallas import tpu as pltpu

D_MODEL = 1024
DEPTH = 4
HEAD_DIM = 64
ATTN_HEADS = 8
KV_HEADS = 2
ATTN_DIM = ATTN_HEADS * HEAD_DIM
KV_DIM = KV_HEADS * HEAD_DIM
WINDOW = 128
MEM_HEADS = 4
MEM_DIM = MEM_HEADS * HEAD_DIM
CONV_DIM = 256
CONV_W = 3
CONV_BUF = CONV_W - 1
MIX_DIM = CONV_DIM + ATTN_DIM + MEM_DIM
IN_DIM = 4 * CONV_DIM + 2 * ATTN_DIM + 2 * KV_DIM + 2 * MEM_DIM
RMS_EPS = 1e-6
SCALE = HEAD_DIM ** -0.5
LOG2E = 1.4426950408889634

LANES = 128
SUBLANES = 8
VMEM_LIMIT = 56 * 1024 * 1024

OFF_CB, OFF_CC, OFF_CH, OFF_CG = 0, CONV_DIM, 2 * CONV_DIM, 3 * CONV_DIM
OFF_Q = 4 * CONV_DIM
OFF_K = OFF_Q + ATTN_DIM
OFF_V = OFF_K + KV_DIM
OFF_AG = OFF_V + KV_DIM
OFF_MQ = OFF_AG + ATTN_DIM
OFF_MG = OFF_MQ + MEM_DIM
assert OFF_MG + MEM_DIM == IN_DIM and OFF_V == OFF_K + LANES

N_CHUNK = ATTN_DIM // LANES
M_CHUNK = MEM_DIM // LANES
CHUNKS_PER_KV = N_CHUNK // KV_HEADS

PROMPT_TILE = 1024
PROMPT_PART_ROWS = (512, 512)
SAMPLE_GROUP = 8
SAMPLE_SUBSTEPS = 4
MEM_SLOTS = 3

F32 = jnp.float32
BF16 = jnp.bfloat16


def _rmsnorm(x, g):
    r = lax.rsqrt(jnp.mean(x * x, axis=-1, keepdims=True) + RMS_EPS)
    return (x * r) * g


def _silu(x):
    return x * jax.nn.sigmoid(x)


def _dot(a, b):
    return jnp.dot(a, b, preferred_element_type=F32)


def _dot_nt(a, b):
    return lax.dot_general(a, b, (((1,), (1,)), ((), ())), preferred_element_type=F32)


def _low_lanes(shape):
    return lax.broadcasted_iota(jnp.int32, shape, len(shape) - 1) < HEAD_DIM


def _swap_halves(a):
    return pltpu.roll(a, HEAD_DIM, a.ndim - 1)


def _memkv_kernel(mem_ref, g_ref, w_ref, win_ref, wout_ref, mkt_ref, mvt_ref, kcat_t_ref, vcat_ref,
                  win_bf_ref, wout_bf_ref, *, batch, n_mem):
    win_bf_ref[0] = win_ref[...].astype(BF16)
    wout_bf_ref[0] = wout_ref[...].astype(BF16)
    win_bf_ref[1:] = jnp.zeros((DEPTH - 1,) + win_ref.shape, BF16)
    wout_bf_ref[1:] = jnp.zeros((DEPTH - 1,) + wout_ref.shape, BF16)
    h = _rmsnorm(mem_ref[...], g_ref[pl.ds(pl.program_id(0), 1), :]).astype(BF16)
    kv = _dot(h, w_ref[...].astype(BF16))
    mk = kv[:, :MEM_DIM]
    mv = kv[:, MEM_DIM:]
    low = _low_lanes((n_mem, LANES))
    for b in range(batch):
        mkt_ref[b] = mk[b * n_mem:(b + 1) * n_mem, :].T
        mvt_ref[b] = mv[b * n_mem:(b + 1) * n_mem, :].T
        for c in range(M_CHUNK):
            kc = mk[b * n_mem:(b + 1) * n_mem, c * LANES:(c + 1) * LANES]
            vc = mv[b * n_mem:(b + 1) * n_mem, c * LANES:(c + 1) * LANES]
            kcat = jnp.concatenate([jnp.where(low, kc, 0.0), jnp.where(low, 0.0, kc)], axis=0)
            vcat = jnp.concatenate([jnp.where(low, vc, 0.0), jnp.where(low, 0.0, vc)], axis=0)
            kcat_t_ref[b, c] = kcat.T.astype(BF16)
            vcat_ref[b, c] = vcat.astype(BF16)


def _memkv(mem2d, norm_mem, w_mem_kv, w_in, w_out, batch, n_mem):
    rows = batch * n_mem
    chunk = D_MODEL // DEPTH
    return pl.pallas_call(
        functools.partial(_memkv_kernel, batch=batch, n_mem=n_mem),
        grid=(DEPTH,),
        in_specs=[
            pl.BlockSpec((rows, D_MODEL), lambda l: (0, 0)),
            pl.BlockSpec((DEPTH, D_MODEL), lambda l: (0, 0)),
            pl.BlockSpec((None, D_MODEL, 2 * MEM_DIM), lambda l: (l, 0, 0)),
            pl.BlockSpec((None, chunk, IN_DIM), lambda l: (0, l, 0)),
            pl.BlockSpec((None, chunk, D_MODEL), lambda l: (0, l, 0)),
        ],
        out_specs=[
            pl.BlockSpec((None, batch, MEM_DIM, n_mem), lambda l: (l, 0, 0, 0)),
            pl.BlockSpec((None, batch, MEM_DIM, n_mem), lambda l: (l, 0, 0, 0)),
            pl.BlockSpec((None, batch, M_CHUNK, LANES, 2 * n_mem), lambda l: (l, 0, 0, 0, 0)),
            pl.BlockSpec((None, batch, M_CHUNK, 2 * n_mem, LANES), lambda l: (l, 0, 0, 0, 0)),
            pl.BlockSpec((DEPTH, chunk, IN_DIM), lambda l: (0, l, 0)),
            pl.BlockSpec((DEPTH, chunk, D_MODEL), lambda l: (0, l, 0)),
        ],
        out_shape=[
            jax.ShapeDtypeStruct((DEPTH, batch, MEM_DIM, n_mem), F32),
            jax.ShapeDtypeStruct((DEPTH, batch, MEM_DIM, n_mem), F32),
            jax.ShapeDtypeStruct((DEPTH, batch, M_CHUNK, LANES, 2 * n_mem), BF16),
            jax.ShapeDtypeStruct((DEPTH, batch, M_CHUNK, 2 * n_mem, LANES), BF16),
            jax.ShapeDtypeStruct((DEPTH, D_MODEL, IN_DIM), BF16),
            jax.ShapeDtypeStruct((DEPTH, MIX_DIM, D_MODEL), BF16),
        ],
        compiler_params=pltpu.CompilerParams(dimension_semantics=("arbitrary",)),
        name="memkv",
    )(mem2d, norm_mem, w_mem_kv, w_in, w_out)


N_MAIN_IN, N_MAIN_OUT = 9, 4


def _prompt_layer_kernel(*refs, layer, tile, n_mem, n_tiles, n_cast):
    main_in = refs[:N_MAIN_IN]
    cast_in = refs[N_MAIN_IN:N_MAIN_IN + n_cast]
    outs = refs[N_MAIN_IN + n_cast:]
    main_out, cast_out, scratch = outs[:N_MAIN_OUT], outs[N_MAIN_OUT:N_MAIN_OUT + n_cast], outs[N_MAIN_OUT + n_cast:]
    for src, dst in zip(cast_in, cast_out):
        dst[...] = src[...].astype(BF16)
    _prompt_tile(*main_in, *main_out, *scratch, layer=layer, tile=tile, n_mem=n_mem, n_tiles=n_tiles)


def _prompt_tile(sink_ref, x_ref, gpre_ref, gpost_ref, win_ref, convw_ref, kcat_t_ref,
                 vcat_ref, wout_ref,
                 xo_ref, klast_ref, vlast_ref, convlast_ref,
                 kprev_ref, vprev_ref, ubuf_ref, *, layer, tile, n_mem, n_tiles):
    i = pl.program_id(1)
    sizes = PROMPT_PART_ROWS
    starts = [sum(sizes[:p]) for p in range(len(sizes))]
    g_pre = gpre_ref[layer:layer + 1, :]
    g_post = gpost_ref[layer:layer + 1, :]

    @pl.when(i == 0)
    def _():
        kprev_ref[...] = jnp.zeros_like(kprev_ref)
        vprev_ref[...] = jnp.zeros_like(vprev_ref)
        ubuf_ref[0:SUBLANES, :] = jnp.zeros((SUBLANES, CONV_DIM), F32)

    rows = N_CHUNK * WINDOW
    qpos = lax.broadcasted_iota(jnp.int32, (rows, 2 * WINDOW), 0) % WINDOW
    kpos = lax.broadcasted_iota(jnp.int32, (rows, 2 * WINDOW), 1)
    band = (kpos > qpos) & (kpos <= qpos + WINDOW)
    band_first = band & ((kpos >= WINDOW) | (i > 0))
    chunk_of_row = lax.broadcasted_iota(jnp.int32, (rows, 1), 0) // WINDOW
    sink_lo = jnp.zeros((rows, 1), F32)
    sink_hi = jnp.zeros((rows, 1), F32)
    for p in range(N_CHUNK):
        sink_lo = jnp.where(chunk_of_row == p, sink_ref[layer, p] * LOG2E, sink_lo)
        sink_hi = jnp.where(chunk_of_row == p, sink_ref[layer, N_CHUNK + p] * LOG2E, sink_hi)
    low_o = _low_lanes((rows, LANES))
    low_w = _low_lanes((WINDOW, LANES))

    def start(pi):
        x = x_ref[starts[pi]:starts[pi] + sizes[pi], :]
        h = _rmsnorm(x, g_pre).astype(BF16)
        proj = lambda off, width: _dot(h, win_ref[:, off:off + width])
        return dict(x=x, proj=proj, q_raw=proj(OFF_Q, ATTN_DIM), kv=proj(OFF_K, 2 * KV_DIM),
                    mq_raw=proj(OFF_MQ, MEM_DIM))

    def mix_part(pi, st, k_prev, v_prev, fill_one):
        part = sizes[pi]
        n_blocks = part // WINDOW
        low_t = _low_lanes((part, LANES))
        low_kv = _low_lanes((WINDOW + part, KV_DIM))
        k = st["kv"][:, :KV_DIM]
        v = st["kv"][:, KV_DIM:]
        kfull = jnp.concatenate([k_prev, k], axis=0)
        vfull = jnp.concatenate([v_prev, v], axis=0)
        klo = jnp.where(low_kv, kfull, 0.0).astype(BF16)
        khi = jnp.where(low_kv, 0.0, kfull).astype(BF16)
        vlo = jnp.where(low_kv, vfull, 0.0).astype(BF16)
        vhi = jnp.where(low_kv, 0.0, vfull).astype(BF16)

        q = st["q_raw"] * (SCALE * LOG2E)
        nat = [q[:, c * LANES:(c + 1) * LANES] for c in range(N_CHUNK)]
        qp = []
        for p in range(N_CHUNK):
            a, b = nat[p // 2], nat[CHUNKS_PER_KV + p // 2]
            pair = jnp.where(low_t, a, _swap_halves(b)) if p % 2 == 0 else jnp.where(low_t, _swap_halves(a), b)
            qp.append(pair.astype(BF16))

        def swa_scores(j):
            r0 = j * WINDOW
            q_all = jnp.concatenate([qp[p][r0:r0 + WINDOW] for p in range(N_CHUNK)], axis=0)
            k_cat = jnp.concatenate([klo[r0:r0 + 2 * WINDOW], khi[r0:r0 + 2 * WINDOW]], axis=0)
            return _dot_nt(q_all, k_cat)

        o_blocks = []
        fill_one()
        s = swa_scores(0)
        for j in range(n_blocks):
            r0 = j * WINDOW
            fill_one()
            s_next = swa_scores(j + 1) if j + 1 < n_blocks else None
            v_cat = jnp.concatenate([vlo[r0:r0 + 2 * WINDOW], vhi[r0:r0 + 2 * WINDOW]], axis=0)
            mask = band_first if (pi == 0 and j == 0) else band
            s_lo = jnp.where(mask, s[:, :2 * WINDOW], -jnp.inf)
            s_hi = jnp.where(mask, s[:, 2 * WINDOW:], -jnp.inf)
            m_lo = jnp.maximum(jnp.max(s_lo, axis=1, keepdims=True), sink_lo)
            m_hi = jnp.maximum(jnp.max(s_hi, axis=1, keepdims=True), sink_hi)
            p_lo = jnp.exp2(s_lo - m_lo)
            p_hi = jnp.exp2(s_hi - m_hi)
            d_lo = jnp.sum(p_lo, axis=1, keepdims=True) + jnp.exp2(sink_lo - m_lo)
            d_hi = jnp.sum(p_hi, axis=1, keepdims=True) + jnp.exp2(sink_hi - m_hi)
            pr = jnp.concatenate([p_lo, p_hi], axis=1).astype(BF16)
            o = _dot(pr, v_cat)
            o = o * jnp.where(low_o, 1.0 / d_lo, 1.0 / d_hi)
            op = [o[p * WINDOW:(p + 1) * WINDOW] for p in range(N_CHUNK)]
            o_blocks.append(jnp.concatenate(
                [jnp.where(low_w, op[0], _swap_halves(op[1])), jnp.where(low_w, op[2], _swap_halves(op[3])),
                 jnp.where(low_w, _swap_halves(op[0]), op[1]), jnp.where(low_w, _swap_halves(op[2]), op[3])],
                axis=1))
            s = s_next
        o_b = jnp.concatenate(o_blocks, axis=0) if len(o_blocks) > 1 else o_blocks[0]

        mq = (st["mq_raw"] * (SCALE * LOG2E)).astype(BF16)
        oc_chunks = []
        for c in range(M_CHUNK):
            s = _dot(mq[:, c * LANES:(c + 1) * LANES], kcat_t_ref[c])
            fill_one()
            s0 = s[:, :n_mem]
            s1 = s[:, n_mem:]
            p0 = jnp.exp2(s0 - jnp.max(s0, axis=1, keepdims=True))
            p1 = jnp.exp2(s1 - jnp.max(s1, axis=1, keepdims=True))
            d0 = jnp.sum(p0, axis=1, keepdims=True)
            d1 = jnp.sum(p1, axis=1, keepdims=True)
            pm = jnp.concatenate([p0, p1], axis=1).astype(BF16)
            o = _dot(pm, vcat_ref[c])
            oc_chunks.append(o * jnp.where(low_t, 1.0 / d0, 1.0 / d1))
        o_c = jnp.concatenate(oc_chunks, axis=1)
        return o_b, o_c, k[part - WINDOW:part, :], v[part - WINDOW:part, :]

    def gate_mix(pi, filled, o_b, o_c):
        out_b = _silu(jnp.concatenate([filled["ag0"], filled["ag1"]], axis=1)) * o_b
        out_c = _silu(filled["mg"]) * o_c
        u = filled["cc"] * filled["ch"]
        part = sizes[pi]
        base = SUBLANES + starts[pi]
        ubuf_ref[base:base + part, :] = u
        u1 = ubuf_ref[base - 1:base - 1 + part, :]
        u2 = ubuf_ref[base - 2:base - 2 + part, :]
        cw = convw_ref[...]
        conv = cw[0:1, :] * u2 + cw[1:2, :] * u1 + cw[2:3, :] * u
        out_a = _silu(filled["cg"]) * filled["cb"] * conv
        return jnp.concatenate([out_a, out_b, out_c], axis=1).astype(BF16), u[part - SUBLANES:part, :]

    def finish(pi, st, mix):
        part = sizes[pi]
        r0 = starts[pi]
        half = part // 2
        y0 = _dot(mix[:half], wout_ref[...])
        y1 = _dot(mix[half:], wout_ref[...])
        xo_ref[r0:r0 + half, :] = st["x"][:half] + _rmsnorm(y0, g_post)
        xo_ref[r0 + half:r0 + part, :] = st["x"][half:] + _rmsnorm(y1, g_post)

    filler_cols = [("cg", OFF_CG), ("cb", OFF_CB), ("cc", OFF_CC), ("ch", OFF_CH),
                   ("ag0", OFF_AG), ("ag1", OFF_AG + 2 * LANES), ("mg", OFF_MG)]

    states = [start(0)]
    k_prev, v_prev = kprev_ref[...], vprev_ref[...]
    u_tail = None
    for pi in range(len(sizes)):
        st = states[pi]
        pending = list(filler_cols)
        filled = {}

        def fill_one():
            if pending:
                name, off = pending.pop(0)
                filled[name] = st["proj"](off, 2 * LANES)

        if pi + 1 < len(sizes):
            states.append(start(pi + 1))
        o_b, o_c, k_prev, v_prev = mix_part(pi, st, k_prev, v_prev, fill_one)
        while pending:
            fill_one()
        mix, u_tail = gate_mix(pi, filled, o_b, o_c)
        finish(pi, st, mix)

    kprev_ref[...] = k_prev
    vprev_ref[...] = v_prev
    ubuf_ref[0:SUBLANES, :] = u_tail
    convlast_ref[...] = u_tail

    @pl.when(i == n_tiles - 1)
    def _():
        klast_ref[...] = kprev_ref[...].T
        vlast_ref[...] = vprev_ref[...].T


def _prompt_layer(l, x, sinks, norm_pre, norm_post, w_in_bf, conv_w, kcat_t, vcat, w_out_bf, w_in, w_out, n_mem):
    batch, seq, _ = x.shape
    tile = PROMPT_TILE
    nt = seq // tile
    steps = batch * nt
    chunk = D_MODEL // steps
    assert chunk * steps == D_MODEL and chunk % (2 * SUBLANES) == 0
    next_chunk = lambda b, i: (l + 1, b * nt + i, 0)
    casts = []
    if l + 1 < DEPTH:
        casts += [(w_in, (None, chunk, IN_DIM), next_chunk, w_in_bf.shape, w_in_bf),
                  (w_out, (None, chunk, D_MODEL), next_chunk, w_out_bf.shape, w_out_bf)]
    in_specs = [
        pl.BlockSpec(memory_space=pltpu.SMEM),
        pl.BlockSpec((None, tile, D_MODEL), lambda b, i: (b, i, 0)),
        pl.BlockSpec((DEPTH, D_MODEL), lambda b, i: (0, 0)),
        pl.BlockSpec((DEPTH, D_MODEL), lambda b, i: (0, 0)),
        pl.BlockSpec((None, D_MODEL, IN_DIM), lambda b, i: (l, 0, 0)),
        pl.BlockSpec((None, CONV_W, CONV_DIM), lambda b, i: (l, 0, 0)),
        pl.BlockSpec((None, None, M_CHUNK, LANES, 2 * n_mem), lambda b, i: (l, b, 0, 0, 0)),
        pl.BlockSpec((None, None, M_CHUNK, 2 * n_mem, LANES), lambda b, i: (l, b, 0, 0, 0)),
        pl.BlockSpec((None, MIX_DIM, D_MODEL), lambda b, i: (l, 0, 0)),
    ]
    out_specs = [
        pl.BlockSpec((None, tile, D_MODEL), lambda b, i: (b, i, 0)),
        pl.BlockSpec((None, KV_DIM, WINDOW), lambda b, i: (b, 0, 0)),
        pl.BlockSpec((None, KV_DIM, WINDOW), lambda b, i: (b, 0, 0)),
        pl.BlockSpec((None, SUBLANES, CONV_DIM), lambda b, i: (b, 0, 0)),
    ]
    out_shape = [
        jax.ShapeDtypeStruct((batch, seq, D_MODEL), F32),
        jax.ShapeDtypeStruct((batch, KV_DIM, WINDOW), F32),
        jax.ShapeDtypeStruct((batch, KV_DIM, WINDOW), F32),
        jax.ShapeDtypeStruct((batch, SUBLANES, CONV_DIM), F32),
    ]
    operands = [sinks, x, norm_pre, norm_post, w_in_bf, conv_w, kcat_t, vcat, w_out_bf]
    assert len(operands) == N_MAIN_IN and len(out_specs) == N_MAIN_OUT
    aliases = {}
    for n, (src, block, index_map, dst_shape, dst) in enumerate(casts):
        operands.append(src)
        in_specs.append(pl.BlockSpec(block, index_map))
        out_specs.append(pl.BlockSpec(block, index_map))
        out_shape.append(jax.ShapeDtypeStruct(dst_shape, BF16))
        (dst_operand,) = [k for k, op in enumerate(operands[:N_MAIN_IN]) if op is dst]
        aliases[dst_operand] = N_MAIN_OUT + n
    outs = pl.pallas_call(
        functools.partial(_prompt_layer_kernel, layer=l, tile=tile, n_mem=n_mem, n_tiles=nt, n_cast=len(casts)),
        grid=(batch, nt),
        in_specs=in_specs,
        out_specs=out_specs,
        out_shape=out_shape,
        input_output_aliases=aliases,
        scratch_shapes=[
            pltpu.VMEM((WINDOW, KV_DIM), F32),
            pltpu.VMEM((WINDOW, KV_DIM), F32),
            pltpu.VMEM((SUBLANES + tile, CONV_DIM), F32),
        ],
        compiler_params=pltpu.CompilerParams(
            dimension_semantics=("arbitrary", "arbitrary"), vmem_limit_bytes=VMEM_LIMIT),
        name="prompt_layer",
    )(*operands)
    main, cast_out = outs[:N_MAIN_OUT], outs[N_MAIN_OUT:]
    if l + 1 < DEPTH:
        w_in_bf, w_out_bf = cast_out
    return (*main, w_in_bf, w_out_bf)


def _sample_kernel(x_ref, gpre_ref, gpost_ref, win_ref, wout_ref, convw_ref, sink_ref, convp_ref,
                   kt_ref, vt_ref, mkt_ref, mvt_ref,
                   y_ref, kto_ref, vto_ref, uo_ref,
                   xs_ref, z_ref, mix_ref, mk_buf, mv_buf, mem_sem, *, group, t, n_mem, n_sub, n_proj):
    l = pl.program_id(0)
    p = pl.program_id(1)
    ss = pl.program_id(2)
    nrows = group * t
    prows = n_sub * nrows
    rows_p = pl.ds(pl.multiple_of(p * prows, prows), prows)
    rows_s = pl.ds(pl.multiple_of(ss * nrows, nrows), nrows)

    groups_per_layer = n_proj * n_sub
    n_steps = DEPTH * groups_per_layer
    step = (l * n_proj + p) * n_sub + ss

    def mem_copies(at_step):
        layer = at_step // groups_per_layer
        first_seq = (at_step % groups_per_layer) * group
        slot = at_step % MEM_SLOTS
        return (pltpu.make_async_copy(mkt_ref.at[layer, pl.ds(first_seq, group)], mk_buf.at[slot], mem_sem.at[0, slot]),
                pltpu.make_async_copy(mvt_ref.at[layer, pl.ds(first_seq, group)], mv_buf.at[slot], mem_sem.at[1, slot]))

    @pl.when(step == 0)
    def _():
        for ahead in range(MEM_SLOTS - 1):
            for copy in mem_copies(ahead):
                copy.start()

    @pl.when(step + MEM_SLOTS - 1 < n_steps)
    def _():
        for copy in mem_copies(step + MEM_SLOTS - 1):
            copy.start()

    for copy in mem_copies(step):
        copy.wait()
    mem_slot = step % MEM_SLOTS

    @pl.when(ss == 0)
    def _():
        @pl.when(l == 0)
        def _():
            xs_ref[rows_p, :] = x_ref[...]

        h = _rmsnorm(xs_ref[rows_p, :], gpre_ref[pl.ds(l, 1), :]).astype(BF16)
        z_ref[...] = _dot(h, win_ref[...])

    def zc(off, width):
        return z_ref[rows_s, off:off + width]

    def per_seq(a):
        return a.reshape(group, t, a.shape[-1])

    u2d = zc(OFF_CC, CONV_DIM) * zc(OFF_CH, CONV_DIM)
    uo_ref[...] = u2d
    u = per_seq(u2d)
    prev = convp_ref[...]
    tpos = lax.broadcasted_iota(jnp.int32, (group, t, CONV_DIM), 1)
    u1 = jnp.where(tpos >= 1, pltpu.roll(u, 1, 1), pltpu.roll(prev, 1, 1))
    u2 = jnp.where(tpos >= 2, pltpu.roll(u, 2, 1), pltpu.roll(prev, 2, 1))
    cw = convw_ref[...]
    conv = cw[0:1, :] * u2 + cw[1:2, :] * u1 + cw[2:3, :] * u
    out_a = per_seq(_silu(zc(OFF_CG, CONV_DIM)) * zc(OFF_CB, CONV_DIM)) * conv

    kt_old = kt_ref[...]
    vt_old = vt_ref[...]
    k_new = zc(OFF_K, KV_DIM)
    v_new = zc(OFF_V, KV_DIM)
    k_new_t = jnp.swapaxes(per_seq(k_new), 1, 2)
    v_new_t = jnp.swapaxes(per_seq(v_new), 1, 2)
    kto_ref[...] = pltpu.roll(jnp.concatenate([k_new_t, kt_old[:, :, t:]], axis=2), WINDOW - t, 2)
    vto_ref[...] = pltpu.roll(jnp.concatenate([v_new_t, vt_old[:, :, t:]], axis=2), WINDOW - t, 2)

    q = per_seq(zc(OFF_Q, ATTN_DIM) * SCALE)
    low_q = _low_lanes((group, t, LANES))
    pieces = []
    for c in range(N_CHUNK):
        qc = q[:, :, c * LANES:(c + 1) * LANES]
        qs = _swap_halves(qc)
        if c // CHUNKS_PER_KV == 0:
            pieces += [jnp.where(low_q, qc, 0.0), jnp.where(low_q, qs, 0.0)]
        else:
            pieces += [jnp.where(low_q, 0.0, qs), jnp.where(low_q, 0.0, qc)]
    qbd = jnp.concatenate(pieces, axis=1).astype(BF16)
    nrow = ATTN_HEADS * t
    s_old = jnp.einsum('gqd,gdk->gqk', qbd, kt_old.astype(BF16), preferred_element_type=F32)
    pad_rows = jnp.zeros((WINDOW - group * t, KV_DIM), F32)
    k_new_w = jnp.concatenate([k_new, pad_rows], axis=0) if group * t < WINDOW else k_new
    v_new_w = jnp.concatenate([v_new, pad_rows], axis=0) if group * t < WINDOW else v_new
    s_new = _dot_nt(qbd.reshape(group * nrow, KV_DIM), k_new_w.astype(BF16)).reshape(group, nrow, WINDOW)
    tq = lax.broadcasted_iota(jnp.int32, (group, nrow, WINDOW), 1) % t
    col = lax.broadcasted_iota(jnp.int32, (group, nrow, WINDOW), 2)
    first_new = lax.broadcasted_iota(jnp.int32, (group, nrow, WINDOW), 0) * t
    s_old = jnp.where(col > tq, s_old, -jnp.inf)
    s_new = jnp.where((col >= first_new) & (col <= first_new + tq), s_new, -jnp.inf)
    sink = sink_ref[...]
    m = jnp.maximum(jnp.max(jnp.maximum(s_old, s_new), axis=2, keepdims=True), sink)
    p_old = jnp.exp(s_old - m)
    p_new = jnp.exp(s_new - m)
    d = jnp.sum(p_old + p_new, axis=2, keepdims=True) + jnp.exp(sink - m)
    o = jnp.einsum('gqk,gdk->gqd', p_old.astype(BF16), vt_old.astype(BF16), preferred_element_type=F32)
    o = o + _dot(p_new.astype(BF16).reshape(group * nrow, WINDOW), v_new_w.astype(BF16)).reshape(group, nrow, KV_DIM)
    o = o * (1.0 / d)
    ob_chunks = []
    for c in range(N_CHUNK):
        o_even = o[:, 2 * c * t:(2 * c + 1) * t]
        o_odd = o[:, (2 * c + 1) * t:(2 * c + 2) * t]
        if c // CHUNKS_PER_KV == 0:
            ob_chunks.append(jnp.where(low_q, o_even, _swap_halves(o_odd)))
        else:
            ob_chunks.append(jnp.where(low_q, _swap_halves(o_even), o_odd))
    o_b = jnp.concatenate(ob_chunks, axis=2)
    out_b = per_seq(_silu(zc(OFF_AG, ATTN_DIM))) * o_b

    mq = per_seq(zc(OFF_MQ, MEM_DIM) * SCALE)
    head_of_lane = lax.broadcasted_iota(jnp.int32, (group, t, MEM_DIM), 2) // HEAD_DIM
    qm = jnp.concatenate([jnp.where(head_of_lane == hh, mq, 0.0) for hh in range(MEM_HEADS)],
                         axis=1).astype(BF16)
    s = jnp.einsum('gqd,gdk->gqk', qm, mk_buf[mem_slot].astype(BF16), preferred_element_type=F32)
    p = jnp.exp(s - jnp.max(s, axis=2, keepdims=True))
    d = jnp.sum(p, axis=2, keepdims=True)
    o = jnp.einsum('gqk,gdk->gqd', p.astype(BF16), mv_buf[mem_slot].astype(BF16), preferred_element_type=F32)
    o = o * (1.0 / d)
    o_c = jnp.zeros((group, t, MEM_DIM), F32)
    for hh in range(MEM_HEADS):
        o_c = jnp.where(head_of_lane == hh, o[:, hh * t:(hh + 1) * t], o_c)
    out_c = per_seq(_silu(zc(OFF_MG, MEM_DIM))) * o_c

    mix_ref[rows_s, :] = jnp.concatenate([out_a, out_b, out_c], axis=2).reshape(nrows, MIX_DIM).astype(BF16)

    @pl.when(ss == n_sub - 1)
    def _():
        y = _dot(mix_ref[...], wout_ref[...])
        x_new = xs_ref[rows_p, :] + _rmsnorm(y, gpost_ref[pl.ds(l, 1), :])
        xs_ref[rows_p, :] = x_new

        @pl.when(l == DEPTH - 1)
        def _():
            y_ref[...] = x_new


def _sample_stream(x2d, norm_pre, norm_post, w_in_bf, w_out_bf, conv_w, sink_rows, convp, cache_kt, cache_vt,
                   cache_mkt, cache_mvt, n_seq, t, n_mem):
    group = SAMPLE_GROUP
    n_sub = SAMPLE_SUBSTEPS
    n_proj = n_seq // (group * n_sub)
    nrows = group * t
    prows = n_sub * nrows
    nrow_attn = ATTN_HEADS * t
    per_layer = lambda l, p, ss: (l, 0, 0)
    per_group = lambda l, p, ss: (l, p * n_sub + ss, 0, 0)
    return pl.pallas_call(
        functools.partial(_sample_kernel, group=group, t=t, n_mem=n_mem, n_sub=n_sub, n_proj=n_proj),
        grid=(DEPTH, n_proj, n_sub),
        in_specs=[
            pl.BlockSpec((prows, D_MODEL), lambda l, p, ss: (jnp.where(l == 0, p, n_proj - 1), 0)),
            pl.BlockSpec((DEPTH, D_MODEL), lambda l, p, ss: (0, 0)),
            pl.BlockSpec((DEPTH, D_MODEL), lambda l, p, ss: (0, 0)),
            pl.BlockSpec((None, D_MODEL, IN_DIM), per_layer),
            pl.BlockSpec((None, MIX_DIM, D_MODEL), per_layer),
            pl.BlockSpec((None, CONV_W, CONV_DIM), per_layer),
            pl.BlockSpec((None, nrow_attn, 1), per_layer),
            pl.BlockSpec((None, group, t, CONV_DIM), per_group),
            pl.BlockSpec((None, group, KV_DIM, WINDOW), per_group),
            pl.BlockSpec((None, group, KV_DIM, WINDOW), per_group),
            pl.BlockSpec(memory_space=pl.ANY),
            pl.BlockSpec(memory_space=pl.ANY),
        ],
        out_specs=[
            pl.BlockSpec((prows, D_MODEL), lambda l, p, ss: (jnp.where(l == DEPTH - 1, p, 0), 0)),
            pl.BlockSpec((None, group, KV_DIM, WINDOW), per_group),
            pl.BlockSpec((None, group, KV_DIM, WINDOW), per_group),
            pl.BlockSpec((None, nrows, CONV_DIM), lambda l, p, ss: (l, p * n_sub + ss, 0)),
        ],
        out_shape=[
            jax.ShapeDtypeStruct((n_seq * t, D_MODEL), F32),
            jax.ShapeDtypeStruct((DEPTH, n_seq, KV_DIM, WINDOW), F32),
            jax.ShapeDtypeStruct((DEPTH, n_seq, KV_DIM, WINDOW), F32),
            jax.ShapeDtypeStruct((DEPTH, n_seq * t, CONV_DIM), F32),
        ],
        scratch_shapes=[
            pltpu.VMEM((n_seq * t, D_MODEL), F32),
            pltpu.VMEM((prows, IN_DIM), F32),
            pltpu.VMEM((prows, MIX_DIM), BF16),
            pltpu.VMEM((MEM_SLOTS, group, MEM_DIM, n_mem), F32),
            pltpu.VMEM((MEM_SLOTS, group, MEM_DIM, n_mem), F32),
            pltpu.SemaphoreType.DMA((2, MEM_SLOTS)),
        ],
        compiler_params=pltpu.CompilerParams(
            dimension_semantics=("arbitrary", "arbitrary", "arbitrary"), vmem_limit_bytes=VMEM_LIMIT),
        name="sample_stream",
    )(x2d, norm_pre, norm_post, w_in_bf, w_out_bf, conv_w, sink_rows, convp, cache_kt, cache_vt, cache_mkt, cache_mvt)


def _keys_minor(a):
    lead = a.shape[:-3]
    n, heads, hd = a.shape[-3:]
    nd = a.ndim
    perm = tuple(range(nd - 3)) + (nd - 2, nd - 1, nd - 3)
    return jnp.transpose(a, perm).reshape(*lead, heads * hd, n)


def _keys_major(a, heads):
    lead = a.shape[:-2]
    n = a.shape[-1]
    a = a.reshape(*lead, heads, HEAD_DIM, n)
    nd = a.ndim
    perm = tuple(range(nd - 3)) + (nd - 1, nd - 3, nd - 2)
    return jnp.transpose(a, perm)


def kernel(x_prompt, x_sample, mem_prompt, cache_win_k, cache_win_v, state_conv, cache_mem_k, cache_mem_v,
           norm_pre, norm_post, norm_mem, w_in, conv_w, attn_sinks, w_mem_kv, w_out):
    batch, seq, _ = x_prompt.shape
    n_seq, t, _ = x_sample.shape
    n_mem = mem_prompt.shape[1]
    assert seq % PROMPT_TILE == 0 and sum(PROMPT_PART_ROWS) == PROMPT_TILE
    assert all(rows % WINDOW == 0 for rows in PROMPT_PART_ROWS)
    assert n_seq % (SAMPLE_GROUP * SAMPLE_SUBSTEPS) == 0 and t == SUBLANES and SAMPLE_GROUP * t <= WINDOW

    sink_rows = jnp.repeat(attn_sinks, t, axis=1)[:, :, None]

    mkt, mvt, kcat_t, vcat, w_in_bf, w_out_bf = _memkv(
        mem_prompt.reshape(batch * n_mem, D_MODEL), norm_mem, w_mem_kv, w_in, w_out, batch, n_mem)

    xp = x_prompt
    ktp, vtp, cvp = [], [], []
    for l in range(DEPTH):
        xp, k_p, v_p, c_p, w_in_bf, w_out_bf = _prompt_layer(
            l, xp, attn_sinks, norm_pre, norm_post, w_in_bf, conv_w, kcat_t, vcat, w_out_bf, w_in, w_out, n_mem)
        ktp.append(k_p)
        vtp.append(v_p)
        cvp.append(c_p[:, SUBLANES - CONV_BUF:, :])

    convp = jnp.pad(state_conv, ((0, 0), (0, 0), (t - CONV_BUF, 0), (0, 0)))
    ys, kts, vts, us = _sample_stream(
        x_sample.reshape(n_seq * t, D_MODEL), norm_pre, norm_post, w_in_bf, w_out_bf, conv_w, sink_rows, convp,
        _keys_minor(cache_win_k), _keys_minor(cache_win_v), _keys_minor(cache_mem_k), _keys_minor(cache_mem_v),
        n_seq, t, n_mem)

    return (xp,
            ys.reshape(n_seq, t, D_MODEL),
            _keys_major(jnp.stack(ktp), KV_HEADS),
            _keys_major(jnp.stack(vtp), KV_HEADS),
            jnp.stack(cvp),
            _keys_major(mkt, MEM_HEADS),
            _keys_major(mvt, MEM_HEADS),
            _keys_major(kts, KV_HEADS),
            _keys_major(vts, KV_HEADS),
            us.reshape(DEPTH, n_seq, t, CONV_DIM)[:, :, t - CONV_BUF:, :])
```

```python
import functools

import jax
import jax.numpy as jnp
from jax import lax
from jax.experimental import pallas as pl
from jax.experimental.pallas import tpu as pltpu

D_MODEL = 1024
DEPTH = 4
HEAD_DIM = 64
ATTN_HEADS = 8
KV_HEADS = 2
ATTN_DIM = ATTN_HEADS * HEAD_DIM
KV_DIM = KV_HEADS * HEAD_DIM
WINDOW = 128
MEM_HEADS = 4
MEM_DIM = MEM_HEADS * HEAD_DIM
CONV_DIM = 256
CONV_W = 3
CONV_BUF = CONV_W - 1
MIX_DIM = CONV_DIM + ATTN_DIM + MEM_DIM
IN_DIM = 4 * CONV_DIM + 2 * ATTN_DIM + 2 * KV_DIM + 2 * MEM_DIM
RMS_EPS = 1e-6
SCALE = HEAD_DIM ** -0.5
LOG2E = 1.4426950408889634

LANES = 128
SUBLANES = 8
VMEM_LIMIT = 56 * 1024 * 1024

OFF_CB, OFF_CC, OFF_CH, OFF_CG = 0, CONV_DIM, 2 * CONV_DIM, 3 * CONV_DIM
OFF_Q = 4 * CONV_DIM
OFF_K = OFF_Q + ATTN_DIM
OFF_V = OFF_K + KV_DIM
OFF_AG = OFF_V + KV_DIM
OFF_MQ = OFF_AG + ATTN_DIM
OFF_MG = OFF_MQ + MEM_DIM
assert OFF_MG + MEM_DIM == IN_DIM and OFF_V == OFF_K + LANES

N_CHUNK = ATTN_DIM // LANES
M_CHUNK = MEM_DIM // LANES
CHUNKS_PER_KV = N_CHUNK // KV_HEADS

PROMPT_TILE = 1024
PROMPT_PART_ROWS = (512, 512)
SAMPLE_GROUP = 8
SAMPLE_SUBSTEPS = 4
MEM_SLOTS = 4

F32 = jnp.float32
BF16 = jnp.bfloat16


def _rmsnorm(x, g):
    r = lax.rsqrt(jnp.mean(x * x, axis=-1, keepdims=True) + RMS_EPS)
    return (x * r) * g


def _silu(x):
    return x * jax.nn.sigmoid(x)


def _dot(a, b):
    return jnp.dot(a, b, preferred_element_type=F32)


def _dot_nt(a, b):
    return lax.dot_general(a, b, (((1,), (1,)), ((), ())), preferred_element_type=F32)


def _low_lanes(shape):
    return lax.broadcasted_iota(jnp.int32, shape, len(shape) - 1) < HEAD_DIM


def _swap_halves(a):
    return pltpu.roll(a, HEAD_DIM, a.ndim - 1)


def _memkv_kernel(mem_ref, g_ref, w_ref, win_ref, wout_ref, mkt_ref, mvt_ref, kcat_t_ref, vcat_ref,
                  win_bf_ref, wout_bf_ref, *, batch, n_mem):
    win_bf_ref[0] = win_ref[...].astype(BF16)
    wout_bf_ref[0] = wout_ref[...].astype(BF16)
    win_bf_ref[1:] = jnp.zeros((DEPTH - 1,) + win_ref.shape, BF16)
    wout_bf_ref[1:] = jnp.zeros((DEPTH - 1,) + wout_ref.shape, BF16)
    h = _rmsnorm(mem_ref[...], g_ref[pl.ds(pl.program_id(0), 1), :]).astype(BF16)
    kv = _dot(h, w_ref[...].astype(BF16))
    mk = kv[:, :MEM_DIM]
    mv = kv[:, MEM_DIM:]
    low = _low_lanes((n_mem, LANES))
    for b in range(batch):
        mkt_ref[b] = mk[b * n_mem:(b + 1) * n_mem, :].T
        mvt_ref[b] = mv[b * n_mem:(b + 1) * n_mem, :].T
        for c in range(M_CHUNK):
            kc = mk[b * n_mem:(b + 1) * n_mem, c * LANES:(c + 1) * LANES]
            vc = mv[b * n_mem:(b + 1) * n_mem, c * LANES:(c + 1) * LANES]
            kcat = jnp.concatenate([jnp.where(low, kc, 0.0), jnp.where(low, 0.0, kc)], axis=0)
            vcat = jnp.concatenate([jnp.where(low, vc, 0.0), jnp.where(low, 0.0, vc)], axis=0)
            kcat_t_ref[b, c] = kcat.T.astype(BF16)
            vcat_ref[b, c] = vcat.astype(BF16)


def _memkv(mem2d, norm_mem, w_mem_kv, w_in, w_out, batch, n_mem):
    rows = batch * n_mem
    chunk = D_MODEL // DEPTH
    return pl.pallas_call(
        functools.partial(_memkv_kernel, batch=batch, n_mem=n_mem),
        grid=(DEPTH,),
        in_specs=[
            pl.BlockSpec((rows, D_MODEL), lambda l: (0, 0)),
            pl.BlockSpec((DEPTH, D_MODEL), lambda l: (0, 0)),
            pl.BlockSpec((None, D_MODEL, 2 * MEM_DIM), lambda l: (l, 0, 0)),
            pl.BlockSpec((None, chunk, IN_DIM), lambda l: (0, l, 0)),
            pl.BlockSpec((None, chunk, D_MODEL), lambda l: (0, l, 0)),
        ],
        out_specs=[
            pl.BlockSpec((None, batch, MEM_DIM, n_mem), lambda l: (l, 0, 0, 0)),
            pl.BlockSpec((None, batch, MEM_DIM, n_mem), lambda l: (l, 0, 0, 0)),
            pl.BlockSpec((None, batch, M_CHUNK, LANES, 2 * n_mem), lambda l: (l, 0, 0, 0, 0)),
            pl.BlockSpec((None, batch, M_CHUNK, 2 * n_mem, LANES), lambda l: (l, 0, 0, 0, 0)),
            pl.BlockSpec((DEPTH, chunk, IN_DIM), lambda l: (0, l, 0)),
            pl.BlockSpec((DEPTH, chunk, D_MODEL), lambda l: (0, l, 0)),
        ],
        out_shape=[
            jax.ShapeDtypeStruct((DEPTH, batch, MEM_DIM, n_mem), F32),
            jax.ShapeDtypeStruct((DEPTH, batch, MEM_DIM, n_mem), F32),
            jax.ShapeDtypeStruct((DEPTH, batch, M_CHUNK, LANES, 2 * n_mem), BF16),
            jax.ShapeDtypeStruct((DEPTH, batch, M_CHUNK, 2 * n_mem, LANES), BF16),
            jax.ShapeDtypeStruct((DEPTH, D_MODEL, IN_DIM), BF16),
            jax.ShapeDtypeStruct((DEPTH, MIX_DIM, D_MODEL), BF16),
        ],
        compiler_params=pltpu.CompilerParams(dimension_semantics=("arbitrary",)),
        name="memkv",
    )(mem2d, norm_mem, w_mem_kv, w_in, w_out)


N_MAIN_IN, N_MAIN_OUT = 9, 4


def _prompt_layer_kernel(*refs, layer, tile, n_mem, n_tiles, n_cast):
    main_in = refs[:N_MAIN_IN]
    cast_in = refs[N_MAIN_IN:N_MAIN_IN + n_cast]
    outs = refs[N_MAIN_IN + n_cast:]
    main_out, cast_out, scratch = outs[:N_MAIN_OUT], outs[N_MAIN_OUT:N_MAIN_OUT + n_cast], outs[N_MAIN_OUT + n_cast:]
    for src, dst in zip(cast_in, cast_out):
        dst[...] = src[...].astype(BF16)
    _prompt_tile(*main_in, *main_out, *scratch, layer=layer, tile=tile, n_mem=n_mem, n_tiles=n_tiles)


def _prompt_tile(sink_ref, x_ref, gpre_ref, gpost_ref, win_ref, convw_ref, kcat_t_ref,
                 vcat_ref, wout_ref,
                 xo_ref, klast_ref, vlast_ref, convlast_ref,
                 kprev_ref, vprev_ref, ubuf_ref, *, layer, tile, n_mem, n_tiles):
    i = pl.program_id(1)
    sizes = PROMPT_PART_ROWS
    starts = [sum(sizes[:p]) for p in range(len(sizes))]
    g_pre = gpre_ref[layer:layer + 1, :]
    g_post = gpost_ref[layer:layer + 1, :]

    @pl.when(i == 0)
    def _():
        kprev_ref[...] = jnp.zeros_like(kprev_ref)
        vprev_ref[...] = jnp.zeros_like(vprev_ref)
        ubuf_ref[0:SUBLANES, :] = jnp.zeros((SUBLANES, CONV_DIM), F32)

    rows = N_CHUNK * WINDOW
    qpos = lax.broadcasted_iota(jnp.int32, (rows, 2 * WINDOW), 0) % WINDOW
    kpos = lax.broadcasted_iota(jnp.int32, (rows, 2 * WINDOW), 1)
    band = (kpos > qpos) & (kpos <= qpos + WINDOW)
    band_first = band & ((kpos >= WINDOW) | (i > 0))
    chunk_of_row = lax.broadcasted_iota(jnp.int32, (rows, 1), 0) // WINDOW
    sink_lo = jnp.zeros((rows, 1), F32)
    sink_hi = jnp.zeros((rows, 1), F32)
    for p in range(N_CHUNK):
        sink_lo = jnp.where(chunk_of_row == p, sink_ref[layer, p] * LOG2E, sink_lo)
        sink_hi = jnp.where(chunk_of_row == p, sink_ref[layer, N_CHUNK + p] * LOG2E, sink_hi)
    low_o = _low_lanes((rows, LANES))
    low_w = _low_lanes((WINDOW, LANES))

    def start(pi):
        x = x_ref[starts[pi]:starts[pi] + sizes[pi], :]
        h = _rmsnorm(x, g_pre).astype(BF16)
        proj = lambda off, width: _dot(h, win_ref[:, off:off + width])
        return dict(x=x, proj=proj, q_raw=proj(OFF_Q, ATTN_DIM), kv=proj(OFF_K, 2 * KV_DIM),
                    mq_raw=proj(OFF_MQ, MEM_DIM))

    def mix_part(pi, st, k_prev, v_prev, fill_one):
        part = sizes[pi]
        n_blocks = part // WINDOW
        low_t = _low_lanes((part, LANES))
        low_kv = _low_lanes((WINDOW + part, KV_DIM))
        k = st["kv"][:, :KV_DIM]
        v = st["kv"][:, KV_DIM:]
        kfull = jnp.concatenate([k_prev, k], axis=0)
        vfull = jnp.concatenate([v_prev, v], axis=0)
        klo = jnp.where(low_kv, kfull, 0.0).astype(BF16)
        khi = jnp.where(low_kv, 0.0, kfull).astype(BF16)
        vlo = jnp.where(low_kv, vfull, 0.0).astype(BF16)
        vhi = jnp.where(low_kv, 0.0, vfull).astype(BF16)

        q = st["q_raw"] * (SCALE * LOG2E)
        nat = [q[:, c * LANES:(c + 1) * LANES] for c in range(N_CHUNK)]
        qp = []
        for p in range(N_CHUNK):
            a, b = nat[p // 2], nat[CHUNKS_PER_KV + p // 2]
            pair = jnp.where(low_t, a, _swap_halves(b)) if p % 2 == 0 else jnp.where(low_t, _swap_halves(a), b)
            qp.append(pair.astype(BF16))

        def swa_scores(j):
            r0 = j * WINDOW
            q_all = jnp.concatenate([qp[p][r0:r0 + WINDOW] for p in range(N_CHUNK)], axis=0)
            k_cat = jnp.concatenate([klo[r0:r0 + 2 * WINDOW], khi[r0:r0 + 2 * WINDOW]], axis=0)
            return _dot_nt(q_all, k_cat)

        o_blocks = []
        fill_one()
        s = swa_scores(0)
        for j in range(n_blocks):
            r0 = j * WINDOW
            fill_one()
            s_next = swa_scores(j + 1) if j + 1 < n_blocks else None
            v_cat = jnp.concatenate([vlo[r0:r0 + 2 * WINDOW], vhi[r0:r0 + 2 * WINDOW]], axis=0)
            mask = band_first if (pi == 0 and j == 0) else band
            s_lo = jnp.where(mask, s[:, :2 * WINDOW], -jnp.inf)
            s_hi = jnp.where(mask, s[:, 2 * WINDOW:], -jnp.inf)
            m_lo = jnp.maximum(jnp.max(s_lo, axis=1, keepdims=True), sink_lo)
            m_hi = jnp.maximum(jnp.max(s_hi, axis=1, keepdims=True), sink_hi)
            p_lo = jnp.exp2(s_lo - m_lo)
            p_hi = jnp.exp2(s_hi - m_hi)
            d_lo = jnp.sum(p_lo, axis=1, keepdims=True) + jnp.exp2(sink_lo - m_lo)
            d_hi = jnp.sum(p_hi, axis=1, keepdims=True) + jnp.exp2(sink_hi - m_hi)
            pr = jnp.concatenate([p_lo, p_hi], axis=1).astype(BF16)
            o = _dot(pr, v_cat)
            o = o * jnp.where(low_o, 1.0 / d_lo, 1.0 / d_hi)
            op = [o[p * WINDOW:(p + 1) * WINDOW] for p in range(N_CHUNK)]
            o_blocks.append(jnp.concatenate(
                [jnp.where(low_w, op[0], _swap_halves(op[1])), jnp.where(low_w, op[2], _swap_halves(op[3])),
                 jnp.where(low_w, _swap_halves(op[0]), op[1]), jnp.where(low_w, _swap_halves(op[2]), op[3])],
                axis=1))
            s = s_next
        o_b = jnp.concatenate(o_blocks, axis=0) if len(o_blocks) > 1 else o_blocks[0]

        mq = (st["mq_raw"] * (SCALE * LOG2E)).astype(BF16)
        oc_chunks = []
        for c in range(M_CHUNK):
            s = _dot(mq[:, c * LANES:(c + 1) * LANES], kcat_t_ref[c])
            fill_one()
            s0 = s[:, :n_mem]
            s1 = s[:, n_mem:]
            p0 = jnp.exp2(s0 - jnp.max(s0, axis=1, keepdims=True))
            p1 = jnp.exp2(s1 - jnp.max(s1, axis=1, keepdims=True))
            d0 = jnp.sum(p0, axis=1, keepdims=True)
            d1 = jnp.sum(p1, axis=1, keepdims=True)
            pm = jnp.concatenate([p0, p1], axis=1).astype(BF16)
            o = _dot(pm, vcat_ref[c])
            oc_chunks.append(o * jnp.where(low_t, 1.0 / d0, 1.0 / d1))
        o_c = jnp.concatenate(oc_chunks, axis=1)
        return o_b, o_c, k[part - WINDOW:part, :], v[part - WINDOW:part, :]

    def gate_mix(pi, filled, o_b, o_c):
        out_b = _silu(jnp.concatenate([filled["ag0"], filled["ag1"]], axis=1)) * o_b
        out_c = _silu(filled["mg"]) * o_c
        u = filled["cc"] * filled["ch"]
        part = sizes[pi]
        base = SUBLANES + starts[pi]
        ubuf_ref[base:base + part, :] = u
        u1 = ubuf_ref[base - 1:base - 1 + part, :]
        u2 = ubuf_ref[base - 2:base - 2 + part, :]
        cw = convw_ref[...]
        conv = cw[0:1, :] * u2 + cw[1:2, :] * u1 + cw[2:3, :] * u
        out_a = _silu(filled["cg"]) * filled["cb"] * conv
        return jnp.concatenate([out_a, out_b, out_c], axis=1).astype(BF16), u[part - SUBLANES:part, :]

    def finish(pi, st, mix):
        part = sizes[pi]
        r0 = starts[pi]
        half = part // 2
        y0 = _dot(mix[:half], wout_ref[...])
        y1 = _dot(mix[half:], wout_ref[...])
        xo_ref[r0:r0 + half, :] = st["x"][:half] + _rmsnorm(y0, g_post)
        xo_ref[r0 + half:r0 + part, :] = st["x"][half:] + _rmsnorm(y1, g_post)

    filler_cols = [("cg", OFF_CG), ("cb", OFF_CB), ("cc", OFF_CC), ("ch", OFF_CH),
                   ("ag0", OFF_AG), ("ag1", OFF_AG + 2 * LANES), ("mg", OFF_MG)]

    states = [start(0)]
    k_prev, v_prev = kprev_ref[...], vprev_ref[...]
    u_tail = None
    for pi in range(len(sizes)):
        st = states[pi]
        pending = list(filler_cols)
        filled = {}

        def fill_one():
            if pending:
                name, off = pending.pop(0)
                filled[name] = st["proj"](off, 2 * LANES)

        if pi + 1 < len(sizes):
            states.append(start(pi + 1))
        o_b, o_c, k_prev, v_prev = mix_part(pi, st, k_prev, v_prev, fill_one)
        while pending:
            fill_one()
        mix, u_tail = gate_mix(pi, filled, o_b, o_c)
        finish(pi, st, mix)

    kprev_ref[...] = k_prev
    vprev_ref[...] = v_prev
    ubuf_ref[0:SUBLANES, :] = u_tail
    convlast_ref[...] = u_tail

    @pl.when(i == n_tiles - 1)
    def _():
        klast_ref[...] = kprev_ref[...].T
        vlast_ref[...] = vprev_ref[...].T


def _prompt_layer(l, x, sinks, norm_pre, norm_post, w_in_bf, conv_w, kcat_t, vcat, w_out_bf, w_in, w_out, n_mem):
    batch, seq, _ = x.shape
    tile = PROMPT_TILE
    nt = seq // tile
    steps = batch * nt
    chunk = D_MODEL // steps
    assert chunk * steps == D_MODEL and chunk % (2 * SUBLANES) == 0
    next_chunk = lambda b, i: (l + 1, b * nt + i, 0)
    casts = []
    if l + 1 < DEPTH:
        casts += [(w_in, (None, chunk, IN_DIM), next_chunk, w_in_bf.shape, w_in_bf),
                  (w_out, (None, chunk, D_MODEL), next_chunk, w_out_bf.shape, w_out_bf)]
    in_specs = [
        pl.BlockSpec(memory_space=pltpu.SMEM),
        pl.BlockSpec((None, tile, D_MODEL), lambda b, i: (b, i, 0)),
        pl.BlockSpec((DEPTH, D_MODEL), lambda b, i: (0, 0)),
        pl.BlockSpec((DEPTH, D_MODEL), lambda b, i: (0, 0)),
        pl.BlockSpec((None, D_MODEL, IN_DIM), lambda b, i: (l, 0, 0)),
        pl.BlockSpec((None, CONV_W, CONV_DIM), lambda b, i: (l, 0, 0)),
        pl.BlockSpec((None, None, M_CHUNK, LANES, 2 * n_mem), lambda b, i: (l, b, 0, 0, 0)),
        pl.BlockSpec((None, None, M_CHUNK, 2 * n_mem, LANES), lambda b, i: (l, b, 0, 0, 0)),
        pl.BlockSpec((None, MIX_DIM, D_MODEL), lambda b, i: (l, 0, 0)),
    ]
    out_specs = [
        pl.BlockSpec((None, tile, D_MODEL), lambda b, i: (b, i, 0)),
        pl.BlockSpec((None, KV_DIM, WINDOW), lambda b, i: (b, 0, 0)),
        pl.BlockSpec((None, KV_DIM, WINDOW), lambda b, i: (b, 0, 0)),
        pl.BlockSpec((None, SUBLANES, CONV_DIM), lambda b, i: (b, 0, 0)),
    ]
    out_shape = [
        jax.ShapeDtypeStruct((batch, seq, D_MODEL), F32),
        jax.ShapeDtypeStruct((batch, KV_DIM, WINDOW), F32),
        jax.ShapeDtypeStruct((batch, KV_DIM, WINDOW), F32),
        jax.ShapeDtypeStruct((batch, SUBLANES, CONV_DIM), F32),
    ]
    operands = [sinks, x, norm_pre, norm_post, w_in_bf, conv_w, kcat_t, vcat, w_out_bf]
    assert len(operands) == N_MAIN_IN and len(out_specs) == N_MAIN_OUT
    aliases = {}
    for n, (src, block, index_map, dst_shape, dst) in enumerate(casts):
        operands.append(src)
        in_specs.append(pl.BlockSpec(block, index_map))
        out_specs.append(pl.BlockSpec(block, index_map))
        out_shape.append(jax.ShapeDtypeStruct(dst_shape, BF16))
        (dst_operand,) = [k for k, op in enumerate(operands[:N_MAIN_IN]) if op is dst]
        aliases[dst_operand] = N_MAIN_OUT + n
    outs = pl.pallas_call(
        functools.partial(_prompt_layer_kernel, layer=l, tile=tile, n_mem=n_mem, n_tiles=nt, n_cast=len(casts)),
        grid=(batch, nt),
        in_specs=in_specs,
        out_specs=out_specs,
        out_shape=out_shape,
        input_output_aliases=aliases,
        scratch_shapes=[
            pltpu.VMEM((WINDOW, KV_DIM), F32),
            pltpu.VMEM((WINDOW, KV_DIM), F32),
            pltpu.VMEM((SUBLANES + tile, CONV_DIM), F32),
        ],
        compiler_params=pltpu.CompilerParams(
            dimension_semantics=("arbitrary", "arbitrary"), vmem_limit_bytes=VMEM_LIMIT),
        name="prompt_layer",
    )(*operands)
    main, cast_out = outs[:N_MAIN_OUT], outs[N_MAIN_OUT:]
    if l + 1 < DEPTH:
        w_in_bf, w_out_bf = cast_out
    return (*main, w_in_bf, w_out_bf)


def _sample_kernel(x_ref, gpre_ref, gpost_ref, win_ref, wout_ref, convw_ref, sink_ref, convp_ref,
                   kt_ref, vt_ref, mkt_ref, mvt_ref,
                   y_ref, kto_ref, vto_ref, uo_ref,
                   xs_ref, z_ref, mix_ref, mk_buf, mv_buf, mem_sem, *, group, t, n_mem, n_sub, n_proj):
    l = pl.program_id(0)
    p = pl.program_id(1)
    ss = pl.program_id(2)
    nrows = group * t
    prows = n_sub * nrows
    rows_p = pl.ds(pl.multiple_of(p * prows, prows), prows)
    rows_s = pl.ds(pl.multiple_of(ss * nrows, nrows), nrows)

    groups_per_layer = n_proj * n_sub
    n_steps = DEPTH * groups_per_layer
    step = (l * n_proj + p) * n_sub + ss

    def mem_copies(at_step):
        layer = at_step // groups_per_layer
        first_seq = (at_step % groups_per_layer) * group
        slot = at_step % MEM_SLOTS
        return (pltpu.make_async_copy(mkt_ref.at[layer, pl.ds(first_seq, group)], mk_buf.at[slot], mem_sem.at[0, slot]),
                pltpu.make_async_copy(mvt_ref.at[layer, pl.ds(first_seq, group)], mv_buf.at[slot], mem_sem.at[1, slot]))

    @pl.when(step == 0)
    def _():
        for ahead in range(MEM_SLOTS - 1):
            for copy in mem_copies(ahead):
                copy.start()

    @pl.when(step + MEM_SLOTS - 1 < n_steps)
    def _():
        for copy in mem_copies(step + MEM_SLOTS - 1):
            copy.start()

    for copy in mem_copies(step):
        copy.wait()
    mem_slot = step % MEM_SLOTS

    @pl.when(ss == 0)
    def _():
        @pl.when(l == 0)
        def _():
            xs_ref[rows_p, :] = x_ref[...]

        h = _rmsnorm(xs_ref[rows_p, :], gpre_ref[pl.ds(l, 1), :]).astype(BF16)
        z_ref[...] = _dot(h, win_ref[...])

    def zc(off, width):
        return z_ref[rows_s, off:off + width]

    def per_seq(a):
        return a.reshape(group, t, a.shape[-1])

    u2d = zc(OFF_CC, CONV_DIM) * zc(OFF_CH, CONV_DIM)
    uo_ref[...] = u2d
    u = per_seq(u2d)
    prev = convp_ref[...]
    tpos = lax.broadcasted_iota(jnp.int32, (group, t, CONV_DIM), 1)
    u1 = jnp.where(tpos >= 1, pltpu.roll(u, 1, 1), pltpu.roll(prev, 1, 1))
    u2 = jnp.where(tpos >= 2, pltpu.roll(u, 2, 1), pltpu.roll(prev, 2, 1))
    cw = convw_ref[...]
    conv = cw[0:1, :] * u2 + cw[1:2, :] * u1 + cw[2:3, :] * u
    out_a = per_seq(_silu(zc(OFF_CG, CONV_DIM)) * zc(OFF_CB, CONV_DIM)) * conv

    kt_old = kt_ref[...]
    vt_old = vt_ref[...]
    k_new = zc(OFF_K, KV_DIM)
    v_new = zc(OFF_V, KV_DIM)
    k_new_t = jnp.swapaxes(per_seq(k_new), 1, 2)
    v_new_t = jnp.swapaxes(per_seq(v_new), 1, 2)
    kto_ref[...] = pltpu.roll(jnp.concatenate([k_new_t, kt_old[:, :, t:]], axis=2), WINDOW - t, 2)
    vto_ref[...] = pltpu.roll(jnp.concatenate([v_new_t, vt_old[:, :, t:]], axis=2), WINDOW - t, 2)

    q = per_seq(zc(OFF_Q, ATTN_DIM) * SCALE)
    low_q = _low_lanes((group, t, LANES))
    pieces = []
    for c in range(N_CHUNK):
        qc = q[:, :, c * LANES:(c + 1) * LANES]
        qs = _swap_halves(qc)
        if c // CHUNKS_PER_KV == 0:
            pieces += [jnp.where(low_q, qc, 0.0), jnp.where(low_q, qs, 0.0)]
        else:
            pieces += [jnp.where(low_q, 0.0, qs), jnp.where(low_q, 0.0, qc)]
    qbd = jnp.concatenate(pieces, axis=1).astype(BF16)
    nrow = ATTN_HEADS * t
    s_old = jnp.einsum('gqd,gdk->gqk', qbd, kt_old.astype(BF16), preferred_element_type=F32)
    pad_rows = jnp.zeros((WINDOW - group * t, KV_DIM), F32)
    k_new_w = jnp.concatenate([k_new, pad_rows], axis=0) if group * t < WINDOW else k_new
    v_new_w = jnp.concatenate([v_new, pad_rows], axis=0) if group * t < WINDOW else v_new
    s_new = _dot_nt(qbd.reshape(group * nrow, KV_DIM), k_new_w.astype(BF16)).reshape(group, nrow, WINDOW)
    tq = lax.broadcasted_iota(jnp.int32, (group, nrow, WINDOW), 1) % t
    col = lax.broadcasted_iota(jnp.int32, (group, nrow, WINDOW), 2)
    first_new = lax.broadcasted_iota(jnp.int32, (group, nrow, WINDOW), 0) * t
    s_old = jnp.where(col > tq, s_old, -jnp.inf)
    s_new = jnp.where((col >= first_new) & (col <= first_new + tq), s_new, -jnp.inf)
    sink = sink_ref[...]
    m = jnp.maximum(jnp.max(jnp.maximum(s_old, s_new), axis=2, keepdims=True), sink)
    p_old = jnp.exp(s_old - m)
    p_new = jnp.exp(s_new - m)
    d = jnp.sum(p_old + p_new, axis=2, keepdims=True) + jnp.exp(sink - m)
    o = jnp.einsum('gqk,gdk->gqd', p_old.astype(BF16), vt_old.astype(BF16), preferred_element_type=F32)
    o = o + _dot(p_new.astype(BF16).reshape(group * nrow, WINDOW), v_new_w.astype(BF16)).reshape(group, nrow, KV_DIM)
    o = o * (1.0 / d)
    ob_chunks = []
    for c in range(N_CHUNK):
        o_even = o[:, 2 * c * t:(2 * c + 1) * t]
        o_odd = o[:, (2 * c + 1) * t:(2 * c + 2) * t]
        if c // CHUNKS_PER_KV == 0:
            ob_chunks.append(jnp.where(low_q, o_even, _swap_halves(o_odd)))
        else:
            ob_chunks.append(jnp.where(low_q, _swap_halves(o_even), o_odd))
    o_b = jnp.concatenate(ob_chunks, axis=2)
    out_b = per_seq(_silu(zc(OFF_AG, ATTN_DIM))) * o_b

    mq = per_seq(zc(OFF_MQ, MEM_DIM) * SCALE)
    head_of_lane = lax.broadcasted_iota(jnp.int32, (group, t, MEM_DIM), 2) // HEAD_DIM
    qm = jnp.concatenate([jnp.where(head_of_lane == hh, mq, 0.0) for hh in range(MEM_HEADS)],
                         axis=1).astype(BF16)
    s = jnp.einsum('gqd,gdk->gqk', qm, mk_buf[mem_slot].astype(BF16), preferred_element_type=F32)
    p = jnp.exp(s - jnp.max(s, axis=2, keepdims=True))
    d = jnp.sum(p, axis=2, keepdims=True)
    o = jnp.einsum('gqk,gdk->gqd', p.astype(BF16), mv_buf[mem_slot].astype(BF16), preferred_element_type=F32)
    o = o * (1.0 / d)
    o_c = jnp.zeros((group, t, MEM_DIM), F32)
    for hh in range(MEM_HEADS):
        o_c = jnp.where(head_of_lane == hh, o[:, hh * t:(hh + 1) * t], o_c)
    out_c = per_seq(_silu(zc(OFF_MG, MEM_DIM))) * o_c

    mix_ref[rows_s, :] = jnp.concatenate([out_a, out_b, out_c], axis=2).reshape(nrows, MIX_DIM).astype(BF16)

    @pl.when(ss == n_sub - 1)
    def _():
        y = _dot(mix_ref[...], wout_ref[...])
        x_new = xs_ref[rows_p, :] + _rmsnorm(y, gpost_ref[pl.ds(l, 1), :])
        xs_ref[rows_p, :] = x_new

        @pl.when(l == DEPTH - 1)
        def _():
            y_ref[...] = x_new


def _sample_stream(x2d, norm_pre, norm_post, w_in_bf, w_out_bf, conv_w, sink_rows, convp, cache_kt, cache_vt,
                   cache_mkt, cache_mvt, n_seq, t, n_mem):
    group = SAMPLE_GROUP
    n_sub = SAMPLE_SUBSTEPS
    n_proj = n_seq // (group * n_sub)
    nrows = group * t
    prows = n_sub * nrows
    nrow_attn = ATTN_HEADS * t
    per_layer = lambda l, p, ss: (l, 0, 0)
    per_group = lambda l, p, ss: (l, p * n_sub + ss, 0, 0)
    return pl.pallas_call(
        functools.partial(_sample_kernel, group=group, t=t, n_mem=n_mem, n_sub=n_sub, n_proj=n_proj),
        grid=(DEPTH, n_proj, n_sub),
        in_specs=[
            pl.BlockSpec((prows, D_MODEL), lambda l, p, ss: (jnp.where(l == 0, p, n_proj - 1), 0)),
            pl.BlockSpec((DEPTH, D_MODEL), lambda l, p, ss: (0, 0)),
            pl.BlockSpec((DEPTH, D_MODEL), lambda l, p, ss: (0, 0)),
            pl.BlockSpec((None, D_MODEL, IN_DIM), per_layer),
            pl.BlockSpec((None, MIX_DIM, D_MODEL), per_layer),
            pl.BlockSpec((None, CONV_W, CONV_DIM), per_layer),
            pl.BlockSpec((None, nrow_attn, 1), per_layer),
            pl.BlockSpec((None, group, t, CONV_DIM), per_group),
            pl.BlockSpec((None, group, KV_DIM, WINDOW), per_group),
            pl.BlockSpec((None, group, KV_DIM, WINDOW), per_group),
            pl.BlockSpec(memory_space=pl.ANY),
            pl.BlockSpec(memory_space=pl.ANY),
        ],
        out_specs=[
            pl.BlockSpec((prows, D_MODEL), lambda l, p, ss: (jnp.where(l == DEPTH - 1, p, 0), 0)),
            pl.BlockSpec((None, group, KV_DIM, WINDOW), per_group),
            pl.BlockSpec((None, group, KV_DIM, WINDOW), per_group),
            pl.BlockSpec((None, nrows, CONV_DIM), lambda l, p, ss: (l, p * n_sub + ss, 0)),
        ],
        out_shape=[
            jax.ShapeDtypeStruct((n_seq * t, D_MODEL), F32),
            jax.ShapeDtypeStruct((DEPTH, n_seq, KV_DIM, WINDOW), F32),
            jax.ShapeDtypeStruct((DEPTH, n_seq, KV_DIM, WINDOW), F32),
            jax.ShapeDtypeStruct((DEPTH, n_seq * t, CONV_DIM), F32),
        ],
        scratch_shapes=[
            pltpu.VMEM((n_seq * t, D_MODEL), F32),
            pltpu.VMEM((prows, IN_DIM), F32),
            pltpu.VMEM((prows, MIX_DIM), BF16),
            pltpu.VMEM((MEM_SLOTS, group, MEM_DIM, n_mem), F32),
            pltpu.VMEM((MEM_SLOTS, group, MEM_DIM, n_mem), F32),
            pltpu.SemaphoreType.DMA((2, MEM_SLOTS)),
        ],
        compiler_params=pltpu.CompilerParams(
            dimension_semantics=("arbitrary", "arbitrary", "arbitrary"), vmem_limit_bytes=VMEM_LIMIT),
        name="sample_stream",
    )(x2d, norm_pre, norm_post, w_in_bf, w_out_bf, conv_w, sink_rows, convp, cache_kt, cache_vt, cache_mkt, cache_mvt)


def _keys_minor(a):
    lead = a.shape[:-3]
    n, heads, hd = a.shape[-3:]
    nd = a.ndim
    perm = tuple(range(nd - 3)) + (nd - 2, nd - 1, nd - 3)
    return jnp.transpose(a, perm).reshape(*lead, heads * hd, n)


def _keys_major(a, heads):
    lead = a.shape[:-2]
    n = a.shape[-1]
    a = a.reshape(*lead, heads, HEAD_DIM, n)
    nd = a.ndim
    perm = tuple(range(nd - 3)) + (nd - 1, nd - 3, nd - 2)
    return jnp.transpose(a, perm)


def kernel(x_prompt, x_sample, mem_prompt, cache_win_k, cache_win_v, state_conv, cache_mem_k, cache_mem_v,
           norm_pre, norm_post, norm_mem, w_in, conv_w, attn_sinks, w_mem_kv, w_out):
    batch, seq, _ = x_prompt.shape
    n_seq, t, _ = x_sample.shape
    n_mem = mem_prompt.shape[1]
    assert seq % PROMPT_TILE == 0 and sum(PROMPT_PART_ROWS) == PROMPT_TILE
    assert all(rows % WINDOW == 0 for rows in PROMPT_PART_ROWS)
    assert n_seq % (SAMPLE_GROUP * SAMPLE_SUBSTEPS) == 0 and t == SUBLANES and SAMPLE_GROUP * t <= WINDOW

    sink_rows = jnp.repeat(attn_sinks, t, axis=1)[:, :, None]

    mkt, mvt, kcat_t, vcat, w_in_bf, w_out_bf = _memkv(
        mem_prompt.reshape(batch * n_mem, D_MODEL), norm_mem, w_mem_kv, w_in, w_out, batch, n_mem)

    xp = x_prompt
    ktp, vtp, cvp = [], [], []
    for l in range(DEPTH):
        xp, k_p, v_p, c_p, w_in_bf, w_out_bf = _prompt_layer(
            l, xp, attn_sinks, norm_pre, norm_post, w_in_bf, conv_w, kcat_t, vcat, w_out_bf, w_in, w_out, n_mem)
        ktp.append(k_p)
        vtp.append(v_p)
        cvp.append(c_p[:, SUBLANES - CONV_BUF:, :])

    convp = jnp.pad(state_conv, ((0, 0), (0, 0), (t - CONV_BUF, 0), (0, 0)))
    ys, kts, vts, us = _sample_stream(
        x_sample.reshape(n_seq * t, D_MODEL), norm_pre, norm_post, w_in_bf, w_out_bf, conv_w, sink_rows, convp,
        _keys_minor(cache_win_k), _keys_minor(cache_win_v), _keys_minor(cache_mem_k), _keys_minor(cache_mem_v),
        n_seq, t, n_mem)

    return (xp,
            ys.reshape(n_seq, t, D_MODEL),
            _keys_major(jnp.stack(ktp), KV_HEADS),
            _keys_major(jnp.stack(vtp), KV_HEADS),
            jnp.stack(cvp),
            _keys_major(mkt, MEM_HEADS),
            _keys_major(mvt, MEM_HEADS),
            _keys_major(kts, KV_HEADS),
            _keys_major(vts, KV_HEADS),
            us.reshape(DEPTH, n_seq, t, CONV_DIM)[:, :, t - CONV_BUF:, :])
```

```python
import functools

import jax
import jax.numpy as jnp
from jax import lax
from jax.experimental import pallas as pl
from jax.experimental.pallas import tpu as pltpu

D_MODEL = 1024
DEPTH = 4
HEAD_DIM = 64
ATTN_HEADS = 8
KV_HEADS = 2
ATTN_DIM = ATTN_HEADS * HEAD_DIM
KV_DIM = KV_HEADS * HEAD_DIM
WINDOW = 128
MEM_HEADS = 4
MEM_DIM = MEM_HEADS * HEAD_DIM
CONV_DIM = 256
CONV_W = 3
CONV_BUF = CONV_W - 1
MIX_DIM = CONV_DIM + ATTN_DIM + MEM_DIM
IN_DIM = 4 * CONV_DIM + 2 * ATTN_DIM + 2 * KV_DIM + 2 * MEM_DIM
RMS_EPS = 1e-6
SCALE = HEAD_DIM ** -0.5
LOG2E = 1.4426950408889634

LANES = 128
SUBLANES = 8
VMEM_LIMIT = 56 * 1024 * 1024

OFF_CB, OFF_CC, OFF_CH, OFF_CG = 0, CONV_DIM, 2 * CONV_DIM, 3 * CONV_DIM
OFF_Q = 4 * CONV_DIM
OFF_K = OFF_Q + ATTN_DIM
OFF_V = OFF_K + KV_DIM
OFF_AG = OFF_V + KV_DIM
OFF_MQ = OFF_AG + ATTN_DIM
OFF_MG = OFF_MQ + MEM_DIM
assert OFF_MG + MEM_DIM == IN_DIM and OFF_V == OFF_K + LANES

N_CHUNK = ATTN_DIM // LANES
M_CHUNK = MEM_DIM // LANES
CHUNKS_PER_KV = N_CHUNK // KV_HEADS

PROMPT_TILE = 1024
PROMPT_PART_ROWS = (512, 512)
SAMPLE_GROUP = 8
SAMPLE_SUBSTEPS = 4
MEM_SLOTS = 4

F32 = jnp.float32
BF16 = jnp.bfloat16


def _rmsnorm(x, g):
    r = lax.rsqrt(jnp.mean(x * x, axis=-1, keepdims=True) + RMS_EPS)
    return (x * r) * g


def _silu(x):
    return x * jax.nn.sigmoid(x)


def _dot(a, b):
    return jnp.dot(a, b, preferred_element_type=F32)


def _dot_nt(a, b):
    return lax.dot_general(a, b, (((1,), (1,)), ((), ())), preferred_element_type=F32)


def _low_lanes(shape):
    return lax.broadcasted_iota(jnp.int32, shape, len(shape) - 1) < HEAD_DIM


def _swap_halves(a):
    return pltpu.roll(a, HEAD_DIM, a.ndim - 1)


def _memkv_kernel(mem_ref, g_ref, w_ref, win_ref, wout_ref, mkt_ref, mvt_ref, kcat_t_ref, vcat_ref,
                  win_bf_ref, wout_bf_ref, *, batch, n_mem):
    win_bf_ref[0] = win_ref[...].astype(BF16)
    wout_bf_ref[0] = wout_ref[...].astype(BF16)
    win_bf_ref[1:] = jnp.zeros((DEPTH - 1,) + win_ref.shape, BF16)
    wout_bf_ref[1:] = jnp.zeros((DEPTH - 1,) + wout_ref.shape, BF16)
    h = _rmsnorm(mem_ref[...], g_ref[pl.ds(pl.program_id(0), 1), :]).astype(BF16)
    kv = _dot(h, w_ref[...].astype(BF16))
    mk = kv[:, :MEM_DIM]
    mv = kv[:, MEM_DIM:]
    low = _low_lanes((n_mem, LANES))
    for b in range(batch):
        mkt_ref[b] = mk[b * n_mem:(b + 1) * n_mem, :].T
        mvt_ref[b] = mv[b * n_mem:(b + 1) * n_mem, :].T
        for c in range(M_CHUNK):
            kc = mk[b * n_mem:(b + 1) * n_mem, c * LANES:(c + 1) * LANES]
            vc = mv[b * n_mem:(b + 1) * n_mem, c * LANES:(c + 1) * LANES]
            kcat = jnp.concatenate([jnp.where(low, kc, 0.0), jnp.where(low, 0.0, kc)], axis=0)
            vcat = jnp.concatenate([jnp.where(low, vc, 0.0), jnp.where(low, 0.0, vc)], axis=0)
            kcat_t_ref[b, c] = kcat.T.astype(BF16)
            vcat_ref[b, c] = vcat.astype(BF16)


def _memkv(mem2d, norm_mem, w_mem_kv, w_in, w_out, batch, n_mem):
    rows = batch * n_mem
    chunk = D_MODEL // DEPTH
    return pl.pallas_call(
        functools.partial(_memkv_kernel, batch=batch, n_mem=n_mem),
        grid=(DEPTH,),
        in_specs=[
            pl.BlockSpec((rows, D_MODEL), lambda l: (0, 0)),
            pl.BlockSpec((DEPTH, D_MODEL), lambda l: (0, 0)),
            pl.BlockSpec((None, D_MODEL, 2 * MEM_DIM), lambda l: (l, 0, 0)),
            pl.BlockSpec((None, chunk, IN_DIM), lambda l: (0, l, 0)),
            pl.BlockSpec((None, chunk, D_MODEL), lambda l: (0, l, 0)),
        ],
        out_specs=[
            pl.BlockSpec((None, batch, MEM_DIM, n_mem), lambda l: (l, 0, 0, 0)),
            pl.BlockSpec((None, batch, MEM_DIM, n_mem), lambda l: (l, 0, 0, 0)),
            pl.BlockSpec((None, batch, M_CHUNK, LANES, 2 * n_mem), lambda l: (l, 0, 0, 0, 0)),
            pl.BlockSpec((None, batch, M_CHUNK, 2 * n_mem, LANES), lambda l: (l, 0, 0, 0, 0)),
            pl.BlockSpec((DEPTH, chunk, IN_DIM), lambda l: (0, l, 0)),
            pl.BlockSpec((DEPTH, chunk, D_MODEL), lambda l: (0, l, 0)),
        ],
        out_shape=[
            jax.ShapeDtypeStruct((DEPTH, batch, MEM_DIM, n_mem), F32),
            jax.ShapeDtypeStruct((DEPTH, batch, MEM_DIM, n_mem), F32),
            jax.ShapeDtypeStruct((DEPTH, batch, M_CHUNK, LANES, 2 * n_mem), BF16),
            jax.ShapeDtypeStruct((DEPTH, batch, M_CHUNK, 2 * n_mem, LANES), BF16),
            jax.ShapeDtypeStruct((DEPTH, D_MODEL, IN_DIM), BF16),
            jax.ShapeDtypeStruct((DEPTH, MIX_DIM, D_MODEL), BF16),
        ],
        compiler_params=pltpu.CompilerParams(dimension_semantics=("arbitrary",)),
        name="memkv",
    )(mem2d, norm_mem, w_mem_kv, w_in, w_out)


N_MAIN_IN, N_MAIN_OUT = 9, 4


def _prompt_layer_kernel(*refs, layer, tile, n_mem, n_tiles, n_cast):
    main_in = refs[:N_MAIN_IN]
    cast_in = refs[N_MAIN_IN:N_MAIN_IN + n_cast]
    outs = refs[N_MAIN_IN + n_cast:]
    main_out, cast_out, scratch = outs[:N_MAIN_OUT], outs[N_MAIN_OUT:N_MAIN_OUT + n_cast], outs[N_MAIN_OUT + n_cast:]
    for src, dst in zip(cast_in, cast_out):
        dst[...] = src[...].astype(BF16)
    _prompt_tile(*main_in, *main_out, *scratch, layer=layer, tile=tile, n_mem=n_mem, n_tiles=n_tiles)


def _prompt_tile(sink_ref, x_ref, gpre_ref, gpost_ref, win_ref, convw_ref, kcat_t_ref,
                 vcat_ref, wout_ref,
                 xo_ref, klast_ref, vlast_ref, convlast_ref,
                 kprev_ref, vprev_ref, ubuf_ref, *, layer, tile, n_mem, n_tiles):
    i = pl.program_id(1)
    sizes = PROMPT_PART_ROWS
    starts = [sum(sizes[:p]) for p in range(len(sizes))]
    g_pre = gpre_ref[layer:layer + 1, :]
    g_post = gpost_ref[layer:layer + 1, :]

    @pl.when(i == 0)
    def _():
        kprev_ref[...] = jnp.zeros_like(kprev_ref)
        vprev_ref[...] = jnp.zeros_like(vprev_ref)
        ubuf_ref[0:SUBLANES, :] = jnp.zeros((SUBLANES, CONV_DIM), F32)

    rows = N_CHUNK * WINDOW
    qpos = lax.broadcasted_iota(jnp.int32, (rows, 2 * WINDOW), 0) % WINDOW
    kpos = lax.broadcasted_iota(jnp.int32, (rows, 2 * WINDOW), 1)
    band = (kpos > qpos) & (kpos <= qpos + WINDOW)
    band_first = band & ((kpos >= WINDOW) | (i > 0))
    chunk_of_row = lax.broadcasted_iota(jnp.int32, (rows, 1), 0) // WINDOW
    sink_lo = jnp.zeros((rows, 1), F32)
    sink_hi = jnp.zeros((rows, 1), F32)
    for p in range(N_CHUNK):
        sink_lo = jnp.where(chunk_of_row == p, sink_ref[layer, p] * LOG2E, sink_lo)
        sink_hi = jnp.where(chunk_of_row == p, sink_ref[layer, N_CHUNK + p] * LOG2E, sink_hi)
    low_o = _low_lanes((rows, LANES))
    low_w = _low_lanes((WINDOW, LANES))

    def start(pi):
        x = x_ref[starts[pi]:starts[pi] + sizes[pi], :]
        h = _rmsnorm(x, g_pre).astype(BF16)
        proj = lambda off, width: _dot(h, win_ref[:, off:off + width])
        return dict(x=x, proj=proj, q_raw=proj(OFF_Q, ATTN_DIM), kv=proj(OFF_K, 2 * KV_DIM),
                    mq_raw=proj(OFF_MQ, MEM_DIM))

    def mix_part(pi, st, k_prev, v_prev, fill_one):
        part = sizes[pi]
        n_blocks = part // WINDOW
        low_t = _low_lanes((part, LANES))
        low_kv = _low_lanes((WINDOW + part, KV_DIM))
        k = st["kv"][:, :KV_DIM]
        v = st["kv"][:, KV_DIM:]
        kfull = jnp.concatenate([k_prev, k], axis=0)
        vfull = jnp.concatenate([v_prev, v], axis=0)
        klo = jnp.where(low_kv, kfull, 0.0).astype(BF16)
        khi = jnp.where(low_kv, 0.0, kfull).astype(BF16)
        vlo = jnp.where(low_kv, vfull, 0.0).astype(BF16)
        vhi = jnp.where(low_kv, 0.0, vfull).astype(BF16)

        q = st["q_raw"] * (SCALE * LOG2E)
        nat = [q[:, c * LANES:(c + 1) * LANES] for c in range(N_CHUNK)]
        qp = []
        for p in range(N_CHUNK):
            a, b = nat[p // 2], nat[CHUNKS_PER_KV + p // 2]
            pair = jnp.where(low_t, a, _swap_halves(b)) if p % 2 == 0 else jnp.where(low_t, _swap_halves(a), b)
            qp.append(pair.astype(BF16))

        def swa_scores(j):
            r0 = j * WINDOW
            q_all = jnp.concatenate([qp[p][r0:r0 + WINDOW] for p in range(N_CHUNK)], axis=0)
            k_cat = jnp.concatenate([klo[r0:r0 + 2 * WINDOW], khi[r0:r0 + 2 * WINDOW]], axis=0)
            return _dot_nt(q_all, k_cat)

        o_blocks = []
        fill_one()
        s = swa_scores(0)
        for j in range(n_blocks):
            r0 = j * WINDOW
            fill_one()
            s_next = swa_scores(j + 1) if j + 1 < n_blocks else None
            v_cat = jnp.concatenate([vlo[r0:r0 + 2 * WINDOW], vhi[r0:r0 + 2 * WINDOW]], axis=0)
            mask = band_first if (pi == 0 and j == 0) else band
            s_lo = jnp.where(mask, s[:, :2 * WINDOW], -jnp.inf)
            s_hi = jnp.where(mask, s[:, 2 * WINDOW:], -jnp.inf)
            m_lo = jnp.maximum(jnp.max(s_lo, axis=1, keepdims=True), sink_lo)
            m_hi = jnp.maximum(jnp.max(s_hi, axis=1, keepdims=True), sink_hi)
            p_lo = jnp.exp2(s_lo - m_lo)
            p_hi = jnp.exp2(s_hi - m_hi)
            d_lo = jnp.sum(p_lo, axis=1, keepdims=True) + jnp.exp2(sink_lo - m_lo)
            d_hi = jnp.sum(p_hi, axis=1, keepdims=True) + jnp.exp2(sink_hi - m_hi)
            pr = jnp.concatenate([p_lo, p_hi], axis=1).astype(BF16)
            o = _dot(pr, v_cat)
            o = o * jnp.where(low_o, 1.0 / d_lo, 1.0 / d_hi)
            op = [o[p * WINDOW:(p + 1) * WINDOW] for p in range(N_CHUNK)]
            o_blocks.append(jnp.concatenate(
                [jnp.where(low_w, op[0], _swap_halves(op[1])), jnp.where(low_w, op[2], _swap_halves(op[3])),
                 jnp.where(low_w, _swap_halves(op[0]), op[1]), jnp.where(low_w, _swap_halves(op[2]), op[3])],
                axis=1))
            s = s_next
        o_b = jnp.concatenate(o_blocks, axis=0) if len(o_blocks) > 1 else o_blocks[0]

        mq = (st["mq_raw"] * (SCALE * LOG2E)).astype(BF16)
        oc_chunks = []
        for c in range(M_CHUNK):
            s = _dot(mq[:, c * LANES:(c + 1) * LANES], kcat_t_ref[c])
            fill_one()
            s0 = s[:, :n_mem]
            s1 = s[:, n_mem:]
            p0 = jnp.exp2(s0 - jnp.max(s0, axis=1, keepdims=True))
            p1 = jnp.exp2(s1 - jnp.max(s1, axis=1, keepdims=True))
            d0 = jnp.sum(p0, axis=1, keepdims=True)
            d1 = jnp.sum(p1, axis=1, keepdims=True)
            pm = jnp.concatenate([p0, p1], axis=1).astype(BF16)
            o = _dot(pm, vcat_ref[c])
            oc_chunks.append(o * jnp.where(low_t, 1.0 / d0, 1.0 / d1))
        o_c = jnp.concatenate(oc_chunks, axis=1)
        return o_b, o_c, k[part - WINDOW:part, :], v[part - WINDOW:part, :]

    def gate_mix(pi, filled, o_b, o_c):
        out_b = _silu(jnp.concatenate([filled["ag0"], filled["ag1"]], axis=1)) * o_b
        out_c = _silu(filled["mg"]) * o_c
        u = filled["cc"] * filled["ch"]
        part = sizes[pi]
        base = SUBLANES + starts[pi]
        ubuf_ref[base:base + part, :] = u
        u1 = ubuf_ref[base - 1:base - 1 + part, :]
        u2 = ubuf_ref[base - 2:base - 2 + part, :]
        cw = convw_ref[...]
        conv = cw[0:1, :] * u2 + cw[1:2, :] * u1 + cw[2:3, :] * u
        out_a = _silu(filled["cg"]) * filled["cb"] * conv
        return jnp.concatenate([out_a, out_b, out_c], axis=1).astype(BF16), u[part - SUBLANES:part, :]

    def finish(pi, st, mix):
        part = sizes[pi]
        r0 = starts[pi]
        half = part // 2
        y0 = _dot(mix[:half], wout_ref[...])
        y1 = _dot(mix[half:], wout_ref[...])
        xo_ref[r0:r0 + half, :] = st["x"][:half] + _rmsnorm(y0, g_post)
        xo_ref[r0 + half:r0 + part, :] = st["x"][half:] + _rmsnorm(y1, g_post)

    filler_cols = [("cg", OFF_CG), ("cb", OFF_CB), ("cc", OFF_CC), ("ch", OFF_CH),
                   ("ag0", OFF_AG), ("ag1", OFF_AG + 2 * LANES), ("mg", OFF_MG)]

    states = [start(0)]
    k_prev, v_prev = kprev_ref[...], vprev_ref[...]
    u_tail = None
    for pi in range(len(sizes)):
        st = states[pi]
        pending = list(filler_cols)
        filled = {}

        def fill_one():
            if pending:
                name, off = pending.pop(0)
                filled[name] = st["proj"](off, 2 * LANES)

        if pi + 1 < len(sizes):
            states.append(start(pi + 1))
        o_b, o_c, k_prev, v_prev = mix_part(pi, st, k_prev, v_prev, fill_one)
        while pending:
            fill_one()
        mix, u_tail = gate_mix(pi, filled, o_b, o_c)
        finish(pi, st, mix)

    kprev_ref[...] = k_prev
    vprev_ref[...] = v_prev
    ubuf_ref[0:SUBLANES, :] = u_tail
    convlast_ref[...] = u_tail

    @pl.when(i == n_tiles - 1)
    def _():
        klast_ref[...] = kprev_ref[...].T
        vlast_ref[...] = vprev_ref[...].T


def _prompt_layer(l, x, sinks, norm_pre, norm_post, w_in_bf, conv_w, kcat_t, vcat, w_out_bf, w_in, w_out, n_mem):
    batch, seq, _ = x.shape
    tile = PROMPT_TILE
    nt = seq // tile
    steps = batch * nt
    chunk = D_MODEL // steps
    assert chunk * steps == D_MODEL and chunk % (2 * SUBLANES) == 0
    next_chunk = lambda b, i: (l + 1, b * nt + i, 0)
    casts = []
    if l + 1 < DEPTH:
        casts += [(w_in, (None, chunk, IN_DIM), next_chunk, w_in_bf.shape, w_in_bf),
                  (w_out, (None, chunk, D_MODEL), next_chunk, w_out_bf.shape, w_out_bf)]
    in_specs = [
        pl.BlockSpec(memory_space=pltpu.SMEM),
        pl.BlockSpec((None, tile, D_MODEL), lambda b, i: (b, i, 0)),
        pl.BlockSpec((DEPTH, D_MODEL), lambda b, i: (0, 0)),
        pl.BlockSpec((DEPTH, D_MODEL), lambda b, i: (0, 0)),
        pl.BlockSpec((None, D_MODEL, IN_DIM), lambda b, i: (l, 0, 0)),
        pl.BlockSpec((None, CONV_W, CONV_DIM), lambda b, i: (l, 0, 0)),
        pl.BlockSpec((None, None, M_CHUNK, LANES, 2 * n_mem), lambda b, i: (l, b, 0, 0, 0)),
        pl.BlockSpec((None, None, M_CHUNK, 2 * n_mem, LANES), lambda b, i: (l, b, 0, 0, 0)),
        pl.BlockSpec((None, MIX_DIM, D_MODEL), lambda b, i: (l, 0, 0)),
    ]
    out_specs = [
        pl.BlockSpec((None, tile, D_MODEL), lambda b, i: (b, i, 0)),
        pl.BlockSpec((None, KV_DIM, WINDOW), lambda b, i: (b, 0, 0)),
        pl.BlockSpec((None, KV_DIM, WINDOW), lambda b, i: (b, 0, 0)),
        pl.BlockSpec((None, SUBLANES, CONV_DIM), lambda b, i: (b, 0, 0)),
    ]
    out_shape = [
        jax.ShapeDtypeStruct((batch, seq, D_MODEL), F32),
        jax.ShapeDtypeStruct((batch, KV_DIM, WINDOW), F32),
        jax.ShapeDtypeStruct((batch, KV_DIM, WINDOW), F32),
        jax.ShapeDtypeStruct((batch, SUBLANES, CONV_DIM), F32),
    ]
    operands = [sinks, x, norm_pre, norm_post, w_in_bf, conv_w, kcat_t, vcat, w_out_bf]
    assert len(operands) == N_MAIN_IN and len(out_specs) == N_MAIN_OUT
    aliases = {}
    for n, (src, block, index_map, dst_shape, dst) in enumerate(casts):
        operands.append(src)
        in_specs.append(pl.BlockSpec(block, index_map))
        out_specs.append(pl.BlockSpec(block, index_map))
        out_shape.append(jax.ShapeDtypeStruct(dst_shape, BF16))
        (dst_operand,) = [k for k, op in enumerate(operands[:N_MAIN_IN]) if op is dst]
        aliases[dst_operand] = N_MAIN_OUT + n
    outs = pl.pallas_call(
        functools.partial(_prompt_layer_kernel, layer=l, tile=tile, n_mem=n_mem, n_tiles=nt, n_cast=len(casts)),
        grid=(batch, nt),
        in_specs=in_specs,
        out_specs=out_specs,
        out_shape=out_shape,
        input_output_aliases=aliases,
        scratch_shapes=[
            pltpu.VMEM((WINDOW, KV_DIM), F32),
            pltpu.VMEM((WINDOW, KV_DIM), F32),
            pltpu.VMEM((SUBLANES + tile, CONV_DIM), F32),
        ],
        compiler_params=pltpu.CompilerParams(
            dimension_semantics=("arbitrary", "arbitrary"), vmem_limit_bytes=VMEM_LIMIT),
        name="prompt_layer",
    )(*operands)
    main, cast_out = outs[:N_MAIN_OUT], outs[N_MAIN_OUT:]
    if l + 1 < DEPTH:
        w_in_bf, w_out_bf = cast_out
    return (*main, w_in_bf, w_out_bf)


def _sample_kernel(x_ref, gpre_ref, gpost_ref, win_ref, wout_ref, convw_ref, sink_ref, convp_ref,
                   kt_ref, vt_ref, mkt_ref, mvt_ref,
                   y_ref, kto_ref, vto_ref, uo_ref,
                   xs_ref, z_ref, mix_ref, mk_buf, mv_buf, kt_buf, vt_buf, mem_sem, *, group, t, n_mem, n_sub, n_proj):
    l = pl.program_id(0)
    p = pl.program_id(1)
    ss = pl.program_id(2)
    nrows = group * t
    prows = n_sub * nrows
    rows_p = pl.ds(pl.multiple_of(p * prows, prows), prows)
    rows_s = pl.ds(pl.multiple_of(ss * nrows, nrows), nrows)

    groups_per_layer = n_proj * n_sub
    n_steps = DEPTH * groups_per_layer
    step = (l * n_proj + p) * n_sub + ss

    def mem_copies(at_step):
        layer = at_step // groups_per_layer
        first_seq = (at_step % groups_per_layer) * group
        slot = at_step % MEM_SLOTS
        streams = ((mkt_ref, mk_buf), (mvt_ref, mv_buf), (kt_ref, kt_buf), (vt_ref, vt_buf))
        return [pltpu.make_async_copy(src.at[layer, pl.ds(first_seq, group)], dst.at[slot], mem_sem.at[n, slot])
                for n, (src, dst) in enumerate(streams)]

    @pl.when(step == 0)
    def _():
        for ahead in range(MEM_SLOTS - 1):
            for copy in mem_copies(ahead):
                copy.start()

    @pl.when(step + MEM_SLOTS - 1 < n_steps)
    def _():
        for copy in mem_copies(step + MEM_SLOTS - 1):
            copy.start()

    for copy in mem_copies(step):
        copy.wait()
    mem_slot = step % MEM_SLOTS

    @pl.when(ss == 0)
    def _():
        @pl.when(l == 0)
        def _():
            xs_ref[rows_p, :] = x_ref[...]

        h = _rmsnorm(xs_ref[rows_p, :], gpre_ref[pl.ds(l, 1), :]).astype(BF16)
        z_ref[...] = _dot(h, win_ref[...])

    def zc(off, width):
        return z_ref[rows_s, off:off + width]

    def per_seq(a):
        return a.reshape(group, t, a.shape[-1])

    u2d = zc(OFF_CC, CONV_DIM) * zc(OFF_CH, CONV_DIM)
    uo_ref[...] = u2d
    u = per_seq(u2d)
    prev = convp_ref[...]
    tpos = lax.broadcasted_iota(jnp.int32, (group, t, CONV_DIM), 1)
    u1 = jnp.where(tpos >= 1, pltpu.roll(u, 1, 1), pltpu.roll(prev, 1, 1))
    u2 = jnp.where(tpos >= 2, pltpu.roll(u, 2, 1), pltpu.roll(prev, 2, 1))
    cw = convw_ref[...]
    conv = cw[0:1, :] * u2 + cw[1:2, :] * u1 + cw[2:3, :] * u
    out_a = per_seq(_silu(zc(OFF_CG, CONV_DIM)) * zc(OFF_CB, CONV_DIM)) * conv

    kt_old = kt_buf[mem_slot]
    vt_old = vt_buf[mem_slot]
    k_new = zc(OFF_K, KV_DIM)
    v_new = zc(OFF_V, KV_DIM)
    k_new_t = jnp.swapaxes(per_seq(k_new), 1, 2)
    v_new_t = jnp.swapaxes(per_seq(v_new), 1, 2)
    kto_ref[...] = pltpu.roll(jnp.concatenate([k_new_t, kt_old[:, :, t:]], axis=2), WINDOW - t, 2)
    vto_ref[...] = pltpu.roll(jnp.concatenate([v_new_t, vt_old[:, :, t:]], axis=2), WINDOW - t, 2)

    q = per_seq(zc(OFF_Q, ATTN_DIM) * SCALE)
    low_q = _low_lanes((group, t, LANES))
    pieces = []
    for c in range(N_CHUNK):
        qc = q[:, :, c * LANES:(c + 1) * LANES]
        qs = _swap_halves(qc)
        if c // CHUNKS_PER_KV == 0:
            pieces += [jnp.where(low_q, qc, 0.0), jnp.where(low_q, qs, 0.0)]
        else:
            pieces += [jnp.where(low_q, 0.0, qs), jnp.where(low_q, 0.0, qc)]
    qbd = jnp.concatenate(pieces, axis=1).astype(BF16)
    nrow = ATTN_HEADS * t
    s_old = jnp.einsum('gqd,gdk->gqk', qbd, kt_old.astype(BF16), preferred_element_type=F32)
    pad_rows = jnp.zeros((WINDOW - group * t, KV_DIM), F32)
    k_new_w = jnp.concatenate([k_new, pad_rows], axis=0) if group * t < WINDOW else k_new
    v_new_w = jnp.concatenate([v_new, pad_rows], axis=0) if group * t < WINDOW else v_new
    s_new = _dot_nt(qbd.reshape(group * nrow, KV_DIM), k_new_w.astype(BF16)).reshape(group, nrow, WINDOW)
    tq = lax.broadcasted_iota(jnp.int32, (group, nrow, WINDOW), 1) % t
    col = lax.broadcasted_iota(jnp.int32, (group, nrow, WINDOW), 2)
    first_new = lax.broadcasted_iota(jnp.int32, (group, nrow, WINDOW), 0) * t
    s_old = jnp.where(col > tq, s_old, -jnp.inf)
    s_new = jnp.where((col >= first_new) & (col <= first_new + tq), s_new, -jnp.inf)
    sink = sink_ref[...]
    m = jnp.maximum(jnp.max(jnp.maximum(s_old, s_new), axis=2, keepdims=True), sink)
    p_old = jnp.exp(s_old - m)
    p_new = jnp.exp(s_new - m)
    d = jnp.sum(p_old + p_new, axis=2, keepdims=True) + jnp.exp(sink - m)
    o = jnp.einsum('gqk,gdk->gqd', p_old.astype(BF16), vt_old.astype(BF16), preferred_element_type=F32)
    o = o + _dot(p_new.astype(BF16).reshape(group * nrow, WINDOW), v_new_w.astype(BF16)).reshape(group, nrow, KV_DIM)
    o = o * (1.0 / d)
    ob_chunks = []
    for c in range(N_CHUNK):
        o_even = o[:, 2 * c * t:(2 * c + 1) * t]
        o_odd = o[:, (2 * c + 1) * t:(2 * c + 2) * t]
        if c // CHUNKS_PER_KV == 0:
            ob_chunks.append(jnp.where(low_q, o_even, _swap_halves(o_odd)))
        else:
            ob_chunks.append(jnp.where(low_q, _swap_halves(o_even), o_odd))
    o_b = jnp.concatenate(ob_chunks, axis=2)
    out_b = per_seq(_silu(zc(OFF_AG, ATTN_DIM))) * o_b

    mq = per_seq(zc(OFF_MQ, MEM_DIM) * SCALE)
    head_of_lane = lax.broadcasted_iota(jnp.int32, (group, t, MEM_DIM), 2) // HEAD_DIM
    qm = jnp.concatenate([jnp.where(head_of_lane == hh, mq, 0.0) for hh in range(MEM_HEADS)],
                         axis=1).astype(BF16)
    s = jnp.einsum('gqd,gdk->gqk', qm, mk_buf[mem_slot].astype(BF16), preferred_element_type=F32)
    p = jnp.exp(s - jnp.max(s, axis=2, keepdims=True))
    d = jnp.sum(p, axis=2, keepdims=True)
    o = jnp.einsum('gqk,gdk->gqd', p.astype(BF16), mv_buf[mem_slot].astype(BF16), preferred_element_type=F32)
    o = o * (1.0 / d)
    o_c = jnp.zeros((group, t, MEM_DIM), F32)
    for hh in range(MEM_HEADS):
        o_c = jnp.where(head_of_lane == hh, o[:, hh * t:(hh + 1) * t], o_c)
    out_c = per_seq(_silu(zc(OFF_MG, MEM_DIM))) * o_c

    mix_ref[rows_s, :] = jnp.concatenate([out_a, out_b, out_c], axis=2).reshape(nrows, MIX_DIM).astype(BF16)

    @pl.when(ss == n_sub - 1)
    def _():
        y = _dot(mix_ref[...], wout_ref[...])
        x_new = xs_ref[rows_p, :] + _rmsnorm(y, gpost_ref[pl.ds(l, 1), :])
        xs_ref[rows_p, :] = x_new

        @pl.when(l == DEPTH - 1)
        def _():
            y_ref[...] = x_new


def _sample_stream(x2d, norm_pre, norm_post, w_in_bf, w_out_bf, conv_w, sink_rows, convp, cache_kt, cache_vt,
                   cache_mkt, cache_mvt, n_seq, t, n_mem):
    group = SAMPLE_GROUP
    n_sub = SAMPLE_SUBSTEPS
    n_proj = n_seq // (group * n_sub)
    nrows = group * t
    prows = n_sub * nrows
    nrow_attn = ATTN_HEADS * t
    per_layer = lambda l, p, ss: (l, 0, 0)
    per_group = lambda l, p, ss: (l, p * n_sub + ss, 0, 0)
    return pl.pallas_call(
        functools.partial(_sample_kernel, group=group, t=t, n_mem=n_mem, n_sub=n_sub, n_proj=n_proj),
        grid=(DEPTH, n_proj, n_sub),
        in_specs=[
            pl.BlockSpec((prows, D_MODEL), lambda l, p, ss: (jnp.where(l == 0, p, n_proj - 1), 0)),
            pl.BlockSpec((DEPTH, D_MODEL), lambda l, p, ss: (0, 0)),
            pl.BlockSpec((DEPTH, D_MODEL), lambda l, p, ss: (0, 0)),
            pl.BlockSpec((None, D_MODEL, IN_DIM), per_layer),
            pl.BlockSpec((None, MIX_DIM, D_MODEL), per_layer),
            pl.BlockSpec((None, CONV_W, CONV_DIM), per_layer),
            pl.BlockSpec((None, nrow_attn, 1), per_layer),
            pl.BlockSpec((None, group, t, CONV_DIM), per_group),
            pl.BlockSpec(memory_space=pl.ANY),
            pl.BlockSpec(memory_space=pl.ANY),
            pl.BlockSpec(memory_space=pl.ANY),
            pl.BlockSpec(memory_space=pl.ANY),
        ],
        out_specs=[
            pl.BlockSpec((prows, D_MODEL), lambda l, p, ss: (jnp.where(l == DEPTH - 1, p, 0), 0)),
            pl.BlockSpec((None, group, KV_DIM, WINDOW), per_group),
            pl.BlockSpec((None, group, KV_DIM, WINDOW), per_group),
            pl.BlockSpec((None, nrows, CONV_DIM), lambda l, p, ss: (l, p * n_sub + ss, 0)),
        ],
        out_shape=[
            jax.ShapeDtypeStruct((n_seq * t, D_MODEL), F32),
            jax.ShapeDtypeStruct((DEPTH, n_seq, KV_DIM, WINDOW), F32),
            jax.ShapeDtypeStruct((DEPTH, n_seq, KV_DIM, WINDOW), F32),
            jax.ShapeDtypeStruct((DEPTH, n_seq * t, CONV_DIM), F32),
        ],
        scratch_shapes=[
            pltpu.VMEM((n_seq * t, D_MODEL), F32),
            pltpu.VMEM((prows, IN_DIM), F32),
            pltpu.VMEM((prows, MIX_DIM), BF16),
            pltpu.VMEM((MEM_SLOTS, group, MEM_DIM, n_mem), F32),
            pltpu.VMEM((MEM_SLOTS, group, MEM_DIM, n_mem), F32),
            pltpu.VMEM((MEM_SLOTS, group, KV_DIM, WINDOW), F32),
            pltpu.VMEM((MEM_SLOTS, group, KV_DIM, WINDOW), F32),
            pltpu.SemaphoreType.DMA((4, MEM_SLOTS)),
        ],
        compiler_params=pltpu.CompilerParams(
            dimension_semantics=("arbitrary", "arbitrary", "arbitrary"), vmem_limit_bytes=VMEM_LIMIT),
        name="sample_stream",
    )(x2d, norm_pre, norm_post, w_in_bf, w_out_bf, conv_w, sink_rows, convp, cache_kt, cache_vt, cache_mkt, cache_mvt)


def _keys_minor(a):
    lead = a.shape[:-3]
    n, heads, hd = a.shape[-3:]
    nd = a.ndim
    perm = tuple(range(nd - 3)) + (nd - 2, nd - 1, nd - 3)
    return jnp.transpose(a, perm).reshape(*lead, heads * hd, n)


def _keys_major(a, heads):
    lead = a.shape[:-2]
    n = a.shape[-1]
    a = a.reshape(*lead, heads, HEAD_DIM, n)
    nd = a.ndim
    perm = tuple(range(nd - 3)) + (nd - 1, nd - 3, nd - 2)
    return jnp.transpose(a, perm)


def kernel(x_prompt, x_sample, mem_prompt, cache_win_k, cache_win_v, state_conv, cache_mem_k, cache_mem_v,
           norm_pre, norm_post, norm_mem, w_in, conv_w, attn_sinks, w_mem_kv, w_out):
    batch, seq, _ = x_prompt.shape
    n_seq, t, _ = x_sample.shape
    n_mem = mem_prompt.shape[1]
    assert seq % PROMPT_TILE == 0 and sum(PROMPT_PART_ROWS) == PROMPT_TILE
    assert all(rows % WINDOW == 0 for rows in PROMPT_PART_ROWS)
    assert n_seq % (SAMPLE_GROUP * SAMPLE_SUBSTEPS) == 0 and t == SUBLANES and SAMPLE_GROUP * t <= WINDOW

    sink_rows = jnp.repeat(attn_sinks, t, axis=1)[:, :, None]

    mkt, mvt, kcat_t, vcat, w_in_bf, w_out_bf = _memkv(
        mem_prompt.reshape(batch * n_mem, D_MODEL), norm_mem, w_mem_kv, w_in, w_out, batch, n_mem)

    xp = x_prompt
    ktp, vtp, cvp = [], [], []
    for l in range(DEPTH):
        xp, k_p, v_p, c_p, w_in_bf, w_out_bf = _prompt_layer(
            l, xp, attn_sinks, norm_pre, norm_post, w_in_bf, conv_w, kcat_t, vcat, w_out_bf, w_in, w_out, n_mem)
        ktp.append(k_p)
        vtp.append(v_p)
        cvp.append(c_p[:, SUBLANES - CONV_BUF:, :])

    convp = jnp.pad(state_conv, ((0, 0), (0, 0), (t - CONV_BUF, 0), (0, 0)))
    ys, kts, vts, us = _sample_stream(
        x_sample.reshape(n_seq * t, D_MODEL), norm_pre, norm_post, w_in_bf, w_out_bf, conv_w, sink_rows, convp,
        _keys_minor(cache_win_k), _keys_minor(cache_win_v), _keys_minor(cache_mem_k), _keys_minor(cache_mem_v),
        n_seq, t, n_mem)

    return (xp,
            ys.reshape(n_seq, t, D_MODEL),
            _keys_major(jnp.stack(ktp), KV_HEADS),
            _keys_major(jnp.stack(vtp), KV_HEADS),
            jnp.stack(cvp),
            _keys_major(mkt, MEM_HEADS),
            _keys_major(mvt, MEM_HEADS),
            _keys_major(kts, KV_HEADS),
            _keys_major(vts, KV_HEADS),
            us.reshape(DEPTH, n_seq, t, CONV_DIM)[:, :, t - CONV_BUF:, :])
```

```python
import functools

import jax
import jax.numpy as jnp
from jax import lax
from jax.experimental import pallas as pl
from jax.experimental.pallas import tpu as pltpu

D_MODEL = 1024
DEPTH = 4
HEAD_DIM = 64
ATTN_HEADS = 8
KV_HEADS = 2
ATTN_DIM = ATTN_HEADS * HEAD_DIM
KV_DIM = KV_HEADS * HEAD_DIM
WINDOW = 128
MEM_HEADS = 4
MEM_DIM = MEM_HEADS * HEAD_DIM
CONV_DIM = 256
CONV_W = 3
CONV_BUF = CONV_W - 1
MIX_DIM = CONV_DIM + ATTN_DIM + MEM_DIM
IN_DIM = 4 * CONV_DIM + 2 * ATTN_DIM + 2 * KV_DIM + 2 * MEM_DIM
RMS_EPS = 1e-6
SCALE = HEAD_DIM ** -0.5
LOG2E = 1.4426950408889634

LANES = 128
SUBLANES = 8
VMEM_LIMIT = 56 * 1024 * 1024

OFF_CB, OFF_CC, OFF_CH, OFF_CG = 0, CONV_DIM, 2 * CONV_DIM, 3 * CONV_DIM
OFF_Q = 4 * CONV_DIM
OFF_K = OFF_Q + ATTN_DIM
OFF_V = OFF_K + KV_DIM
OFF_AG = OFF_V + KV_DIM
OFF_MQ = OFF_AG + ATTN_DIM
OFF_MG = OFF_MQ + MEM_DIM
assert OFF_MG + MEM_DIM == IN_DIM and OFF_V == OFF_K + LANES

N_CHUNK = ATTN_DIM // LANES
M_CHUNK = MEM_DIM // LANES
CHUNKS_PER_KV = N_CHUNK // KV_HEADS

PROMPT_TILE = 1024
PROMPT_PART_ROWS = (512, 512)
SAMPLE_GROUP = 8
SAMPLE_SUBSTEPS = 4
MEM_SLOTS = 4

F32 = jnp.float32
BF16 = jnp.bfloat16


def _rmsnorm(x, g):
    r = lax.rsqrt(jnp.mean(x * x, axis=-1, keepdims=True) + RMS_EPS)
    return (x * r) * g


def _silu(x):
    return x * jax.nn.sigmoid(x)


def _dot(a, b):
    return jnp.dot(a, b, preferred_element_type=F32)


def _dot_nt(a, b):
    return lax.dot_general(a, b, (((1,), (1,)), ((), ())), preferred_element_type=F32)


def _low_lanes(shape):
    return lax.broadcasted_iota(jnp.int32, shape, len(shape) - 1) < HEAD_DIM


def _swap_halves(a):
    return pltpu.roll(a, HEAD_DIM, a.ndim - 1)


def _memkv_kernel(mem_ref, g_ref, w_ref, win_ref, wout_ref, mkt_ref, mvt_ref, kcat_t_ref, vcat_ref,
                  win_bf_ref, wout_bf_ref, *, batch, n_mem):
    win_bf_ref[0] = win_ref[...].astype(BF16)
    wout_bf_ref[0] = wout_ref[...].astype(BF16)
    win_bf_ref[1:] = jnp.zeros((DEPTH - 1,) + win_ref.shape, BF16)
    wout_bf_ref[1:] = jnp.zeros((DEPTH - 1,) + wout_ref.shape, BF16)
    h = _rmsnorm(mem_ref[...], g_ref[pl.ds(pl.program_id(0), 1), :]).astype(BF16)
    kv = _dot(h, w_ref[...].astype(BF16))
    mk = kv[:, :MEM_DIM]
    mv = kv[:, MEM_DIM:]
    low = _low_lanes((n_mem, LANES))
    for b in range(batch):
        mkt_ref[b] = mk[b * n_mem:(b + 1) * n_mem, :].T
        mvt_ref[b] = mv[b * n_mem:(b + 1) * n_mem, :].T
        for c in range(M_CHUNK):
            kc = mk[b * n_mem:(b + 1) * n_mem, c * LANES:(c + 1) * LANES]
            vc = mv[b * n_mem:(b + 1) * n_mem, c * LANES:(c + 1) * LANES]
            kcat = jnp.concatenate([jnp.where(low, kc, 0.0), jnp.where(low, 0.0, kc)], axis=0)
            vcat = jnp.concatenate([jnp.where(low, vc, 0.0), jnp.where(low, 0.0, vc)], axis=0)
            kcat_t_ref[b, c] = kcat.T.astype(BF16)
            vcat_ref[b, c] = vcat.astype(BF16)


def _memkv(mem2d, norm_mem, w_mem_kv, w_in, w_out, batch, n_mem):
    rows = batch * n_mem
    chunk = D_MODEL // DEPTH
    return pl.pallas_call(
        functools.partial(_memkv_kernel, batch=batch, n_mem=n_mem),
        grid=(DEPTH,),
        in_specs=[
            pl.BlockSpec((rows, D_MODEL), lambda l: (0, 0)),
            pl.BlockSpec((DEPTH, D_MODEL), lambda l: (0, 0)),
            pl.BlockSpec((None, D_MODEL, 2 * MEM_DIM), lambda l: (l, 0, 0)),
            pl.BlockSpec((None, chunk, IN_DIM), lambda l: (0, l, 0)),
            pl.BlockSpec((None, chunk, D_MODEL), lambda l: (0, l, 0)),
        ],
        out_specs=[
            pl.BlockSpec((None, batch, MEM_DIM, n_mem), lambda l: (l, 0, 0, 0)),
            pl.BlockSpec((None, batch, MEM_DIM, n_mem), lambda l: (l, 0, 0, 0)),
            pl.BlockSpec((None, batch, M_CHUNK, LANES, 2 * n_mem), lambda l: (l, 0, 0, 0, 0)),
            pl.BlockSpec((None, batch, M_CHUNK, 2 * n_mem, LANES), lambda l: (l, 0, 0, 0, 0)),
            pl.BlockSpec((DEPTH, chunk, IN_DIM), lambda l: (0, l, 0)),
            pl.BlockSpec((DEPTH, chunk, D_MODEL), lambda l: (0, l, 0)),
        ],
        out_shape=[
            jax.ShapeDtypeStruct((DEPTH, batch, MEM_DIM, n_mem), F32),
            jax.ShapeDtypeStruct((DEPTH, batch, MEM_DIM, n_mem), F32),
            jax.ShapeDtypeStruct((DEPTH, batch, M_CHUNK, LANES, 2 * n_mem), BF16),
            jax.ShapeDtypeStruct((DEPTH, batch, M_CHUNK, 2 * n_mem, LANES), BF16),
            jax.ShapeDtypeStruct((DEPTH, D_MODEL, IN_DIM), BF16),
            jax.ShapeDtypeStruct((DEPTH, MIX_DIM, D_MODEL), BF16),
        ],
        compiler_params=pltpu.CompilerParams(dimension_semantics=("arbitrary",)),
        name="memkv",
    )(mem2d, norm_mem, w_mem_kv, w_in, w_out)


N_MAIN_IN, N_MAIN_OUT = 9, 4


def _prompt_layer_kernel(*refs, layer, tile, n_mem, n_tiles, n_cast):
    main_in = refs[:N_MAIN_IN]
    cast_in = refs[N_MAIN_IN:N_MAIN_IN + n_cast]
    outs = refs[N_MAIN_IN + n_cast:]
    main_out, cast_out, scratch = outs[:N_MAIN_OUT], outs[N_MAIN_OUT:N_MAIN_OUT + n_cast], outs[N_MAIN_OUT + n_cast:]
    for src, dst in zip(cast_in, cast_out):
        dst[...] = src[...].astype(BF16)
    _prompt_tile(*main_in, *main_out, *scratch, layer=layer, tile=tile, n_mem=n_mem, n_tiles=n_tiles)


def _prompt_tile(sink_ref, x_ref, gpre_ref, gpost_ref, win_ref, convw_ref, kcat_t_ref,
                 vcat_ref, wout_ref,
                 xo_ref, klast_ref, vlast_ref, convlast_ref,
                 kprev_ref, vprev_ref, ubuf_ref, *, layer, tile, n_mem, n_tiles):
    i = pl.program_id(1)
    sizes = PROMPT_PART_ROWS
    starts = [sum(sizes[:p]) for p in range(len(sizes))]
    g_pre = gpre_ref[layer:layer + 1, :]
    g_post = gpost_ref[layer:layer + 1, :]

    @pl.when(i == 0)
    def _():
        kprev_ref[...] = jnp.zeros_like(kprev_ref)
        vprev_ref[...] = jnp.zeros_like(vprev_ref)
        ubuf_ref[0:SUBLANES, :] = jnp.zeros((SUBLANES, CONV_DIM), F32)

    rows = N_CHUNK * WINDOW
    qpos = lax.broadcasted_iota(jnp.int32, (rows, 2 * WINDOW), 0) % WINDOW
    kpos = lax.broadcasted_iota(jnp.int32, (rows, 2 * WINDOW), 1)
    band = (kpos > qpos) & (kpos <= qpos + WINDOW)
    band_first = band & ((kpos >= WINDOW) | (i > 0))
    chunk_of_row = lax.broadcasted_iota(jnp.int32, (rows, 1), 0) // WINDOW
    sink_lo = jnp.zeros((rows, 1), F32)
    sink_hi = jnp.zeros((rows, 1), F32)
    for p in range(N_CHUNK):
        sink_lo = jnp.where(chunk_of_row == p, sink_ref[layer, p] * LOG2E, sink_lo)
        sink_hi = jnp.where(chunk_of_row == p, sink_ref[layer, N_CHUNK + p] * LOG2E, sink_hi)
    low_o = _low_lanes((rows, LANES))
    low_w = _low_lanes((WINDOW, LANES))

    def start(pi):
        x = x_ref[starts[pi]:starts[pi] + sizes[pi], :]
        h = _rmsnorm(x, g_pre).astype(BF16)
        proj = lambda off, width: _dot(h, win_ref[:, off:off + width])
        return dict(x=x, proj=proj, q_raw=proj(OFF_Q, ATTN_DIM), kv=proj(OFF_K, 2 * KV_DIM),
                    mq_raw=proj(OFF_MQ, MEM_DIM))

    def mix_part(pi, st, k_prev, v_prev, fill_one):
        part = sizes[pi]
        n_blocks = part // WINDOW
        low_t = _low_lanes((part, LANES))
        low_kv = _low_lanes((WINDOW + part, KV_DIM))
        k = st["kv"][:, :KV_DIM]
        v = st["kv"][:, KV_DIM:]
        kfull = jnp.concatenate([k_prev, k], axis=0)
        vfull = jnp.concatenate([v_prev, v], axis=0)
        klo = jnp.where(low_kv, kfull, 0.0).astype(BF16)
        khi = jnp.where(low_kv, 0.0, kfull).astype(BF16)
        vlo = jnp.where(low_kv, vfull, 0.0).astype(BF16)
        vhi = jnp.where(low_kv, 0.0, vfull).astype(BF16)

        q = st["q_raw"] * (SCALE * LOG2E)
        nat = [q[:, c * LANES:(c + 1) * LANES] for c in range(N_CHUNK)]
        qp = []
        for p in range(N_CHUNK):
            a, b = nat[p // 2], nat[CHUNKS_PER_KV + p // 2]
            pair = jnp.where(low_t, a, _swap_halves(b)) if p % 2 == 0 else jnp.where(low_t, _swap_halves(a), b)
            qp.append(pair.astype(BF16))

        def swa_scores(j):
            r0 = j * WINDOW
            q_all = jnp.concatenate([qp[p][r0:r0 + WINDOW] for p in range(N_CHUNK)], axis=0)
            k_cat = jnp.concatenate([klo[r0:r0 + 2 * WINDOW], khi[r0:r0 + 2 * WINDOW]], axis=0)
            return _dot_nt(q_all, k_cat)

        o_blocks = []
        fill_one()
        s = swa_scores(0)
        for j in range(n_blocks):
            r0 = j * WINDOW
            fill_one()
            s_next = swa_scores(j + 1) if j + 1 < n_blocks else None
            v_cat = jnp.concatenate([vlo[r0:r0 + 2 * WINDOW], vhi[r0:r0 + 2 * WINDOW]], axis=0)
            mask = band_first if (pi == 0 and j == 0) else band
            s_lo = jnp.where(mask, s[:, :2 * WINDOW], -jnp.inf)
            s_hi = jnp.where(mask, s[:, 2 * WINDOW:], -jnp.inf)
            m_lo = jnp.maximum(jnp.max(s_lo, axis=1, keepdims=True), sink_lo)
            m_hi = jnp.maximum(jnp.max(s_hi, axis=1, keepdims=True), sink_hi)
            p_lo = jnp.exp2(s_lo - m_lo)
            p_hi = jnp.exp2(s_hi - m_hi)
            d_lo = jnp.sum(p_lo, axis=1, keepdims=True) + jnp.exp2(sink_lo - m_lo)
            d_hi = jnp.sum(p_hi, axis=1, keepdims=True) + jnp.exp2(sink_hi - m_hi)
            pr = jnp.concatenate([p_lo, p_hi], axis=1).astype(BF16)
            o = _dot(pr, v_cat)
            o = o * jnp.where(low_o, 1.0 / d_lo, 1.0 / d_hi)
            op = [o[p * WINDOW:(p + 1) * WINDOW] for p in range(N_CHUNK)]
            o_blocks.append(jnp.concatenate(
                [jnp.where(low_w, op[0], _swap_halves(op[1])), jnp.where(low_w, op[2], _swap_halves(op[3])),
                 jnp.where(low_w, _swap_halves(op[0]), op[1]), jnp.where(low_w, _swap_halves(op[2]), op[3])],
                axis=1))
            s = s_next
        o_b = jnp.concatenate(o_blocks, axis=0) if len(o_blocks) > 1 else o_blocks[0]

        mq = (st["mq_raw"] * (SCALE * LOG2E)).astype(BF16)
        oc_chunks = []
        for c in range(M_CHUNK):
            s = _dot(mq[:, c * LANES:(c + 1) * LANES], kcat_t_ref[c])
            fill_one()
            s0 = s[:, :n_mem]
            s1 = s[:, n_mem:]
            p0 = jnp.exp2(s0 - jnp.max(s0, axis=1, keepdims=True))
            p1 = jnp.exp2(s1 - jnp.max(s1, axis=1, keepdims=True))
            d0 = jnp.sum(p0, axis=1, keepdims=True)
            d1 = jnp.sum(p1, axis=1, keepdims=True)
            pm = jnp.concatenate([p0, p1], axis=1).astype(BF16)
            o = _dot(pm, vcat_ref[c])
            oc_chunks.append(o * jnp.where(low_t, 1.0 / d0, 1.0 / d1))
        o_c = jnp.concatenate(oc_chunks, axis=1)
        return o_b, o_c, k[part - WINDOW:part, :], v[part - WINDOW:part, :]

    def gate_mix(pi, filled, o_b, o_c):
        out_b = _silu(jnp.concatenate([filled["ag0"], filled["ag1"]], axis=1)) * o_b
        out_c = _silu(filled["mg"]) * o_c
        u = filled["cc"] * filled["ch"]
        part = sizes[pi]
        base = SUBLANES + starts[pi]
        ubuf_ref[base:base + part, :] = u
        u1 = ubuf_ref[base - 1:base - 1 + part, :]
        u2 = ubuf_ref[base - 2:base - 2 + part, :]
        cw = convw_ref[...]
        conv = cw[0:1, :] * u2 + cw[1:2, :] * u1 + cw[2:3, :] * u
        out_a = _silu(filled["cg"]) * filled["cb"] * conv
        return jnp.concatenate([out_a, out_b, out_c], axis=1).astype(BF16), u[part - SUBLANES:part, :]

    def finish(pi, st, mix):
        part = sizes[pi]
        r0 = starts[pi]
        half = part // 2
        y0 = _dot(mix[:half], wout_ref[...])
        y1 = _dot(mix[half:], wout_ref[...])
        xo_ref[r0:r0 + half, :] = st["x"][:half] + _rmsnorm(y0, g_post)
        xo_ref[r0 + half:r0 + part, :] = st["x"][half:] + _rmsnorm(y1, g_post)

    filler_cols = [("cg", OFF_CG), ("cb", OFF_CB), ("cc", OFF_CC), ("ch", OFF_CH),
                   ("ag0", OFF_AG), ("ag1", OFF_AG + 2 * LANES), ("mg", OFF_MG)]

    states = [start(0)]
    k_prev, v_prev = kprev_ref[...], vprev_ref[...]
    u_tail = None
    for pi in range(len(sizes)):
        st = states[pi]
        pending = list(filler_cols)
        filled = {}

        def fill_one():
            if pending:
                name, off = pending.pop(0)
                filled[name] = st["proj"](off, 2 * LANES)

        if pi + 1 < len(sizes):
            states.append(start(pi + 1))
        o_b, o_c, k_prev, v_prev = mix_part(pi, st, k_prev, v_prev, fill_one)
        while pending:
            fill_one()
        mix, u_tail = gate_mix(pi, filled, o_b, o_c)
        finish(pi, st, mix)

    kprev_ref[...] = k_prev
    vprev_ref[...] = v_prev
    ubuf_ref[0:SUBLANES, :] = u_tail
    convlast_ref[...] = u_tail

    @pl.when(i == n_tiles - 1)
    def _():
        klast_ref[...] = kprev_ref[...].T
        vlast_ref[...] = vprev_ref[...].T


def _prompt_layer(l, x, sinks, norm_pre, norm_post, w_in_bf, conv_w, kcat_t, vcat, w_out_bf, w_in, w_out, n_mem):
    batch, seq, _ = x.shape
    tile = PROMPT_TILE
    nt = seq // tile
    steps = batch * nt
    chunk = D_MODEL // steps
    assert chunk * steps == D_MODEL and chunk % (2 * SUBLANES) == 0
    next_chunk = lambda b, i: (l + 1, b * nt + i, 0)
    casts = []
    if l + 1 < DEPTH:
        casts += [(w_in, (None, chunk, IN_DIM), next_chunk, w_in_bf.shape, w_in_bf),
                  (w_out, (None, chunk, D_MODEL), next_chunk, w_out_bf.shape, w_out_bf)]
    in_specs = [
        pl.BlockSpec(memory_space=pltpu.SMEM),
        pl.BlockSpec((None, tile, D_MODEL), lambda b, i: (b, i, 0)),
        pl.BlockSpec((DEPTH, D_MODEL), lambda b, i: (0, 0)),
        pl.BlockSpec((DEPTH, D_MODEL), lambda b, i: (0, 0)),
        pl.BlockSpec((None, D_MODEL, IN_DIM), lambda b, i: (l, 0, 0)),
        pl.BlockSpec((None, CONV_W, CONV_DIM), lambda b, i: (l, 0, 0)),
        pl.BlockSpec((None, None, M_CHUNK, LANES, 2 * n_mem), lambda b, i: (l, b, 0, 0, 0)),
        pl.BlockSpec((None, None, M_CHUNK, 2 * n_mem, LANES), lambda b, i: (l, b, 0, 0, 0)),
        pl.BlockSpec((None, MIX_DIM, D_MODEL), lambda b, i: (l, 0, 0)),
    ]
    out_specs = [
        pl.BlockSpec((None, tile, D_MODEL), lambda b, i: (b, i, 0)),
        pl.BlockSpec((None, KV_DIM, WINDOW), lambda b, i: (b, 0, 0)),
        pl.BlockSpec((None, KV_DIM, WINDOW), lambda b, i: (b, 0, 0)),
        pl.BlockSpec((None, SUBLANES, CONV_DIM), lambda b, i: (b, 0, 0)),
    ]
    out_shape = [
        jax.ShapeDtypeStruct((batch, seq, D_MODEL), F32),
        jax.ShapeDtypeStruct((batch, KV_DIM, WINDOW), F32),
        jax.ShapeDtypeStruct((batch, KV_DIM, WINDOW), F32),
        jax.ShapeDtypeStruct((batch, SUBLANES, CONV_DIM), F32),
    ]
    operands = [sinks, x, norm_pre, norm_post, w_in_bf, conv_w, kcat_t, vcat, w_out_bf]
    assert len(operands) == N_MAIN_IN and len(out_specs) == N_MAIN_OUT
    aliases = {}
    for n, (src, block, index_map, dst_shape, dst) in enumerate(casts):
        operands.append(src)
        in_specs.append(pl.BlockSpec(block, index_map))
        out_specs.append(pl.BlockSpec(block, index_map))
        out_shape.append(jax.ShapeDtypeStruct(dst_shape, BF16))
        (dst_operand,) = [k for k, op in enumerate(operands[:N_MAIN_IN]) if op is dst]
        aliases[dst_operand] = N_MAIN_OUT + n
    outs = pl.pallas_call(
        functools.partial(_prompt_layer_kernel, layer=l, tile=tile, n_mem=n_mem, n_tiles=nt, n_cast=len(casts)),
        grid=(batch, nt),
        in_specs=in_specs,
        out_specs=out_specs,
        out_shape=out_shape,
        input_output_aliases=aliases,
        scratch_shapes=[
            pltpu.VMEM((WINDOW, KV_DIM), F32),
            pltpu.VMEM((WINDOW, KV_DIM), F32),
            pltpu.VMEM((SUBLANES + tile, CONV_DIM), F32),
        ],
        compiler_params=pltpu.CompilerParams(
            dimension_semantics=("arbitrary", "arbitrary"), vmem_limit_bytes=VMEM_LIMIT),
        name="prompt_layer",
    )(*operands)
    main, cast_out = outs[:N_MAIN_OUT], outs[N_MAIN_OUT:]
    if l + 1 < DEPTH:
        w_in_bf, w_out_bf = cast_out
    return (*main, w_in_bf, w_out_bf)


def _sample_kernel(x_ref, gpre_ref, gpost_ref, win_ref, wout_ref, convw_ref, sink_ref, convp_ref,
                   kt_ref, vt_ref, mkt_ref, mvt_ref,
                   y_ref, kto_ref, vto_ref, uo_ref,
                   xs_ref, z_ref, mix_ref, mk_buf, mv_buf, kt_buf, vt_buf, mem_sem, *, group, t, n_mem, n_sub, n_proj):
    l = pl.program_id(0)
    p = pl.program_id(1)
    ss = pl.program_id(2)
    nrows = group * t
    prows = n_sub * nrows
    rows_p = pl.ds(pl.multiple_of(p * prows, prows), prows)
    rows_s = pl.ds(pl.multiple_of(ss * nrows, nrows), nrows)

    groups_per_layer = n_proj * n_sub
    n_steps = DEPTH * groups_per_layer
    step = (l * n_proj + p) * n_sub + ss

    def mem_copies(at_step):
        layer = at_step // groups_per_layer
        first_seq = (at_step % groups_per_layer) * group
        slot = at_step % MEM_SLOTS
        streams = ((mkt_ref, mk_buf), (mvt_ref, mv_buf), (kt_ref, kt_buf), (vt_ref, vt_buf))
        return [pltpu.make_async_copy(src.at[layer, pl.ds(first_seq, group)], dst.at[slot], mem_sem.at[n, slot])
                for n, (src, dst) in enumerate(streams)]

    @pl.when(step == 0)
    def _():
        for ahead in range(MEM_SLOTS - 1):
            for n, copy in enumerate(mem_copies(ahead)):
                copy.start(priority=n % 2)

    @pl.when(step + MEM_SLOTS - 1 < n_steps)
    def _():
        for n, copy in enumerate(mem_copies(step + MEM_SLOTS - 1)):
            copy.start(priority=n % 2)

    for copy in mem_copies(step):
        copy.wait()
    mem_slot = step % MEM_SLOTS

    @pl.when(ss == 0)
    def _():
        @pl.when(l == 0)
        def _():
            xs_ref[rows_p, :] = x_ref[...]

        h = _rmsnorm(xs_ref[rows_p, :], gpre_ref[pl.ds(l, 1), :]).astype(BF16)
        z_ref[...] = _dot(h, win_ref[...])

    def zc(off, width):
        return z_ref[rows_s, off:off + width]

    def per_seq(a):
        return a.reshape(group, t, a.shape[-1])

    u2d = zc(OFF_CC, CONV_DIM) * zc(OFF_CH, CONV_DIM)
    uo_ref[...] = u2d
    u = per_seq(u2d)
    prev = convp_ref[...]
    tpos = lax.broadcasted_iota(jnp.int32, (group, t, CONV_DIM), 1)
    u1 = jnp.where(tpos >= 1, pltpu.roll(u, 1, 1), pltpu.roll(prev, 1, 1))
    u2 = jnp.where(tpos >= 2, pltpu.roll(u, 2, 1), pltpu.roll(prev, 2, 1))
    cw = convw_ref[...]
    conv = cw[0:1, :] * u2 + cw[1:2, :] * u1 + cw[2:3, :] * u
    out_a = per_seq(_silu(zc(OFF_CG, CONV_DIM)) * zc(OFF_CB, CONV_DIM)) * conv

    kt_old = kt_buf[mem_slot]
    vt_old = vt_buf[mem_slot]
    k_new = zc(OFF_K, KV_DIM)
    v_new = zc(OFF_V, KV_DIM)
    k_new_t = jnp.swapaxes(per_seq(k_new), 1, 2)
    v_new_t = jnp.swapaxes(per_seq(v_new), 1, 2)
    kto_ref[...] = pltpu.roll(jnp.concatenate([k_new_t, kt_old[:, :, t:]], axis=2), WINDOW - t, 2)
    vto_ref[...] = pltpu.roll(jnp.concatenate([v_new_t, vt_old[:, :, t:]], axis=2), WINDOW - t, 2)

    q = per_seq(zc(OFF_Q, ATTN_DIM) * SCALE)
    low_q = _low_lanes((group, t, LANES))
    pieces = []
    for c in range(N_CHUNK):
        qc = q[:, :, c * LANES:(c + 1) * LANES]
        qs = _swap_halves(qc)
        if c // CHUNKS_PER_KV == 0:
            pieces += [jnp.where(low_q, qc, 0.0), jnp.where(low_q, qs, 0.0)]
        else:
            pieces += [jnp.where(low_q, 0.0, qs), jnp.where(low_q, 0.0, qc)]
    qbd = jnp.concatenate(pieces, axis=1).astype(BF16)
    nrow = ATTN_HEADS * t
    s_old = jnp.einsum('gqd,gdk->gqk', qbd, kt_old.astype(BF16), preferred_element_type=F32)
    pad_rows = jnp.zeros((WINDOW - group * t, KV_DIM), F32)
    k_new_w = jnp.concatenate([k_new, pad_rows], axis=0) if group * t < WINDOW else k_new
    v_new_w = jnp.concatenate([v_new, pad_rows], axis=0) if group * t < WINDOW else v_new
    s_new = _dot_nt(qbd.reshape(group * nrow, KV_DIM), k_new_w.astype(BF16)).reshape(group, nrow, WINDOW)
    tq = lax.broadcasted_iota(jnp.int32, (group, nrow, WINDOW), 1) % t
    col = lax.broadcasted_iota(jnp.int32, (group, nrow, WINDOW), 2)
    first_new = lax.broadcasted_iota(jnp.int32, (group, nrow, WINDOW), 0) * t
    s_old = jnp.where(col > tq, s_old, -jnp.inf)
    s_new = jnp.where((col >= first_new) & (col <= first_new + tq), s_new, -jnp.inf)
    sink = sink_ref[...]
    m = jnp.maximum(jnp.max(jnp.maximum(s_old, s_new), axis=2, keepdims=True), sink)
    p_old = jnp.exp(s_old - m)
    p_new = jnp.exp(s_new - m)
    d = jnp.sum(p_old + p_new, axis=2, keepdims=True) + jnp.exp(sink - m)
    o = jnp.einsum('gqk,gdk->gqd', p_old.astype(BF16), vt_old.astype(BF16), preferred_element_type=F32)
    o = o + _dot(p_new.astype(BF16).reshape(group * nrow, WINDOW), v_new_w.astype(BF16)).reshape(group, nrow, KV_DIM)
    o = o * (1.0 / d)
    ob_chunks = []
    for c in range(N_CHUNK):
        o_even = o[:, 2 * c * t:(2 * c + 1) * t]
        o_odd = o[:, (2 * c + 1) * t:(2 * c + 2) * t]
        if c // CHUNKS_PER_KV == 0:
            ob_chunks.append(jnp.where(low_q, o_even, _swap_halves(o_odd)))
        else:
            ob_chunks.append(jnp.where(low_q, _swap_halves(o_even), o_odd))
    o_b = jnp.concatenate(ob_chunks, axis=2)
    out_b = per_seq(_silu(zc(OFF_AG, ATTN_DIM))) * o_b

    mq = per_seq(zc(OFF_MQ, MEM_DIM) * SCALE)
    head_of_lane = lax.broadcasted_iota(jnp.int32, (group, t, MEM_DIM), 2) // HEAD_DIM
    qm = jnp.concatenate([jnp.where(head_of_lane == hh, mq, 0.0) for hh in range(MEM_HEADS)],
                         axis=1).astype(BF16)
    s = jnp.einsum('gqd,gdk->gqk', qm, mk_buf[mem_slot].astype(BF16), preferred_element_type=F32)
    p = jnp.exp(s - jnp.max(s, axis=2, keepdims=True))
    d = jnp.sum(p, axis=2, keepdims=True)
    o = jnp.einsum('gqk,gdk->gqd', p.astype(BF16), mv_buf[mem_slot].astype(BF16), preferred_element_type=F32)
    o = o * (1.0 / d)
    o_c = jnp.zeros((group, t, MEM_DIM), F32)
    for hh in range(MEM_HEADS):
        o_c = jnp.where(head_of_lane == hh, o[:, hh * t:(hh + 1) * t], o_c)
    out_c = per_seq(_silu(zc(OFF_MG, MEM_DIM))) * o_c

    mix_ref[rows_s, :] = jnp.concatenate([out_a, out_b, out_c], axis=2).reshape(nrows, MIX_DIM).astype(BF16)

    @pl.when(ss == n_sub - 1)
    def _():
        y = _dot(mix_ref[...], wout_ref[...])
        x_new = xs_ref[rows_p, :] + _rmsnorm(y, gpost_ref[pl.ds(l, 1), :])
        xs_ref[rows_p, :] = x_new

        @pl.when(l == DEPTH - 1)
        def _():
            y_ref[...] = x_new


def _sample_stream(x2d, norm_pre, norm_post, w_in_bf, w_out_bf, conv_w, sink_rows, convp, cache_kt, cache_vt,
                   cache_mkt, cache_mvt, n_seq, t, n_mem):
    group = SAMPLE_GROUP
    n_sub = SAMPLE_SUBSTEPS
    n_proj = n_seq // (group * n_sub)
    nrows = group * t
    prows = n_sub * nrows
    nrow_attn = ATTN_HEADS * t
    per_layer = lambda l, p, ss: (l, 0, 0)
    per_group = lambda l, p, ss: (l, p * n_sub + ss, 0, 0)
    return pl.pallas_call(
        functools.partial(_sample_kernel, group=group, t=t, n_mem=n_mem, n_sub=n_sub, n_proj=n_proj),
        grid=(DEPTH, n_proj, n_sub),
        in_specs=[
            pl.BlockSpec((prows, D_MODEL), lambda l, p, ss: (jnp.where(l == 0, p, n_proj - 1), 0)),
            pl.BlockSpec((DEPTH, D_MODEL), lambda l, p, ss: (0, 0)),
            pl.BlockSpec((DEPTH, D_MODEL), lambda l, p, ss: (0, 0)),
            pl.BlockSpec((None, D_MODEL, IN_DIM), per_layer),
            pl.BlockSpec((None, MIX_DIM, D_MODEL), per_layer),
            pl.BlockSpec((None, CONV_W, CONV_DIM), per_layer),
            pl.BlockSpec((None, nrow_attn, 1), per_layer),
            pl.BlockSpec((None, group, t, CONV_DIM), per_group),
            pl.BlockSpec(memory_space=pl.ANY),
            pl.BlockSpec(memory_space=pl.ANY),
            pl.BlockSpec(memory_space=pl.ANY),
            pl.BlockSpec(memory_space=pl.ANY),
        ],
        out_specs=[
            pl.BlockSpec((prows, D_MODEL), lambda l, p, ss: (jnp.where(l == DEPTH - 1, p, 0), 0)),
            pl.BlockSpec((None, group, KV_DIM, WINDOW), per_group),
            pl.BlockSpec((None, group, KV_DIM, WINDOW), per_group),
            pl.BlockSpec((None, nrows, CONV_DIM), lambda l, p, ss: (l, p * n_sub + ss, 0)),
        ],
        out_shape=[
            jax.ShapeDtypeStruct((n_seq * t, D_MODEL), F32),
            jax.ShapeDtypeStruct((DEPTH, n_seq, KV_DIM, WINDOW), F32),
            jax.ShapeDtypeStruct((DEPTH, n_seq, KV_DIM, WINDOW), F32),
            jax.ShapeDtypeStruct((DEPTH, n_seq * t, CONV_DIM), F32),
        ],
        scratch_shapes=[
            pltpu.VMEM((n_seq * t, D_MODEL), F32),
            pltpu.VMEM((prows, IN_DIM), F32),
            pltpu.VMEM((prows, MIX_DIM), BF16),
            pltpu.VMEM((MEM_SLOTS, group, MEM_DIM, n_mem), F32),
            pltpu.VMEM((MEM_SLOTS, group, MEM_DIM, n_mem), F32),
            pltpu.VMEM((MEM_SLOTS, group, KV_DIM, WINDOW), F32),
            pltpu.VMEM((MEM_SLOTS, group, KV_DIM, WINDOW), F32),
            pltpu.SemaphoreType.DMA((4, MEM_SLOTS)),
        ],
        compiler_params=pltpu.CompilerParams(
            dimension_semantics=("arbitrary", "arbitrary", "arbitrary"), vmem_limit_bytes=VMEM_LIMIT),
        name="sample_stream",
    )(x2d, norm_pre, norm_post, w_in_bf, w_out_bf, conv_w, sink_rows, convp, cache_kt, cache_vt, cache_mkt, cache_mvt)


def _keys_minor(a):
    lead = a.shape[:-3]
    n, heads, hd = a.shape[-3:]
    nd = a.ndim
    perm = tuple(range(nd - 3)) + (nd - 2, nd - 1, nd - 3)
    return jnp.transpose(a, perm).reshape(*lead, heads * hd, n)


def _keys_major(a, heads):
    lead = a.shape[:-2]
    n = a.shape[-1]
    a = a.reshape(*lead, heads, HEAD_DIM, n)
    nd = a.ndim
    perm = tuple(range(nd - 3)) + (nd - 1, nd - 3, nd - 2)
    return jnp.transpose(a, perm)


def kernel(x_prompt, x_sample, mem_prompt, cache_win_k, cache_win_v, state_conv, cache_mem_k, cache_mem_v,
           norm_pre, norm_post, norm_mem, w_in, conv_w, attn_sinks, w_mem_kv, w_out):
    batch, seq, _ = x_prompt.shape
    n_seq, t, _ = x_sample.shape
    n_mem = mem_prompt.shape[1]
    assert seq % PROMPT_TILE == 0 and sum(PROMPT_PART_ROWS) == PROMPT_TILE
    assert all(rows % WINDOW == 0 for rows in PROMPT_PART_ROWS)
    assert n_seq % (SAMPLE_GROUP * SAMPLE_SUBSTEPS) == 0 and t == SUBLANES and SAMPLE_GROUP * t <= WINDOW

    sink_rows = jnp.repeat(attn_sinks, t, axis=1)[:, :, None]

    mkt, mvt, kcat_t, vcat, w_in_bf, w_out_bf = _memkv(
        mem_prompt.reshape(batch * n_mem, D_MODEL), norm_mem, w_mem_kv, w_in, w_out, batch, n_mem)

    xp = x_prompt
    ktp, vtp, cvp = [], [], []
    for l in range(DEPTH):
        xp, k_p, v_p, c_p, w_in_bf, w_out_bf = _prompt_layer(
            l, xp, attn_sinks, norm_pre, norm_post, w_in_bf, conv_w, kcat_t, vcat, w_out_bf, w_in, w_out, n_mem)
        ktp.append(k_p)
        vtp.append(v_p)
        cvp.append(c_p[:, SUBLANES - CONV_BUF:, :])

    convp = jnp.pad(state_conv, ((0, 0), (0, 0), (t - CONV_BUF, 0), (0, 0)))
    ys, kts, vts, us = _sample_stream(
        x_sample.reshape(n_seq * t, D_MODEL), norm_pre, norm_post, w_in_bf, w_out_bf, conv_w, sink_rows, convp,
        _keys_minor(cache_win_k), _keys_minor(cache_win_v), _keys_minor(cache_mem_k), _keys_minor(cache_mem_v),
        n_seq, t, n_mem)

    return (xp,
            ys.reshape(n_seq, t, D_MODEL),
            _keys_major(jnp.stack(ktp), KV_HEADS),
            _keys_major(jnp.stack(vtp), KV_HEADS),
            jnp.stack(cvp),
            _keys_major(mkt, MEM_HEADS),
            _keys_major(mvt, MEM_HEADS),
            _keys_major(kts, KV_HEADS),
            _keys_major(vts, KV_HEADS),
            us.reshape(DEPTH, n_seq, t, CONV_DIM)[:, :, t - CONV_BUF:, :])
```
